```python
import functools
import jax, jax.numpy as jnp
from jax import lax
import numpy as np

D_MODEL = 2048
BATCH = 16
SEQ = 2048
DEPTH = 2
DEC_BATCH = 16
DEC_SEQ = 32
PAST_LEN = 4096

CHUNK = 64
M_HEADS = 4
M_DK = 128
M_DV = 256
M_WIDTH = M_HEADS * M_DV
A_HEADS = 16
A_KV_HEADS = 4
A_HEAD_DIM = 64
A_GROUP = A_HEADS // A_KV_HEADS
A_WIDTH = A_HEADS * A_HEAD_DIM
WINDOW = 128
N_WIN_CHUNKS = WINDOW // CHUNK
N_BUCKETS = 32
MAX_DISTANCE = 128
D_FF = 4 * D_MODEL
EPS = 1e-6
NEG_INF = -1e30
SPLITS = (M_HEADS * M_DK, M_HEADS * M_DK, M_WIDTH, M_WIDTH, M_HEADS, M_HEADS,
          A_WIDTH, A_KV_HEADS * A_HEAD_DIM, A_KV_HEADS * A_HEAD_DIM)
D_IN = sum(SPLITS)

kernel_name = 'hymba_mlstm_swa_sink_stream_step'


def rms_norm(x, g):
    xf = x.astype(jnp.float32)
    y = xf * lax.rsqrt(jnp.mean(xf * xf, axis=-1, keepdims=True) + EPS)
    return (y * g.astype(jnp.float32)).astype(x.dtype)


def t5_bucket(rel):
    half = N_BUCKETS // 2
    exact = half // 2
    n = np.abs(rel)
    large = exact + (np.log(np.maximum(n, 1) / exact) / np.log(MAX_DISTANCE / exact) * (half - exact)).astype(np.int32)
    large = np.minimum(large, half - 1)
    return (rel > 0).astype(np.int32) * half + np.where(n < exact, n, large).astype(np.int32)


def band_bias(rel_bias, n_prev, lq, lk):
    rel = (np.arange(lk)[None, :] - n_prev) - np.arange(lq)[:, None]
    b = rel_bias[jnp.asarray(t5_bucket(rel))]
    return b.transpose(2, 0, 1).reshape(A_KV_HEADS, A_GROUP, lq, lk).astype(jnp.float32)


def sink_attention(q, k, v, bias, sink, valid=None):
    s = jnp.einsum('...qhgd,...khd->...hgqk', q, k).astype(jnp.float32) * A_HEAD_DIM ** -0.5 + bias
    if valid is not None:
        s = jnp.where(valid, s, NEG_INF)
    sk = sink.astype(jnp.float32).reshape(A_KV_HEADS, A_GROUP, 1)
    mx = jnp.maximum(s.max(-1), sk)
    p = jnp.exp(s - mx[..., None])
    p = p / (p.sum(-1) + jnp.exp(sk - mx))[..., None]
    return jnp.einsum('...hgqk,...khd->...qhgd', p.astype(v.dtype), v)


def attn_prompt(q, k, v, rel_bias, sink):
    B, S = q.shape[:2]
    nc = S // CHUNK
    lk = WINDOW + CHUNK
    qc = q.reshape(B, nc, CHUNK, A_KV_HEADS, A_GROUP, A_HEAD_DIM)

    def band(a):
        ap = jnp.pad(a, ((0, 0), (WINDOW, 0), (0, 0), (0, 0))).reshape(B, nc + N_WIN_CHUNKS, CHUNK, A_KV_HEADS, A_HEAD_DIM)
        return jnp.concatenate([ap[:, w:w + nc] for w in range(N_WIN_CHUNKS + 1)], axis=2)

    key_pos = np.arange(nc)[:, None] * CHUNK + np.arange(lk)[None, :] - WINDOW
    valid = jnp.asarray(key_pos >= 0)[:, None, None, None, :]
    o = sink_attention(qc, band(k), band(v), band_bias(rel_bias, WINDOW, CHUNK, lk), sink, valid)
    return o.reshape(B, S, A_WIDTH), (k[:, -WINDOW:], v[:, -WINDOW:])


def attn_sample(q, k, v, ck, cv, rel_bias, sink):
    B, L = q.shape[:2]
    n_win = ck.shape[1]
    kk = jnp.concatenate([ck.astype(k.dtype), k], axis=1)
    vv = jnp.concatenate([cv.astype(v.dtype), v], axis=1)
    o = sink_attention(q.reshape(B, L, A_KV_HEADS, A_GROUP, A_HEAD_DIM), kk, vv,
                       band_bias(rel_bias, n_win, L, n_win + L), sink)
    return o.reshape(B, L, A_WIDTH), (kk[:, -n_win:], vv[:, -n_win:])


def mlstm_chunk(carry, xs):
    C, n, m = carry
    q, k, v, ig, logf = xs
    L = q.shape[1]
    b = jnp.cumsum(logf, axis=1).transpose(0, 2, 1)
    igh = ig.transpose(0, 2, 1)
    causal = jnp.tril(jnp.ones((L, L), dtype=bool))
    D = jnp.where(causal, b[..., :, None] - b[..., None, :] + igh[..., None, :], NEG_INF)
    inter = b + m[..., None]
    m_t = jnp.maximum(inter, D.max(-1))
    w = jnp.exp(D - m_t[..., None]) * jnp.einsum('blhk,bshk->bhls', q, k)
    g = jnp.exp(inter - m_t)
    num = jnp.einsum('bhls,bshv->blhv', w, v) + jnp.einsum('bhl,bhvk,blhk->blhv', g, C, q)
    den = w.sum(-1) + g * jnp.einsum('bhk,blhk->bhl', n, q)
    h = num / jnp.maximum(jnp.abs(den), jnp.exp(-m_t)).transpose(0, 2, 1)[..., None]
    m_end = m_t[..., -1]
    wk = jnp.exp(b[..., -1:] - b + igh - m_end[..., None])
    decay = jnp.exp(inter[..., -1] - m_end)
    C_new = decay[..., None, None] * C + jnp.einsum('bhs,bshv,bshk->bhvk', wk, v, k)
    n_new = decay[..., None] * n + jnp.einsum('bhs,bshk->bhk', wk, k)
    return (C_new, n_new, m_end), h


def mlstm_prompt(q, k, v, ig, logf):
    B, S = q.shape[:2]
    nc = S // CHUNK

    def chunks(a):
        return a.reshape(B, nc, CHUNK, *a.shape[2:]).swapaxes(0, 1)

    init = (jnp.zeros((B, M_HEADS, M_DV, M_DK), jnp.float32),
            jnp.zeros((B, M_HEADS, M_DK), jnp.float32),
            jnp.zeros((B, M_HEADS), jnp.float32))
    state, h = lax.scan(mlstm_chunk, init, (chunks(q), chunks(k), chunks(v), chunks(ig), chunks(logf)))
    return h.swapaxes(0, 1).reshape(B, S, M_HEADS, M_DV), state


def mlstm_sample(q, k, v, ig, logf, C, n, m):
    f32 = jnp.float32
    state, h = mlstm_chunk((C.astype(f32), n.astype(f32), m.astype(f32)), (q, k, v, ig, logf))
    return h, state


def trunk_layer(x, attn_fn, mlstm_fn, g_mix, w_in, b_i, b_f, g_q, g_k, g_mo, g_ao, w_out, g_ffn, w_up, w_down):
    B, L, _ = x.shape
    f32 = jnp.float32
    h = rms_norm(x, g_mix)
    mq, mk, mv, mo, mi, mf, aq, ak, av = jnp.split(h @ w_in, np.cumsum(SPLITS)[:-1].tolist(), axis=-1)
    hm, m_state = mlstm_fn(mq.reshape(B, L, M_HEADS, M_DK).astype(f32),
                           mk.reshape(B, L, M_HEADS, M_DK).astype(f32) * M_DK ** -0.5,
                           mv.reshape(B, L, M_HEADS, M_DV).astype(f32),
                           (mi + b_i).astype(f32),
                           jax.nn.log_sigmoid((mf + b_f).astype(f32)))
    hm = rms_norm(hm, g_mo.reshape(M_HEADS, M_DV)).reshape(B, L, M_WIDTH).astype(x.dtype) * jax.nn.sigmoid(mo)
    ha, kv_state = attn_fn(rms_norm(aq.reshape(B, L, A_HEADS, A_HEAD_DIM), g_q),
                           rms_norm(ak.reshape(B, L, A_KV_HEADS, A_HEAD_DIM), g_k),
                           av.reshape(B, L, A_KV_HEADS, A_HEAD_DIM))
    ha = rms_norm(ha, g_ao)
    x = x + jnp.concatenate([hm, ha], axis=-1) @ w_out
    u = rms_norm(x, g_ffn) @ w_up
    x = x + jnp.square(jax.nn.relu(u)) @ w_down
    return x, kv_state, m_state


def setup_inputs(seed: int = 0) -> dict:
    key = jax.random.key(seed)
    ks = jax.random.split(key, 24)
    n_win = min(WINDOW, PAST_LEN)

    def nrm(k, shape, scale):
        return scale * jax.random.normal(k, shape, jnp.float32)

    def gain(k, shape):
        return 1.0 + 0.05 * jax.random.normal(k, shape, jnp.float32)

    return {
        'x_prompt': nrm(ks[0], (BATCH, SEQ, D_MODEL), 1.0),
        'x_sample': nrm(ks[1], (DEC_BATCH, DEC_SEQ, D_MODEL), 1.0),
        'cache_k': nrm(ks[2], (DEPTH, DEC_BATCH, n_win, A_KV_HEADS, A_HEAD_DIM), 1.0),
        'cache_v': nrm(ks[3], (DEPTH, DEC_BATCH, n_win, A_KV_HEADS, A_HEAD_DIM), 1.0),
        'state_C': nrm(ks[4], (DEPTH, DEC_BATCH, M_HEADS, M_DV, M_DK), 0.5),
        'state_n': nrm(ks[5], (DEPTH, DEC_BATCH, M_HEADS, M_DK), 0.5),
        'state_m': nrm(ks[6], (DEPTH, DEC_BATCH, M_HEADS), 1.0),
        'rel_bias': nrm(ks[7], (N_BUCKETS, A_HEADS), 0.5),
        'g_mix': gain(ks[8], (DEPTH, D_MODEL)),
        'w_in': nrm(ks[9], (DEPTH, D_MODEL, D_IN), D_MODEL ** -0.5),
        'b_i': nrm(ks[10], (DEPTH, M_HEADS), 0.1),
        'b_f': 3.0 + nrm(ks[11], (DEPTH, M_HEADS), 0.5),
        'g_q': gain(ks[12], (DEPTH, A_HEAD_DIM)),
        'g_k': gain(ks[13], (DEPTH, A_HEAD_DIM)),
        'sinks': nrm(ks[14], (DEPTH, A_HEADS), 0.5),
        'g_mo': gain(ks[15], (DEPTH, M_WIDTH)),
        'g_ao': gain(ks[16], (DEPTH, A_WIDTH)),
        'w_out': nrm(ks[17], (DEPTH, M_WIDTH + A_WIDTH, D_MODEL), (M_WIDTH + A_WIDTH) ** -0.5),
        'g_ffn': gain(ks[18], (DEPTH, D_MODEL)),
        'w_up': nrm(ks[19], (DEPTH, D_MODEL, D_FF), D_MODEL ** -0.5),
        'w_down': nrm(ks[20], (DEPTH, D_FF, D_MODEL), D_FF ** -0.5),
    }


def reference(x_prompt, x_sample, cache_k, cache_v, state_C, state_n, state_m, rel_bias,
              g_mix, w_in, b_i, b_f, g_q, g_k, sinks, g_mo, g_ao, w_out, g_ffn, w_up, w_down):
    xp, xs = x_prompt, x_sample
    p_k, p_v, p_C, p_n, p_m = [], [], [], [], []
    s_k, s_v, s_C, s_n, s_m = [], [], [], [], []
    for l in range(DEPTH):
        w = (g_mix[l], w_in[l], b_i[l], b_f[l], g_q[l], g_k[l], g_mo[l], g_ao[l], w_out[l], g_ffn[l], w_up[l], w_down[l])
        xp, (k_, v_), (C_, n_, m_) = trunk_layer(
            xp, functools.partial(attn_prompt, rel_bias=rel_bias, sink=sinks[l]), mlstm_prompt, *w)
        p_k.append(k_); p_v.append(v_); p_C.append(C_); p_n.append(n_); p_m.append(m_)
        xs, (k_, v_), (C_, n_, m_) = trunk_layer(
            xs,
            functools.partial(attn_sample, ck=cache_k[l], cv=cache_v[l], rel_bias=rel_bias, sink=sinks[l]),
            functools.partial(mlstm_sample, C=state_C[l], n=state_n[l], m=state_m[l]), *w)
        s_k.append(k_); s_v.append(v_); s_C.append(C_); s_n.append(n_); s_m.append(m_)
    return (xp, xs,
            jnp.stack(p_k), jnp.stack(p_v), jnp.stack(p_C), jnp.stack(p_n), jnp.stack(p_m),
            jnp.stack(s_k), jnp.stack(s_v), jnp.stack(s_C), jnp.stack(s_n), jnp.stack(s_m))
```

```python
import functools

import jax
import jax.numpy as jnp
import numpy as np
from jax import lax
from jax.experimental import pallas as pl
from jax.experimental.pallas import tpu as pltpu

F32 = jnp.float32
BF16 = jnp.bfloat16

CHUNK = 64
M_HEADS = 4
M_DK = 128
M_DV = 256
M_WIDTH = M_HEADS * M_DV
A_HEADS = 16
A_KV_HEADS = 4
A_HEAD_DIM = 64
A_GROUP = A_HEADS // A_KV_HEADS
A_WIDTH = A_HEADS * A_HEAD_DIM
A_KV_WIDTH = A_KV_HEADS * A_HEAD_DIM
WINDOW = 128
N_BUCKETS = 32
MAX_DISTANCE = 128
EPS = 1e-6
NEG_INF = -1e30
M_SCALE = M_DK ** -0.5
A_SCALE = A_HEAD_DIM ** -0.5

QK_W = M_HEADS * M_DK
PROJ_W = 2 * QK_W + 2 * M_WIDTH + A_WIDTH + 2 * A_KV_WIDTH
GATE_W = 128

V7X_VMEM_LIMIT = 56 * 1024 * 1024


def _params(sem, vmem=V7X_VMEM_LIMIT):
    return pltpu.CompilerParams(dimension_semantics=sem, vmem_limit_bytes=vmem)


def _row_tile(m, cap):
    t = min(m, cap)
    while m % t:
        t //= 2
    return t


def _t5_bucket(rel):
    half = N_BUCKETS // 2
    exact = half // 2
    n = np.abs(rel)
    large = exact + (np.log(np.maximum(n, 1) / exact) / np.log(MAX_DISTANCE / exact) * (half - exact)).astype(np.int32)
    large = np.minimum(large, half - 1)
    return (rel > 0).astype(np.int32) * half + np.where(n < exact, n, large).astype(np.int32)


def _bias_kernel(rel_ref, map_ref, out_ref):
    bmap = map_ref[...]
    for h in range(A_HEADS):
        acc = jnp.zeros(bmap.shape, F32)
        for b in range(N_BUCKETS):
            acc = jnp.where(bmap == b, rel_ref[b, h], acc)
        out_ref[h] = acc


def _bias_table(rel_bias):
    lk = WINDOW + CHUNK
    rel = (np.arange(lk)[None, :] - WINDOW) - np.arange(CHUNK)[:, None]
    bmap = jnp.asarray(_t5_bucket(rel), jnp.int32)
    return pl.pallas_call(
        _bias_kernel,
        out_shape=jax.ShapeDtypeStruct((A_HEADS, CHUNK, lk), F32),
        in_specs=[pl.BlockSpec(memory_space=pltpu.SMEM),
                  pl.BlockSpec(memory_space=pltpu.VMEM)],
        out_specs=pl.BlockSpec(memory_space=pltpu.VMEM),
        name="t5_bias_table",
    )(rel_bias, bmap)


def _rms(x, g):
    return x * lax.rsqrt(jnp.mean(x * x, axis=-1, keepdims=True) + EPS) * g


def _in_proj_kernel(x_ref, g_ref, w_ref, wg_ref, o_ref, gate_ref, h_scr):
    @pl.when(pl.program_id(1) == 0)
    def _():
        h = _rms(x_ref[...], g_ref[...]).astype(BF16)
        h_scr[...] = h
        gate_ref[...] = jnp.dot(h, wg_ref[...], preferred_element_type=F32)

    o_ref[...] = jnp.dot(h_scr[...], w_ref[...], preferred_element_type=F32).astype(o_ref.dtype)


def _in_proj(x, g_mix, w_main, w_gate, layer):
    m, d = x.shape
    tm = _row_tile(m, 1024)
    tn = 768
    return pl.pallas_call(
        _in_proj_kernel,
        out_shape=(jax.ShapeDtypeStruct((m, PROJ_W), BF16), jax.ShapeDtypeStruct((m, GATE_W), F32)),
        grid=(m // tm, PROJ_W // tn),
        in_specs=[pl.BlockSpec((tm, d), lambda i, j: (i, 0)),
                  pl.BlockSpec((None, 1, d), lambda i, j: (layer, 0, 0)),
                  pl.BlockSpec((None, d, tn), lambda i, j: (layer, 0, j)),
                  pl.BlockSpec((None, d, GATE_W), lambda i, j: (layer, 0, 0))],
        out_specs=(pl.BlockSpec((tm, tn), lambda i, j: (i, j)),
                   pl.BlockSpec((tm, GATE_W), lambda i, j: (i, 0))),
        scratch_shapes=[pltpu.VMEM((tm, d), BF16)],
        compiler_params=_params(("parallel", "arbitrary")),
        name="in_proj",
    )(x, g_mix, w_main, w_gate)


def _mlstm_kernel(q_ref, k_ref, v_ref, og_ref, gate_ref, gbias_ref, gmo_ref, c0_ref, n0_ref, m0_ref,
                  hm_ref, cout_ref, nout_ref, mout_ref, ct_scr, n_scr, m_scr, *, L):
    c = pl.program_id(1)
    last = pl.num_programs(1) - 1

    @pl.when(c == 0)
    def _():
        for h in range(M_HEADS):
            ct_scr[h] = c0_ref[0, h].T
            m_scr[h:h + 1, :] = jnp.broadcast_to(m0_ref[0, :, h:h + 1], (1, 128))
        n_scr[...] = n0_ref[0]

    a_all = gate_ref[0] + gbias_ref[...]
    logf = jnp.minimum(a_all, 0.0) - jnp.log(1.0 + jnp.exp(-jnp.abs(a_all)))
    row = lax.broadcasted_iota(jnp.int32, (L, GATE_W), 0)
    b_all = logf
    k = 1
    while k < L:
        b_all = b_all + jnp.where(row >= k, pltpu.roll(b_all, k, axis=0), 0.0)
        k *= 2
    a_t = a_all.T
    b_t = b_all.T
    tri = lax.broadcasted_iota(jnp.int32, (L, L), 0) >= lax.broadcasted_iota(jnp.int32, (L, L), 1)

    for h in range(M_HEADS):
        b_col = b_all[:, M_HEADS + h:M_HEADS + h + 1]
        a_col = a_all[:, h:h + 1] - b_col
        a_row = a_t[h:h + 1, :] - b_t[M_HEADS + h:M_HEADS + h + 1, :]
        m_prev = m_scr[h:h + 1, 0:1]
        dm = jnp.where(tri, a_row, NEG_INF)
        m_run = jnp.maximum(jnp.max(dm, axis=1, keepdims=True), m_prev)
        decay_w = jnp.exp(dm - m_run)
        qh = q_ref[0, :, h * M_DK:(h + 1) * M_DK]
        kh = k_ref[0, :, h * M_DK:(h + 1) * M_DK]
        vh = v_ref[0, :, h * M_DV:(h + 1) * M_DV]
        k_t = kh.astype(F32).T.astype(BF16)
        s = jnp.dot(qh, k_t, preferred_element_type=F32) * M_SCALE
        w = decay_w * s
        g = jnp.exp(m_prev - m_run)
        ct = ct_scr[h]
        n_row = n_scr[h:h + 1, :]
        num = (jnp.dot(w.astype(BF16), vh, preferred_element_type=F32)
               + g * jnp.dot(qh, ct.astype(BF16), preferred_element_type=F32))
        den = (jnp.sum(w, axis=1, keepdims=True)
               + g * jnp.sum(qh.astype(F32) * n_row, axis=1, keepdims=True))
        m_t = b_col + m_run
        hh = num / jnp.maximum(jnp.abs(den), jnp.exp(-m_t))
        hn = _rms(hh, gmo_ref[:, h * M_DV:(h + 1) * M_DV])
        og = og_ref[0, :, h * M_DV:(h + 1) * M_DV].astype(F32)
        hm_ref[0, :, h * M_DV:(h + 1) * M_DV] = (hn * jax.nn.sigmoid(og)).astype(hm_ref.dtype)

        b_last = b_col[L - 1:L, :]
        m_end = b_last + m_run[L - 1:L, :]
        wk = jnp.exp(b_last + a_col - m_end)
        decay = jnp.exp(b_last + m_prev - m_end)
        vw = (vh.astype(F32) * wk).astype(BF16)
        ct_new = decay * ct + M_SCALE * jnp.dot(k_t, vw, preferred_element_type=F32)
        n_new = decay * n_row + M_SCALE * jnp.sum(kh.astype(F32) * wk, axis=0, keepdims=True)
        ct_scr[h] = ct_new
        n_scr[h:h + 1, :] = n_new
        m_scr[h:h + 1, :] = jnp.broadcast_to(m_end, (1, 128))

        @pl.when(c == last)
        def _():
            cout_ref[0, h] = ct_new.T
            nout_ref[0, h:h + 1, :] = n_new
            mout_ref[0, :, h:h + 1] = m_end


def _mlstm(proj, gates, gbias, g_mo, c0, n0, m0, layer, L):
    b, s, _ = proj.shape
    nc = s // L
    out_shape = (jax.ShapeDtypeStruct((b, s, M_WIDTH), BF16),
                 jax.ShapeDtypeStruct((b, M_HEADS, M_DV, M_DK), F32),
                 jax.ShapeDtypeStruct((b, M_HEADS, M_DK), F32),
                 jax.ShapeDtypeStruct((b, 1, M_HEADS), F32))
    return pl.pallas_call(
        functools.partial(_mlstm_kernel, L=L),
        out_shape=out_shape,
        grid=(b, nc),
        in_specs=[pl.BlockSpec((1, L, QK_W), lambda i, c: (i, c, 0)),
                  pl.BlockSpec((1, L, QK_W), lambda i, c: (i, c, 1)),
                  pl.BlockSpec((1, L, M_WIDTH), lambda i, c: (i, c, 1)),
                  pl.BlockSpec((1, L, M_WIDTH), lambda i, c: (i, c, 2)),
                  pl.BlockSpec((1, L, GATE_W), lambda i, c: (i, c, 0)),
                  pl.BlockSpec((None, 1, GATE_W), lambda i, c: (layer, 0, 0)),
                  pl.BlockSpec((None, 1, M_WIDTH), lambda i, c: (layer, 0, 0)),
                  pl.BlockSpec((1, M_HEADS, M_DV, M_DK), lambda i, c: (i, 0, 0, 0)),
                  pl.BlockSpec((1, M_HEADS, M_DK), lambda i, c: (i, 0, 0)),
                  pl.BlockSpec((1, 1, M_HEADS), lambda i, c: (i, 0, 0))],
        out_specs=(pl.BlockSpec((1, L, M_WIDTH), lambda i, c: (i, c, 0)),
                   pl.BlockSpec((1, M_HEADS, M_DV, M_DK), lambda i, c: (i, 0, 0, 0)),
                   pl.BlockSpec((1, M_HEADS, M_DK), lambda i, c: (i, 0, 0)),
                   pl.BlockSpec((1, 1, M_HEADS), lambda i, c: (i, 0, 0))),
        scratch_shapes=[pltpu.VMEM((M_HEADS, M_DK, M_DV), F32),
                        pltpu.VMEM((M_HEADS, M_DK), F32),
                        pltpu.VMEM((M_HEADS, 128), F32)],
        compiler_params=_params(("arbitrary", "arbitrary")),
        name="mlstm",
    )(proj, proj, proj, proj, gates, gbias, g_mo, c0, n0, m0)


def _swa_kernel(sink_ref, q_ref, k_ref, v_ref, ck_ref, cv_ref, gq_ref, gk_ref, gao_ref, bias_ref,
                ha_ref, cko_ref, cvo_ref, kbuf, vbuf, obuf, *, L, cache_len):
    c = pl.program_id(1)
    last = pl.num_programs(1) - 1
    lk = WINDOW + L

    @pl.when(c == 0)
    def _():
        kbuf[0:WINDOW, :] = ck_ref[0]
        vbuf[0:WINDOW, :] = cv_ref[0]

    gk = gk_ref[...]
    for kv in range(A_KV_HEADS):
        sl = slice(kv * A_HEAD_DIM, (kv + 1) * A_HEAD_DIM)
        kbuf[WINDOW:lk, sl] = _rms(k_ref[0, :, sl].astype(F32), gk)
    vbuf[WINDOW:lk, :] = v_ref[0].astype(F32)

    n_prev = jnp.minimum(c * L + cache_len, WINDOW)
    valid = lax.broadcasted_iota(jnp.int32, (A_GROUP * L, lk), 1) >= WINDOW - n_prev
    gq = gq_ref[...]
    for kv in range(A_KV_HEADS):
        sl = slice(kv * A_HEAD_DIM, (kv + 1) * A_HEAD_DIM)
        kb = kbuf[:, sl].astype(BF16)
        vb = vbuf[:, sl].astype(BF16)
        qs = []
        for g in range(A_GROUP):
            hsl = slice((kv * A_GROUP + g) * A_HEAD_DIM, (kv * A_GROUP + g + 1) * A_HEAD_DIM)
            qs.append(_rms(q_ref[0, :, hsl].astype(F32), gq).astype(BF16))
        qst = jnp.concatenate(qs, axis=0)
        s = lax.dot_general(qst, kb, (((1,), (1,)), ((), ())), preferred_element_type=F32)
        s = jnp.where(valid, s * A_SCALE + bias_ref[kv], NEG_INF)
        ps = []
        for g in range(A_GROUP):
            sg = s[g * L:(g + 1) * L, :]
            sk = sink_ref[kv * A_GROUP + g]
            mx = jnp.maximum(jnp.max(sg, axis=1, keepdims=True), sk)
            p = jnp.exp(sg - mx)
            p = p / (jnp.sum(p, axis=1, keepdims=True) + jnp.exp(sk - mx))
            ps.append(p.astype(BF16))
        o = jnp.dot(jnp.concatenate(ps, axis=0), vb, preferred_element_type=F32)
        for g in range(A_GROUP):
            hsl = slice((kv * A_GROUP + g) * A_HEAD_DIM, (kv * A_GROUP + g + 1) * A_HEAD_DIM)
            obuf[:, hsl] = o[g * L:(g + 1) * L, :]
    ha_ref[0] = _rms(obuf[...], gao_ref[...]).astype(ha_ref.dtype)

    k_win = kbuf[L:lk, :]
    v_win = vbuf[L:lk, :]
    kbuf[0:WINDOW, :] = k_win
    vbuf[0:WINDOW, :] = v_win

    @pl.when(c == last)
    def _():
        cko_ref[0] = k_win
        cvo_ref[0] = v_win


def _swa(proj, sinks, ck, cv, g_q, g_k, g_ao, bias, layer, L, cache_len):
    b, s, _ = proj.shape
    nc = s // L
    lk = WINDOW + L
    kv_blk = (2 * QK_W + 2 * M_WIDTH + A_WIDTH) // A_KV_WIDTH
    q_blk = (2 * QK_W + 2 * M_WIDTH) // A_WIDTH
    out_shape = (jax.ShapeDtypeStruct((b, s, A_WIDTH), BF16),
                 jax.ShapeDtypeStruct((b, WINDOW, A_KV_WIDTH), F32),
                 jax.ShapeDtypeStruct((b, WINDOW, A_KV_WIDTH), F32))
    return pl.pallas_call(
        functools.partial(_swa_kernel, L=L, cache_len=cache_len),
        out_shape=out_shape,
        grid=(b, nc),
        in_specs=[pl.BlockSpec(memory_space=pltpu.SMEM),
                  pl.BlockSpec((1, L, A_WIDTH), lambda i, c: (i, c, q_blk)),
                  pl.BlockSpec((1, L, A_KV_WIDTH), lambda i, c: (i, c, kv_blk)),
                  pl.BlockSpec((1, L, A_KV_WIDTH), lambda i, c: (i, c, kv_blk + 1)),
                  pl.BlockSpec((1, WINDOW, A_KV_WIDTH), lambda i, c: (i, 0, 0)),
                  pl.BlockSpec((1, WINDOW, A_KV_WIDTH), lambda i, c: (i, 0, 0)),
                  pl.BlockSpec((None, 1, A_HEAD_DIM), lambda i, c: (layer, 0, 0)),
                  pl.BlockSpec((None, 1, A_HEAD_DIM), lambda i, c: (layer, 0, 0)),
                  pl.BlockSpec((None, 1, A_WIDTH), lambda i, c: (layer, 0, 0)),
                  pl.BlockSpec((A_KV_HEADS, A_GROUP * L, lk), lambda i, c: (0, 0, 0))],
        out_specs=(pl.BlockSpec((1, L, A_WIDTH), lambda i, c: (i, c, 0)),
                   pl.BlockSpec((1, WINDOW, A_KV_WIDTH), lambda i, c: (i, 0, 0)),
                   pl.BlockSpec((1, WINDOW, A_KV_WIDTH), lambda i, c: (i, 0, 0))),
        scratch_shapes=[pltpu.VMEM((lk, A_KV_WIDTH), F32),
                        pltpu.VMEM((lk, A_KV_WIDTH), F32),
                        pltpu.VMEM((L, A_WIDTH), F32)],
        compiler_params=_params(("arbitrary", "arbitrary")),
        name="swa",
    )(sinks, proj, proj, proj, ck, cv, g_q, g_k, g_ao, bias)


def _out_proj_kernel(x_ref, hm_ref, ha_ref, wm_ref, wa_ref, o_ref):
    o_ref[...] = (x_ref[...]
                  + jnp.dot(hm_ref[...], wm_ref[...], preferred_element_type=F32)
                  + jnp.dot(ha_ref[...], wa_ref[...], preferred_element_type=F32))


def _out_proj(x, hm, ha, w_out, layer):
    m, d = x.shape
    tm = _row_tile(m, 512)
    return pl.pallas_call(
        _out_proj_kernel,
        out_shape=jax.ShapeDtypeStruct((m, d), F32),
        grid=(m // tm,),
        in_specs=[pl.BlockSpec((tm, d), lambda i: (i, 0)),
                  pl.BlockSpec((tm, M_WIDTH), lambda i: (i, 0)),
                  pl.BlockSpec((tm, A_WIDTH), lambda i: (i, 0)),
                  pl.BlockSpec((None, M_WIDTH, d), lambda i: (layer, 0, 0)),
                  pl.BlockSpec((None, A_WIDTH, d), lambda i: (layer, 1, 0))],
        out_specs=pl.BlockSpec((tm, d), lambda i: (i, 0)),
        compiler_params=_params(("parallel",)),
        name="out_proj",
    )(x, hm, ha, w_out, w_out)


def _ffn_kernel(x_ref, g_ref, wu_ref, wd_ref, o_ref, h_scr):
    f = pl.program_id(1)

    @pl.when(f == 0)
    def _():
        x = x_ref[...]
        h_scr[...] = _rms(x, g_ref[...]).astype(BF16)
        o_ref[...] = x

    u = jnp.dot(h_scr[...], wu_ref[...], preferred_element_type=F32)
    a = jnp.square(jnp.maximum(u, 0.0)).astype(BF16)
    o_ref[...] += jnp.dot(a, wd_ref[...], preferred_element_type=F32)


def _ffn(x, g_ffn, w_up, w_down, layer):
    m, d = x.shape
    d_ff = w_up.shape[-1]
    tm = _row_tile(m, 512)
    tf = 1024
    return pl.pallas_call(
        _ffn_kernel,
        out_shape=jax.ShapeDtypeStruct((m, d), F32),
        grid=(m // tm, d_ff // tf),
        in_specs=[pl.BlockSpec((tm, d), lambda i, f: (i, 0)),
                  pl.BlockSpec((None, 1, d), lambda i, f: (layer, 0, 0)),
                  pl.BlockSpec((None, d, tf), lambda i, f: (layer, 0, f)),
                  pl.BlockSpec((None, tf, d), lambda i, f: (layer, f, 0))],
        out_specs=pl.BlockSpec((tm, d), lambda i, f: (i, 0)),
        scratch_shapes=[pltpu.VMEM((tm, d), BF16)],
        compiler_params=_params(("parallel", "arbitrary")),
        name="ffn",
    )(x, g_ffn, w_up, w_down)


def _layer(x, wts, layer, L, bias, ck, cv, c0, n0, m0, cache_len):
    b, s, d = x.shape
    x2 = x.reshape(b * s, d)
    proj, gates = _in_proj(x2, wts["g_mix"], wts["w_main"], wts["w_gate"], layer)
    proj = proj.reshape(b, s, PROJ_W)
    gates = gates.reshape(b, s, GATE_W)
    hm, c_new, n_new, m_new = _mlstm(proj, gates, wts["gbias"], wts["g_mo"], c0, n0, m0, layer, L)
    ha, k_new, v_new = _swa(proj, wts["sinks"][layer], ck, cv, wts["g_q"], wts["g_k"], wts["g_ao"], bias,
                            layer, L, cache_len)
    x2 = _out_proj(x2, hm.reshape(b * s, M_WIDTH), ha.reshape(b * s, A_WIDTH), wts["w_out"], layer)
    x2 = _ffn(x2, wts["g_ffn"], wts["w_up"], wts["w_down"], layer)
    return x2.reshape(b, s, d), k_new, v_new, c_new, n_new, m_new


def kernel(x_prompt, x_sample, cache_k, cache_v, state_C, state_n, state_m, rel_bias, g_mix, w_in, b_i, b_f, g_q, g_k, sinks, g_mo, g_ao, w_out, g_ffn, w_up, w_down):
    depth = w_in.shape[0]
    bp, sp, d = x_prompt.shape
    bs, ls, _ = x_sample.shape
    n_win = cache_k.shape[2]
    assert sp % CHUNK == 0 and n_win == WINDOW and ls % 8 == 0 and ls <= CHUNK

    o = np.cumsum((0, QK_W, QK_W, M_WIDTH, M_WIDTH, M_HEADS, M_HEADS, A_WIDTH, A_KV_WIDTH, A_KV_WIDTH))
    seg = [w_in[:, :, o[i]:o[i + 1]] for i in range(9)]
    w_main = jnp.concatenate([seg[0], seg[1], seg[2], seg[3], seg[6], seg[7], seg[8]], axis=-1).astype(BF16)
    w_gate = jnp.concatenate([seg[4], seg[5], jnp.zeros((depth, d, GATE_W - 2 * M_HEADS), F32)], axis=-1).astype(BF16)
    gbias = jnp.concatenate([b_i, b_f, jnp.zeros((depth, GATE_W - 2 * M_HEADS), F32)], axis=-1)
    wts = {
        "g_mix": g_mix.reshape(depth, 1, d), "w_main": w_main, "w_gate": w_gate,
        "gbias": gbias.reshape(depth, 1, GATE_W), "g_mo": g_mo.reshape(depth, 1, M_WIDTH),
        "g_q": g_q.reshape(depth, 1, A_HEAD_DIM), "g_k": g_k.reshape(depth, 1, A_HEAD_DIM),
        "g_ao": g_ao.reshape(depth, 1, A_WIDTH), "sinks": sinks,
        "w_out": w_out.astype(BF16), "g_ffn": g_ffn.reshape(depth, 1, d),
        "w_up": w_up.astype(BF16), "w_down": w_down.astype(BF16),
    }

    bias = _bias_table(rel_bias)
    bias_p = bias.reshape(A_KV_HEADS, A_GROUP * CHUNK, WINDOW + CHUNK)
    bias_s = bias[:, :ls, :WINDOW + ls].reshape(A_KV_HEADS, A_GROUP * ls, WINDOW + ls)

    zk = jnp.zeros((bp, WINDOW, A_KV_WIDTH), F32)
    zc = jnp.zeros((bp, M_HEADS, M_DV, M_DK), F32)
    zn = jnp.zeros((bp, M_HEADS, M_DK), F32)
    zm = jnp.zeros((bp, 1, M_HEADS), F32)

    xp, xs = x_prompt, x_sample
    outs_p, outs_s = [], []
    for l in range(depth):
        xp, *st = _layer(xp, wts, l, CHUNK, bias_p, zk, zk, zc, zn, zm, 0)
        outs_p.append(st)
        xs, *st = _layer(xs, wts, l, ls, bias_s,
                         cache_k[l].reshape(bs, n_win, A_KV_WIDTH), cache_v[l].reshape(bs, n_win, A_KV_WIDTH),
                         state_C[l], state_n[l], state_m[l].reshape(bs, 1, M_HEADS), WINDOW)
        outs_s.append(st)

    def stack(outs, b):
        k, v, c, n, m = (jnp.stack([o_[i] for o_ in outs]) for i in range(5))
        return (k.reshape(depth, b, WINDOW, A_KV_HEADS, A_HEAD_DIM), v.reshape(depth, b, WINDOW, A_KV_HEADS, A_HEAD_DIM),
                c, n, m.reshape(depth, b, M_HEADS))

    return (xp, xs) + stack(outs_p, bp) + stack(outs_s, bs)
```

```python
import functools

import jax
import jax.numpy as jnp
import numpy as np
from jax import lax
from jax.experimental import pallas as pl
from jax.experimental.pallas import tpu as pltpu

F32 = jnp.float32
BF16 = jnp.bfloat16

CHUNK = 64
M_CHUNK = 256
M_HEADS = 4
M_DK = 128
M_DV = 256
M_WIDTH = M_HEADS * M_DV
A_HEADS = 16
A_KV_HEADS = 4
A_HEAD_DIM = 64
A_GROUP = A_HEADS // A_KV_HEADS
A_WIDTH = A_HEADS * A_HEAD_DIM
A_KV_WIDTH = A_KV_HEADS * A_HEAD_DIM
WINDOW = 128
N_BUCKETS = 32
MAX_DISTANCE = 128
EPS = 1e-6
NEG_INF = -1e30
M_SCALE = M_DK ** -0.5
A_SCALE = A_HEAD_DIM ** -0.5

QK_W = M_HEADS * M_DK
PROJ_W = 2 * QK_W + 2 * M_WIDTH + A_WIDTH + 2 * A_KV_WIDTH
GATE_W = 128

V7X_VMEM_LIMIT = 56 * 1024 * 1024

NT_DIMS = (((1,), (1,)), ((), ()))


def _params(sem, vmem=V7X_VMEM_LIMIT):
    return pltpu.CompilerParams(dimension_semantics=sem, vmem_limit_bytes=vmem)


def _row_tile(m, cap):
    t = min(m, cap)
    while m % t:
        t //= 2
    return t


def _dot(a, b):
    return jnp.dot(a, b, preferred_element_type=F32)


def _dot_nt(a, b):
    return lax.dot_general(a, b, NT_DIMS, preferred_element_type=F32)


def _t5_bucket(rel):
    half = N_BUCKETS // 2
    exact = half // 2
    n = np.abs(rel)
    large = exact + (np.log(np.maximum(n, 1) / exact) / np.log(MAX_DISTANCE / exact) * (half - exact)).astype(np.int32)
    large = np.minimum(large, half - 1)
    return (rel > 0).astype(np.int32) * half + np.where(n < exact, n, large).astype(np.int32)


def _bias_kernel(rel_ref, map_ref, out_ref):
    bmap = map_ref[...]
    for h in range(A_HEADS):
        acc = jnp.zeros(bmap.shape, F32)
        for b in range(N_BUCKETS):
            acc = jnp.where(bmap == b, rel_ref[b, h], acc)
        out_ref[h] = acc


def _bias_table(rel_bias):
    lk = WINDOW + CHUNK
    rel = (np.arange(lk)[None, :] - WINDOW) - np.arange(CHUNK)[:, None]
    bmap = jnp.asarray(_t5_bucket(rel), jnp.int32)
    return pl.pallas_call(
        _bias_kernel,
        out_shape=jax.ShapeDtypeStruct((A_HEADS, CHUNK, lk), F32),
        in_specs=[pl.BlockSpec(memory_space=pltpu.SMEM),
                  pl.BlockSpec(memory_space=pltpu.VMEM)],
        out_specs=pl.BlockSpec(memory_space=pltpu.VMEM),
        name="t5_bias_table",
    )(rel_bias, bmap)


def _rms(x, g):
    return x * lax.rsqrt(jnp.mean(x * x, axis=-1, keepdims=True) + EPS) * g


def _in_proj_kernel(x_ref, g_ref, w_ref, wg_ref, o_ref, gate_ref, h_scr):
    @pl.when(pl.program_id(1) == 0)
    def _():
        h = _rms(x_ref[...], g_ref[...]).astype(BF16)
        h_scr[...] = h
        gate_ref[...] = _dot(h, wg_ref[...])

    o_ref[...] = _dot(h_scr[...], w_ref[...]).astype(o_ref.dtype)


def _in_proj(x, g_mix, w_main, w_gate, layer):
    m, d = x.shape
    tm = _row_tile(m, 1024)
    tn = 768
    return pl.pallas_call(
        _in_proj_kernel,
        out_shape=(jax.ShapeDtypeStruct((m, PROJ_W), BF16), jax.ShapeDtypeStruct((m, GATE_W), F32)),
        grid=(m // tm, PROJ_W // tn),
        in_specs=[pl.BlockSpec((tm, d), lambda i, j: (i, 0)),
                  pl.BlockSpec((None, 1, d), lambda i, j: (layer, 0, 0)),
                  pl.BlockSpec((None, d, tn), lambda i, j: (layer, 0, j)),
                  pl.BlockSpec((None, d, GATE_W), lambda i, j: (layer, 0, 0))],
        out_specs=(pl.BlockSpec((tm, tn), lambda i, j: (i, j)),
                   pl.BlockSpec((tm, GATE_W), lambda i, j: (i, 0))),
        scratch_shapes=[pltpu.VMEM((tm, d), BF16)],
        compiler_params=_params(("parallel", "arbitrary")),
        name="in_proj",
    )(x, g_mix, w_main, w_gate)


def _mlstm_kernel(q_ref, k_ref, v_ref, og_ref, gate_ref, gbias_ref, gmo_ref, eye_ref, c0_ref, n0_ref, m0_ref,
                  hm_ref, cout_ref, nout_ref, mout_ref, c_scr, n_scr, m_scr, *, L):
    c = pl.program_id(1)
    last = pl.num_programs(1) - 1

    @pl.when(c == 0)
    def _():
        c_scr[...] = c0_ref[0]
        n_scr[...] = n0_ref[0]
        for h in range(M_HEADS):
            m_scr[h:h + 1, :] = jnp.broadcast_to(m0_ref[0, :, h:h + 1], (1, 128))

    a_all = gate_ref[0] + gbias_ref[...]
    logf = jnp.minimum(a_all, 0.0) - jnp.log(1.0 + jnp.exp(-jnp.abs(a_all)))
    row = lax.broadcasted_iota(jnp.int32, (L, GATE_W), 0)
    b_all = logf
    k = 1
    while k < L:
        b_all = b_all + jnp.where(row >= k, pltpu.roll(b_all, k, axis=0), 0.0)
        k *= 2
    b_sh = pltpu.roll(b_all, GATE_W - M_HEADS, axis=1)
    a_i = a_all - b_sh
    cmax = a_i
    k = 1
    while k < L:
        cmax = jnp.maximum(cmax, jnp.where(row >= k, pltpu.roll(cmax, k, axis=0), NEG_INF))
        k *= 2
    a_t = a_i.T
    tri = lax.broadcasted_iota(jnp.int32, (L, L), 0) >= lax.broadcasted_iota(jnp.int32, (L, L), 1)
    eye = eye_ref[...]

    for h in range(M_HEADS):
        b_col = b_sh[:, h:h + 1]
        a_col = a_i[:, h:h + 1]
        a_row = a_t[h:h + 1, :]
        m_prev = m_scr[h:h + 1, 0:1]
        m_run = jnp.maximum(cmax[:, h:h + 1], m_prev)
        decay_w = jnp.exp(jnp.where(tri, a_row - m_run, NEG_INF))
        qh = q_ref[0, :, h * M_DK:(h + 1) * M_DK]
        kh = k_ref[0, :, h * M_DK:(h + 1) * M_DK]
        vh = v_ref[0, :, h * M_DV:(h + 1) * M_DV]
        w = decay_w * (_dot_nt(qh, kh) * M_SCALE)
        g = jnp.exp(m_prev - m_run)
        cst = c_scr[h]
        n_row = n_scr[h:h + 1, :]
        num = _dot(w.astype(BF16), vh) + g * _dot_nt(qh, cst.astype(BF16))
        den = (jnp.sum(w, axis=1, keepdims=True)
               + g * jnp.sum(qh.astype(F32) * n_row, axis=1, keepdims=True))
        inv = 1.0 / jnp.maximum(jnp.abs(den), jnp.exp(-(b_col + m_run)))
        ms = jnp.mean(num * num, axis=1, keepdims=True)
        scale = inv * lax.rsqrt(inv * inv * ms + EPS)
        og = og_ref[0, :, h * M_DV:(h + 1) * M_DV].astype(F32)
        hm_ref[0, :, h * M_DV:(h + 1) * M_DV] = (
            num * scale * gmo_ref[:, h * M_DV:(h + 1) * M_DV] * jax.nn.sigmoid(og)).astype(hm_ref.dtype)

        b_last = b_col[L - 1:L, :]
        m_end = b_last + m_run[L - 1:L, :]
        wk = jnp.exp(b_last + a_col - m_end)
        decay = jnp.exp(b_last + m_prev - m_end)
        vw = (vh.astype(F32) * wk).astype(BF16)
        vw_t = _dot_nt(eye, vw).astype(BF16)
        c_new = decay * cst + M_SCALE * _dot(vw_t, kh)
        n_new = decay * n_row + M_SCALE * jnp.sum(kh.astype(F32) * wk, axis=0, keepdims=True)
        c_scr[h] = c_new
        n_scr[h:h + 1, :] = n_new
        m_scr[h:h + 1, :] = jnp.broadcast_to(m_end, (1, 128))

        @pl.when(c == last)
        def _():
            cout_ref[0, h] = c_new
            nout_ref[0, h:h + 1, :] = n_new
            mout_ref[0, :, h:h + 1] = m_end


def _mlstm(proj, gates, gbias, g_mo, eye, c0, n0, m0, layer, L):
    b, s, _ = proj.shape
    nc = s // L
    out_shape = (jax.ShapeDtypeStruct((b, s, M_WIDTH), BF16),
                 jax.ShapeDtypeStruct((b, M_HEADS, M_DV, M_DK), F32),
                 jax.ShapeDtypeStruct((b, M_HEADS, M_DK), F32),
                 jax.ShapeDtypeStruct((b, 1, M_HEADS), F32))
    return pl.pallas_call(
        functools.partial(_mlstm_kernel, L=L),
        out_shape=out_shape,
        grid=(b, nc),
        in_specs=[pl.BlockSpec((1, L, QK_W), lambda i, c: (i, c, 0)),
                  pl.BlockSpec((1, L, QK_W), lambda i, c: (i, c, 1)),
                  pl.BlockSpec((1, L, M_WIDTH), lambda i, c: (i, c, 1)),
                  pl.BlockSpec((1, L, M_WIDTH), lambda i, c: (i, c, 2)),
                  pl.BlockSpec((1, L, GATE_W), lambda i, c: (i, c, 0)),
                  pl.BlockSpec((None, 1, GATE_W), lambda i, c: (layer, 0, 0)),
                  pl.BlockSpec((None, 1, M_WIDTH), lambda i, c: (layer, 0, 0)),
                  pl.BlockSpec((M_DV, M_DV), lambda i, c: (0, 0)),
                  pl.BlockSpec((1, M_HEADS, M_DV, M_DK), lambda i, c: (i, 0, 0, 0)),
                  pl.BlockSpec((1, M_HEADS, M_DK), lambda i, c: (i, 0, 0)),
                  pl.BlockSpec((1, 1, M_HEADS), lambda i, c: (i, 0, 0))],
        out_specs=(pl.BlockSpec((1, L, M_WIDTH), lambda i, c: (i, c, 0)),
                   pl.BlockSpec((1, M_HEADS, M_DV, M_DK), lambda i, c: (i, 0, 0, 0)),
                   pl.BlockSpec((1, M_HEADS, M_DK), lambda i, c: (i, 0, 0)),
                   pl.BlockSpec((1, 1, M_HEADS), lambda i, c: (i, 0, 0))),
        scratch_shapes=[pltpu.VMEM((M_HEADS, M_DV, M_DK), F32),
                        pltpu.VMEM((M_HEADS, M_DK), F32),
                        pltpu.VMEM((M_HEADS, 128), F32)],
        compiler_params=_params(("arbitrary", "arbitrary")),
        name="mlstm",
    )(proj, proj, proj, proj, gates, gbias, g_mo, eye, c0, n0, m0)


def _swa_kernel(q_ref, k_ref, v_ref, ck_ref, cv_ref, gq_ref, gk_ref, gao_ref, bias_ref, sink_ref,
                e4_ref, rep_ref, eye_ref, ha_ref, cko_ref, cvo_ref, kbuf, k4buf, vbuf, *, L, cache_len):
    c = pl.program_id(1)
    last = pl.num_programs(1) - 1
    lk = WINDOW + L
    gl = A_GROUP * L
    rep = rep_ref[...]

    @pl.when(c == 0)
    def _():
        kbuf[0:WINDOW, :] = ck_ref[0]
        vbuf[0:WINDOW, :] = cv_ref[0]
        k4buf[0:WINDOW, :] = _dot(ck_ref[0].astype(BF16), rep).astype(BF16)

    q = q_ref[0].astype(F32)
    k = k_ref[0].astype(F32)
    sq = jnp.concatenate([q[:, i * A_KV_WIDTH:(i + 1) * A_KV_WIDTH] for i in range(A_KV_HEADS)] + [k], axis=0)
    sq = sq * sq
    sq_hi = sq.astype(BF16)
    sq_lo = (sq - sq_hi.astype(F32)).astype(BF16)
    ss = _dot(sq_hi, e4_ref[...]) + _dot(sq_lo, e4_ref[...])
    rinv = lax.rsqrt(ss * (1.0 / A_HEAD_DIM) + EPS)
    kn = k * rinv[A_KV_HEADS * L:, :] * gk_ref[...]
    kbuf[WINDOW:lk, :] = kn
    k4buf[WINDOW:lk, :] = _dot(kn.astype(BF16), rep).astype(BF16)
    vbuf[WINDOW:lk, :] = v_ref[0].astype(F32)

    v_t = _dot_nt(eye_ref[...], vbuf[...].astype(BF16)).astype(BF16)
    n_prev = jnp.minimum(c * L + cache_len, WINDOW)
    valid = lax.broadcasted_iota(jnp.int32, (lk, gl), 0) >= WINDOW - n_prev
    gmask = (lax.broadcasted_iota(jnp.int32, (gl, A_KV_WIDTH), 0) // L
             == lax.broadcasted_iota(jnp.int32, (gl, A_KV_WIDTH), 1) // A_HEAD_DIM)
    o_t = []
    for kv in range(A_KV_HEADS):
        sl = slice(kv * A_KV_WIDTH, (kv + 1) * A_KV_WIDTH)
        qn = (q[:, sl] * rinv[kv * L:(kv + 1) * L, :] * gq_ref[:, sl]).astype(BF16)
        qm = jnp.where(gmask, jnp.concatenate([qn] * A_GROUP, axis=0), 0.0)
        s = _dot_nt(k4buf[:, sl], qm)
        s = jnp.where(valid, s * A_SCALE + bias_ref[kv], NEG_INF)
        sk = sink_ref[kv]
        mx = jnp.maximum(jnp.max(s, axis=0, keepdims=True), sk)
        p = jnp.exp(s - mx)
        den = jnp.sum(p, axis=0, keepdims=True) + jnp.exp(sk - mx)
        o = _dot(v_t[kv * A_HEAD_DIM:(kv + 1) * A_HEAD_DIM, :], p.astype(BF16))
        o_t.append(o * (1.0 / den))
    o_all = jnp.concatenate(o_t, axis=0).T
    ha = jnp.concatenate([o_all[g * L:(g + 1) * L, :] for g in range(A_GROUP)], axis=1)
    ha_ref[0] = _rms(ha, gao_ref[...]).astype(ha_ref.dtype)

    k_win = kbuf[L:lk, :]
    v_win = vbuf[L:lk, :]
    k4_win = k4buf[L:lk, :]
    kbuf[0:WINDOW, :] = k_win
    vbuf[0:WINDOW, :] = v_win
    k4buf[0:WINDOW, :] = k4_win

    @pl.when(c == last)
    def _():
        cko_ref[0] = k_win
        cvo_ref[0] = v_win


def _swa(proj, ck, cv, g_q, g_k, g_ao, bias, sink, consts, layer, L, cache_len):
    b, s, _ = proj.shape
    nc = s // L
    lk = WINDOW + L
    gl = A_GROUP * L
    kv_blk = (2 * QK_W + 2 * M_WIDTH + A_WIDTH) // A_KV_WIDTH
    q_blk = (2 * QK_W + 2 * M_WIDTH) // A_WIDTH
    e4, rep, eye = consts
    out_shape = (jax.ShapeDtypeStruct((b, s, A_WIDTH), BF16),
                 jax.ShapeDtypeStruct((b, WINDOW, A_KV_WIDTH), F32),
                 jax.ShapeDtypeStruct((b, WINDOW, A_KV_WIDTH), F32))
    full = lambda *shape: pl.BlockSpec(shape, lambda i, c: (0,) * len(shape))
    return pl.pallas_call(
        functools.partial(_swa_kernel, L=L, cache_len=cache_len),
        out_shape=out_shape,
        grid=(b, nc),
        in_specs=[pl.BlockSpec((1, L, A_WIDTH), lambda i, c: (i, c, q_blk)),
                  pl.BlockSpec((1, L, A_KV_WIDTH), lambda i, c: (i, c, kv_blk)),
                  pl.BlockSpec((1, L, A_KV_WIDTH), lambda i, c: (i, c, kv_blk + 1)),
                  pl.BlockSpec((1, WINDOW, A_KV_WIDTH), lambda i, c: (i, 0, 0)),
                  pl.BlockSpec((1, WINDOW, A_KV_WIDTH), lambda i, c: (i, 0, 0)),
                  pl.BlockSpec((None, 1, A_WIDTH), lambda i, c: (layer, 0, 0)),
                  pl.BlockSpec((None, 1, A_KV_WIDTH), lambda i, c: (layer, 0, 0)),
                  pl.BlockSpec((None, 1, A_WIDTH), lambda i, c: (layer, 0, 0)),
                  full(A_KV_HEADS, lk, gl),
                  pl.BlockSpec((None, A_KV_HEADS, 1, gl), lambda i, c: (layer, 0, 0, 0)),
                  full(A_KV_WIDTH, A_KV_WIDTH),
                  full(A_KV_WIDTH, A_WIDTH),
                  full(A_KV_WIDTH, A_KV_WIDTH)],
        out_specs=(pl.BlockSpec((1, L, A_WIDTH), lambda i, c: (i, c, 0)),
                   pl.BlockSpec((1, WINDOW, A_KV_WIDTH), lambda i, c: (i, 0, 0)),
                   pl.BlockSpec((1, WINDOW, A_KV_WIDTH), lambda i, c: (i, 0, 0))),
        scratch_shapes=[pltpu.VMEM((lk, A_KV_WIDTH), F32),
                        pltpu.VMEM((lk, A_WIDTH), BF16),
                        pltpu.VMEM((lk, A_KV_WIDTH), F32)],
        compiler_params=_params(("arbitrary", "arbitrary")),
        name="swa",
    )(proj, proj, proj, ck, cv, g_q, g_k, g_ao, bias, sink, e4, rep, eye)


def _out_proj_kernel(x_ref, hm_ref, ha_ref, wm_ref, wa_ref, o_ref):
    o_ref[...] = x_ref[...] + _dot(hm_ref[...], wm_ref[...]) + _dot(ha_ref[...], wa_ref[...])


def _out_proj(x, hm, ha, w_out, layer):
    m, d = x.shape
    tm = _row_tile(m, 512)
    return pl.pallas_call(
        _out_proj_kernel,
        out_shape=jax.ShapeDtypeStruct((m, d), F32),
        grid=(m // tm,),
        in_specs=[pl.BlockSpec((tm, d), lambda i: (i, 0)),
                  pl.BlockSpec((tm, M_WIDTH), lambda i: (i, 0)),
                  pl.BlockSpec((tm, A_WIDTH), lambda i: (i, 0)),
                  pl.BlockSpec((None, M_WIDTH, d), lambda i: (layer, 0, 0)),
                  pl.BlockSpec((None, A_WIDTH, d), lambda i: (layer, 1, 0))],
        out_specs=pl.BlockSpec((tm, d), lambda i: (i, 0)),
        compiler_params=_params(("parallel",)),
        name="out_proj",
    )(x, hm, ha, w_out, w_out)


def _ffn_kernel(x_ref, g_ref, wu_ref, wd_ref, o_ref, h_scr):
    f = pl.program_id(1)

    @pl.when(f == 0)
    def _():
        x = x_ref[...]
        h_scr[...] = _rms(x, g_ref[...]).astype(BF16)
        o_ref[...] = x

    u = _dot(h_scr[...], wu_ref[...])
    a = jnp.square(jnp.maximum(u, 0.0)).astype(BF16)
    o_ref[...] += _dot(a, wd_ref[...])


def _ffn(x, g_ffn, w_up, w_down, layer):
    m, d = x.shape
    d_ff = w_up.shape[-1]
    tm = _row_tile(m, 512)
    tf = 1024
    return pl.pallas_call(
        _ffn_kernel,
        out_shape=jax.ShapeDtypeStruct((m, d), F32),
        grid=(m // tm, d_ff // tf),
        in_specs=[pl.BlockSpec((tm, d), lambda i, f: (i, 0)),
                  pl.BlockSpec((None, 1, d), lambda i, f: (layer, 0, 0)),
                  pl.BlockSpec((None, d, tf), lambda i, f: (layer, 0, f)),
                  pl.BlockSpec((None, tf, d), lambda i, f: (layer, f, 0))],
        out_specs=pl.BlockSpec((tm, d), lambda i, f: (i, 0)),
        scratch_shapes=[pltpu.VMEM((tm, d), BF16)],
        compiler_params=_params(("parallel", "arbitrary")),
        name="ffn",
    )(x, g_ffn, w_up, w_down)


def _layer(x, wts, layer, lm, la, bias, sink, ck, cv, c0, n0, m0, cache_len):
    b, s, d = x.shape
    x2 = x.reshape(b * s, d)
    proj, gates = _in_proj(x2, wts["g_mix"], wts["w_main"], wts["w_gate"], layer)
    proj = proj.reshape(b, s, PROJ_W)
    gates = gates.reshape(b, s, GATE_W)
    hm, c_new, n_new, m_new = _mlstm(proj, gates, wts["gbias"], wts["g_mo"], wts["consts"][2], c0, n0, m0, layer, lm)
    ha, k_new, v_new = _swa(proj, ck, cv, wts["g_q"], wts["g_k"], wts["g_ao"], bias, sink, wts["consts"],
                            layer, la, cache_len)
    x2 = _out_proj(x2, hm.reshape(b * s, M_WIDTH), ha.reshape(b * s, A_WIDTH), wts["w_out"], layer)
    x2 = _ffn(x2, wts["g_ffn"], wts["w_up"], wts["w_down"], layer)
    return x2.reshape(b, s, d), k_new, v_new, c_new, n_new, m_new


def _head_consts():
    hd = np.arange(A_KV_WIDTH) // A_HEAD_DIM
    e4 = (hd[:, None] == hd[None, :]).astype(np.float32)
    col = np.arange(A_WIDTH)
    src = (col // A_KV_WIDTH) * A_HEAD_DIM + col % A_HEAD_DIM
    rep = (np.arange(A_KV_WIDTH)[:, None] == src[None, :]).astype(np.float32)
    eye = np.eye(A_KV_WIDTH, dtype=np.float32)
    return tuple(jnp.asarray(a, BF16) for a in (e4, rep, eye))


def kernel(x_prompt, x_sample, cache_k, cache_v, state_C, state_n, state_m, rel_bias, g_mix, w_in, b_i, b_f, g_q, g_k, sinks, g_mo, g_ao, w_out, g_ffn, w_up, w_down):
    depth = w_in.shape[0]
    bp, sp, d = x_prompt.shape
    bs, ls, _ = x_sample.shape
    n_win = cache_k.shape[2]
    assert sp % CHUNK == 0 and n_win == WINDOW and ls % 16 == 0 and ls <= CHUNK
    lm = M_CHUNK if sp % M_CHUNK == 0 else CHUNK

    o = np.cumsum((0, QK_W, QK_W, M_WIDTH, M_WIDTH, M_HEADS, M_HEADS, A_WIDTH, A_KV_WIDTH, A_KV_WIDTH))
    seg = [w_in[:, :, o[i]:o[i + 1]] for i in range(9)]
    w_main = jnp.concatenate([seg[0], seg[1], seg[2], seg[3], seg[6], seg[7], seg[8]], axis=-1).astype(BF16)
    w_gate = jnp.concatenate([seg[4], seg[5], jnp.zeros((depth, d, GATE_W - 2 * M_HEADS), F32)], axis=-1).astype(BF16)
    gbias = jnp.concatenate([b_i, b_f, jnp.zeros((depth, GATE_W - 2 * M_HEADS), F32)], axis=-1)

    def regroup(a):
        rest = a.shape[2:]
        a = a.reshape(depth, A_KV_HEADS, A_GROUP, A_HEAD_DIM, *rest)
        return jnp.swapaxes(a, 1, 2).reshape(depth, A_WIDTH, *rest)

    w_out_b = jnp.concatenate([w_out[:, :M_WIDTH], regroup(w_out[:, M_WIDTH:])], axis=1).astype(BF16)
    wts = {
        "g_mix": g_mix.reshape(depth, 1, d), "w_main": w_main, "w_gate": w_gate,
        "gbias": gbias.reshape(depth, 1, GATE_W), "g_mo": g_mo.reshape(depth, 1, M_WIDTH),
        "g_q": jnp.tile(g_q, (1, A_HEADS)).reshape(depth, 1, A_WIDTH),
        "g_k": jnp.tile(g_k, (1, A_KV_HEADS)).reshape(depth, 1, A_KV_WIDTH),
        "g_ao": regroup(g_ao).reshape(depth, 1, A_WIDTH),
        "w_out": w_out_b, "g_ffn": g_ffn.reshape(depth, 1, d),
        "w_up": w_up.astype(BF16), "w_down": w_down.astype(BF16),
        "consts": _head_consts(),
    }

    bias = _bias_table(rel_bias)

    def bias_t(lq):
        lk = WINDOW + lq
        t = bias[:, :lq, :lk].reshape(A_KV_HEADS, A_GROUP, lq, lk)
        return t.transpose(0, 3, 1, 2).reshape(A_KV_HEADS, lk, A_GROUP * lq)

    def sink_rows(lq):
        return jnp.repeat(sinks.reshape(depth, A_KV_HEADS, 1, A_GROUP), lq, axis=-1)

    zk = jnp.zeros((bp, WINDOW, A_KV_WIDTH), F32)
    zc = jnp.zeros((bp, M_HEADS, M_DV, M_DK), F32)
    zn = jnp.zeros((bp, M_HEADS, M_DK), F32)
    zm = jnp.zeros((bp, 1, M_HEADS), F32)

    xp, xs = x_prompt, x_sample
    bias_p, bias_s, sink_p, sink_s = bias_t(CHUNK), bias_t(ls), sink_rows(CHUNK), sink_rows(ls)
    outs_p, outs_s = [], []
    for l in range(depth):
        xp, *st = _layer(xp, wts, l, lm, CHUNK, bias_p, sink_p, zk, zk, zc, zn, zm, 0)
        outs_p.append(st)
        xs, *st = _layer(xs, wts, l, ls, ls, bias_s, sink_s,
                         cache_k[l].reshape(bs, n_win, A_KV_WIDTH), cache_v[l].reshape(bs, n_win, A_KV_WIDTH),
                         state_C[l], state_n[l], state_m[l].reshape(bs, 1, M_HEADS), WINDOW)
        outs_s.append(st)

    def stack(outs, b):
        k, v, c, n, m = (jnp.stack([o_[i] for o_ in outs]) for i in range(5))
        return (k.reshape(depth, b, WINDOW, A_KV_HEADS, A_HEAD_DIM), v.reshape(depth, b, WINDOW, A_KV_HEADS, A_HEAD_DIM),
                c, n, m.reshape(depth, b, M_HEADS))

    return (xp, xs) + stack(outs_p, bp) + stack(outs_s, bs)
```

```python
import functools

import jax
import jax.numpy as jnp
import numpy as np
from jax import lax
from jax.experimental import pallas as pl
from jax.experimental.pallas import tpu as pltpu

F32 = jnp.float32
BF16 = jnp.bfloat16

CHUNK = 64
M_CHUNK = 256
A_CHUNKS_PER_STEP = 4
M_HEADS = 4
M_DK = 128
M_DV = 256
M_WIDTH = M_HEADS * M_DV
A_HEADS = 16
A_KV_HEADS = 4
A_HEAD_DIM = 64
A_GROUP = A_HEADS // A_KV_HEADS
A_WIDTH = A_HEADS * A_HEAD_DIM
A_KV_WIDTH = A_KV_HEADS * A_HEAD_DIM
WINDOW = 128
N_BUCKETS = 32
MAX_DISTANCE = 128
EPS = 1e-6
NEG_INF = -1e30
M_SCALE = M_DK ** -0.5
A_SCALE = A_HEAD_DIM ** -0.5

QK_W = M_HEADS * M_DK
PROJ_W = 2 * QK_W + 2 * M_WIDTH + A_WIDTH + 2 * A_KV_WIDTH
GATE_W = 128

V7X_VMEM_LIMIT = 56 * 1024 * 1024

NT_DIMS = (((1,), (1,)), ((), ()))


def _params(sem, vmem=V7X_VMEM_LIMIT):
    return pltpu.CompilerParams(dimension_semantics=sem, vmem_limit_bytes=vmem)


def _row_tile(m, cap):
    t = min(m, cap)
    while m % t:
        t //= 2
    return t


def _dot(a, b):
    return jnp.dot(a, b, preferred_element_type=F32)


def _dot_nt(a, b):
    return lax.dot_general(a, b, NT_DIMS, preferred_element_type=F32)


def _t5_bucket(rel):
    half = N_BUCKETS // 2
    exact = half // 2
    n = np.abs(rel)
    large = exact + (np.log(np.maximum(n, 1) / exact) / np.log(MAX_DISTANCE / exact) * (half - exact)).astype(np.int32)
    large = np.minimum(large, half - 1)
    return (rel > 0).astype(np.int32) * half + np.where(n < exact, n, large).astype(np.int32)


def _bias_kernel(rel_ref, map_ref, out_ref):
    bmap = map_ref[...]
    for h in range(A_HEADS):
        acc = jnp.zeros(bmap.shape, F32)
        for b in range(N_BUCKETS):
            acc = jnp.where(bmap == b, rel_ref[b, h], acc)
        out_ref[h] = acc


def _bias_table(rel_bias):
    lk = WINDOW + CHUNK
    rel = (np.arange(lk)[None, :] - WINDOW) - np.arange(CHUNK)[:, None]
    bmap = jnp.asarray(_t5_bucket(rel), jnp.int32)
    return pl.pallas_call(
        _bias_kernel,
        out_shape=jax.ShapeDtypeStruct((A_HEADS, CHUNK, lk), F32),
        in_specs=[pl.BlockSpec(memory_space=pltpu.SMEM),
                  pl.BlockSpec(memory_space=pltpu.VMEM)],
        out_specs=pl.BlockSpec(memory_space=pltpu.VMEM),
        name="t5_bias_table",
    )(rel_bias, bmap)


def _rms(x, g):
    return x * lax.rsqrt(jnp.mean(x * x, axis=-1, keepdims=True) + EPS) * g


def _in_proj_kernel(x_ref, g_ref, w_ref, wg_ref, o_ref, gate_ref, h_scr):
    @pl.when(pl.program_id(1) == 0)
    def _():
        h = _rms(x_ref[...], g_ref[...]).astype(BF16)
        h_scr[...] = h
        gate_ref[...] = _dot(h, wg_ref[...])

    o_ref[...] = _dot(h_scr[...], w_ref[...]).astype(o_ref.dtype)


def _in_proj(x, g_mix, w_main, w_gate, layer):
    m, d = x.shape
    tm = _row_tile(m, 1024)
    tn = 768
    return pl.pallas_call(
        _in_proj_kernel,
        out_shape=(jax.ShapeDtypeStruct((m, PROJ_W), BF16), jax.ShapeDtypeStruct((m, GATE_W), F32)),
        grid=(m // tm, PROJ_W // tn),
        in_specs=[pl.BlockSpec((tm, d), lambda i, j: (i, 0)),
                  pl.BlockSpec((None, 1, d), lambda i, j: (layer, 0, 0)),
                  pl.BlockSpec((None, d, tn), lambda i, j: (layer, 0, j)),
                  pl.BlockSpec((None, d, GATE_W), lambda i, j: (layer, 0, 0))],
        out_specs=(pl.BlockSpec((tm, tn), lambda i, j: (i, j)),
                   pl.BlockSpec((tm, GATE_W), lambda i, j: (i, 0))),
        scratch_shapes=[pltpu.VMEM((tm, d), BF16)],
        compiler_params=_params(("parallel", "arbitrary")),
        name="in_proj",
    )(x, g_mix, w_main, w_gate)


def _mlstm_kernel(q_ref, k_ref, v_ref, og_ref, gate_ref, gbias_ref, gmo_ref, eye_ref, c0_ref, n0_ref, m0_ref,
                  hm_ref, cout_ref, nout_ref, mout_ref, c_scr, n_scr, m_scr, *, L):
    c = pl.program_id(1)
    last = pl.num_programs(1) - 1

    @pl.when(c == 0)
    def _():
        c_scr[...] = c0_ref[0]
        n_scr[...] = n0_ref[0]
        for h in range(M_HEADS):
            m_scr[h:h + 1, :] = jnp.broadcast_to(m0_ref[0, :, h:h + 1], (1, 128))

    a_all = gate_ref[0] + gbias_ref[...]
    logf = jnp.minimum(a_all, 0.0) - jnp.log(1.0 + jnp.exp(-jnp.abs(a_all)))
    row = lax.broadcasted_iota(jnp.int32, (L, GATE_W), 0)
    b_all = logf
    k = 1
    while k < L:
        b_all = b_all + jnp.where(row >= k, pltpu.roll(b_all, k, axis=0), 0.0)
        k *= 2
    b_sh = pltpu.roll(b_all, GATE_W - M_HEADS, axis=1)
    a_i = a_all - b_sh
    cmax = a_i
    k = 1
    while k < L:
        cmax = jnp.maximum(cmax, jnp.where(row >= k, pltpu.roll(cmax, k, axis=0), NEG_INF))
        k *= 2
    a_t = a_i.T
    tri = lax.broadcasted_iota(jnp.int32, (L, L), 0) >= lax.broadcasted_iota(jnp.int32, (L, L), 1)
    eye = eye_ref[...]

    for h in range(M_HEADS):
        b_col = b_sh[:, h:h + 1]
        a_col = a_i[:, h:h + 1]
        a_row = a_t[h:h + 1, :]
        m_prev = m_scr[h:h + 1, 0:1]
        m_run = jnp.maximum(cmax[:, h:h + 1], m_prev)
        decay_w = jnp.exp(jnp.where(tri, a_row - m_run, NEG_INF))
        qh = q_ref[0, :, h * M_DK:(h + 1) * M_DK]
        kh = k_ref[0, :, h * M_DK:(h + 1) * M_DK]
        vh = v_ref[0, :, h * M_DV:(h + 1) * M_DV]
        w = decay_w * (_dot_nt(qh, kh) * M_SCALE)
        g = jnp.exp(m_prev - m_run)
        cst = c_scr[h]
        n_row = n_scr[h:h + 1, :]
        num = _dot(w.astype(BF16), vh) + g * _dot_nt(qh, cst.astype(BF16))
        den = (jnp.sum(w, axis=1, keepdims=True)
               + g * jnp.sum(qh.astype(F32) * n_row, axis=1, keepdims=True))
        inv = 1.0 / jnp.maximum(jnp.abs(den), jnp.exp(-(b_col + m_run)))
        ms = jnp.mean(num * num, axis=1, keepdims=True)
        scale = inv * lax.rsqrt(inv * inv * ms + EPS)
        og = og_ref[0, :, h * M_DV:(h + 1) * M_DV].astype(F32)
        hm_ref[0, :, h * M_DV:(h + 1) * M_DV] = (
            num * scale * gmo_ref[:, h * M_DV:(h + 1) * M_DV] * jax.nn.sigmoid(og)).astype(hm_ref.dtype)

        b_last = b_col[L - 1:L, :]
        m_end = b_last + m_run[L - 1:L, :]
        wk = jnp.exp(b_last + a_col - m_end)
        decay = jnp.exp(b_last + m_prev - m_end)
        vw = (vh.astype(F32) * wk).astype(BF16)
        vw_t = _dot_nt(eye, vw).astype(BF16)
        c_new = decay * cst + M_SCALE * _dot(vw_t, kh)
        n_new = decay * n_row + M_SCALE * jnp.sum(kh.astype(F32) * wk, axis=0, keepdims=True)
        c_scr[h] = c_new
        n_scr[h:h + 1, :] = n_new
        m_scr[h:h + 1, :] = jnp.broadcast_to(m_end, (1, 128))

        @pl.when(c == last)
        def _():
            cout_ref[0, h] = c_new
            nout_ref[0, h:h + 1, :] = n_new
            mout_ref[0, :, h:h + 1] = m_end


def _mlstm(proj, gates, gbias, g_mo, eye, c0, n0, m0, layer, L):
    b, s, _ = proj.shape
    nc = s // L
    out_shape = (jax.ShapeDtypeStruct((b, s, M_WIDTH), BF16),
                 jax.ShapeDtypeStruct((b, M_HEADS, M_DV, M_DK), F32),
                 jax.ShapeDtypeStruct((b, M_HEADS, M_DK), F32),
                 jax.ShapeDtypeStruct((b, 1, M_HEADS), F32))
    return pl.pallas_call(
        functools.partial(_mlstm_kernel, L=L),
        out_shape=out_shape,
        grid=(b, nc),
        in_specs=[pl.BlockSpec((1, L, QK_W), lambda i, c: (i, c, 0)),
                  pl.BlockSpec((1, L, QK_W), lambda i, c: (i, c, 1)),
                  pl.BlockSpec((1, L, M_WIDTH), lambda i, c: (i, c, 1)),
                  pl.BlockSpec((1, L, M_WIDTH), lambda i, c: (i, c, 2)),
                  pl.BlockSpec((1, L, GATE_W), lambda i, c: (i, c, 0)),
                  pl.BlockSpec((None, 1, GATE_W), lambda i, c: (layer, 0, 0)),
                  pl.BlockSpec((None, 1, M_WIDTH), lambda i, c: (layer, 0, 0)),
                  pl.BlockSpec((M_DV, M_DV), lambda i, c: (0, 0)),
                  pl.BlockSpec((1, M_HEADS, M_DV, M_DK), lambda i, c: (i, 0, 0, 0)),
                  pl.BlockSpec((1, M_HEADS, M_DK), lambda i, c: (i, 0, 0)),
                  pl.BlockSpec((1, 1, M_HEADS), lambda i, c: (i, 0, 0))],
        out_specs=(pl.BlockSpec((1, L, M_WIDTH), lambda i, c: (i, c, 0)),
                   pl.BlockSpec((1, M_HEADS, M_DV, M_DK), lambda i, c: (i, 0, 0, 0)),
                   pl.BlockSpec((1, M_HEADS, M_DK), lambda i, c: (i, 0, 0)),
                   pl.BlockSpec((1, 1, M_HEADS), lambda i, c: (i, 0, 0))),
        scratch_shapes=[pltpu.VMEM((M_HEADS, M_DV, M_DK), F32),
                        pltpu.VMEM((M_HEADS, M_DK), F32),
                        pltpu.VMEM((M_HEADS, 128), F32)],
        compiler_params=_params(("arbitrary", "arbitrary")),
        name="mlstm",
    )(proj, proj, proj, proj, gates, gbias, g_mo, eye, c0, n0, m0)


def _swa_kernel(q_ref, k_ref, v_ref, ck_ref, cv_ref, gq_ref, gk_ref, gao_ref, bias_ref, sink_ref,
                e4_ref, eye_ref, ha_ref, cko_ref, cvo_ref, khist, vhist, *, L, CS, cache_len):
    c = pl.program_id(1)
    last = pl.num_programs(1) - 1
    T = CS * L
    lk = WINDOW + L
    gl = A_GROUP * L
    kv_lane = lax.broadcasted_iota(jnp.int32, (1, A_KV_WIDTH), 1) // A_HEAD_DIM

    def lane_masked(kn):
        knb = kn.astype(BF16)
        return [jnp.where(kv_lane == kv, knb, 0.0) for kv in range(A_KV_HEADS)]

    @pl.when(c == 0)
    def _():
        khist[...] = ck_ref[0]
        vhist[...] = cv_ref[0]

    q = q_ref[0].astype(F32)
    k = k_ref[0].astype(F32)
    sq = jnp.concatenate([q[:, g * A_KV_WIDTH:(g + 1) * A_KV_WIDTH] for g in range(A_GROUP)] + [k], axis=0)
    sq = sq * sq
    sq_hi = sq.astype(BF16)
    sq_lo = (sq - sq_hi.astype(F32)).astype(BF16)
    ss = _dot(jnp.concatenate([sq_hi, sq_lo], axis=0), e4_ref[...])
    rinv = lax.rsqrt((ss[:5 * T] + ss[5 * T:]) * (1.0 / A_HEAD_DIM) + EPS)
    qn = [(q[:, g * A_KV_WIDTH:(g + 1) * A_KV_WIDTH] * rinv[g * T:(g + 1) * T, :]
           * gq_ref[:, g * A_KV_WIDTH:(g + 1) * A_KV_WIDTH]).astype(BF16) for g in range(A_GROUP)]
    kn = k * rinv[A_GROUP * T:, :] * gk_ref[...]

    k_all = jnp.concatenate([khist[...], kn], axis=0)
    v_all = jnp.concatenate([vhist[...], v_ref[0].astype(F32)], axis=0)
    vb_all = v_all.astype(BF16)

    for j in range(CS):
        rows = slice(j * L, j * L + lk)
        qst = jnp.concatenate([qn[g][j * L:(j + 1) * L, :] for g in range(A_GROUP)], axis=0)
        km = jnp.concatenate(lane_masked(k_all[rows]), axis=0)
        s_all = _dot_nt(km, qst)
        v_t = _dot_nt(eye_ref[...], vb_all[rows]).astype(BF16)
        var = jnp.minimum((c * CS + j) * L, WINDOW - cache_len) // L
        o_t = []
        for kv in range(A_KV_HEADS):
            s = s_all[kv * lk:(kv + 1) * lk, :] + bias_ref[var, kv]
            sk = sink_ref[kv]
            mx = jnp.maximum(jnp.max(s, axis=0, keepdims=True), sk)
            p = jnp.exp(s - mx)
            den = jnp.sum(p, axis=0, keepdims=True) + jnp.exp(sk - mx)
            o = _dot(v_t[kv * A_HEAD_DIM:(kv + 1) * A_HEAD_DIM, :], p.astype(BF16))
            o_t.append(o * (1.0 / den))
        o_all = jnp.concatenate(o_t, axis=0).T
        ha = jnp.concatenate([o_all[g * L:(g + 1) * L, :] for g in range(A_GROUP)], axis=1)
        ha_ref[0, j * L:(j + 1) * L, :] = _rms(ha, gao_ref[...]).astype(ha_ref.dtype)

    k_win = k_all[T:T + WINDOW]
    v_win = v_all[T:T + WINDOW]
    khist[...] = k_win
    vhist[...] = v_win

    @pl.when(c == last)
    def _():
        cko_ref[0] = k_win
        cvo_ref[0] = v_win


def _swa(proj, ck, cv, g_q, g_k, g_ao, bias, sink, consts, layer, L, CS, cache_len):
    b, s, _ = proj.shape
    T = CS * L
    lk = WINDOW + L
    gl = A_GROUP * L
    nvar = bias.shape[0]
    kv_blk = (2 * QK_W + 2 * M_WIDTH + A_WIDTH) // A_KV_WIDTH
    q_blk = (2 * QK_W + 2 * M_WIDTH) // A_WIDTH
    e4, eye = consts
    out_shape = (jax.ShapeDtypeStruct((b, s, A_WIDTH), BF16),
                 jax.ShapeDtypeStruct((b, WINDOW, A_KV_WIDTH), F32),
                 jax.ShapeDtypeStruct((b, WINDOW, A_KV_WIDTH), F32))
    full = lambda *shape: pl.BlockSpec(shape, lambda i, c: (0,) * len(shape))
    return pl.pallas_call(
        functools.partial(_swa_kernel, L=L, CS=CS, cache_len=cache_len),
        out_shape=out_shape,
        grid=(b, s // T),
        in_specs=[pl.BlockSpec((1, T, A_WIDTH), lambda i, c: (i, c, q_blk)),
                  pl.BlockSpec((1, T, A_KV_WIDTH), lambda i, c: (i, c, kv_blk)),
                  pl.BlockSpec((1, T, A_KV_WIDTH), lambda i, c: (i, c, kv_blk + 1)),
                  pl.BlockSpec((1, WINDOW, A_KV_WIDTH), lambda i, c: (i, 0, 0)),
                  pl.BlockSpec((1, WINDOW, A_KV_WIDTH), lambda i, c: (i, 0, 0)),
                  pl.BlockSpec((None, 1, A_WIDTH), lambda i, c: (layer, 0, 0)),
                  pl.BlockSpec((None, 1, A_KV_WIDTH), lambda i, c: (layer, 0, 0)),
                  pl.BlockSpec((None, 1, A_WIDTH), lambda i, c: (layer, 0, 0)),
                  full(nvar, A_KV_HEADS, lk, gl),
                  pl.BlockSpec((None, A_KV_HEADS, 1, gl), lambda i, c: (layer, 0, 0, 0)),
                  full(A_KV_WIDTH, A_KV_WIDTH),
                  full(A_KV_WIDTH, A_KV_WIDTH)],
        out_specs=(pl.BlockSpec((1, T, A_WIDTH), lambda i, c: (i, c, 0)),
                   pl.BlockSpec((1, WINDOW, A_KV_WIDTH), lambda i, c: (i, 0, 0)),
                   pl.BlockSpec((1, WINDOW, A_KV_WIDTH), lambda i, c: (i, 0, 0))),
        scratch_shapes=[pltpu.VMEM((WINDOW, A_KV_WIDTH), F32),
                        pltpu.VMEM((WINDOW, A_KV_WIDTH), F32)],
        compiler_params=_params(("arbitrary", "arbitrary")),
        name="swa",
    )(proj, proj, proj, ck, cv, g_q, g_k, g_ao, bias, sink, e4, eye)


def _out_proj_kernel(x_ref, hm_ref, ha_ref, wm_ref, wa_ref, o_ref):
    o_ref[...] = x_ref[...] + _dot(hm_ref[...], wm_ref[...]) + _dot(ha_ref[...], wa_ref[...])


def _out_proj(x, hm, ha, w_out, layer):
    m, d = x.shape
    tm = _row_tile(m, 512)
    return pl.pallas_call(
        _out_proj_kernel,
        out_shape=jax.ShapeDtypeStruct((m, d), F32),
        grid=(m // tm,),
        in_specs=[pl.BlockSpec((tm, d), lambda i: (i, 0)),
                  pl.BlockSpec((tm, M_WIDTH), lambda i: (i, 0)),
                  pl.BlockSpec((tm, A_WIDTH), lambda i: (i, 0)),
                  pl.BlockSpec((None, M_WIDTH, d), lambda i: (layer, 0, 0)),
                  pl.BlockSpec((None, A_WIDTH, d), lambda i: (layer, 1, 0))],
        out_specs=pl.BlockSpec((tm, d), lambda i: (i, 0)),
        compiler_params=_params(("parallel",)),
        name="out_proj",
    )(x, hm, ha, w_out, w_out)


def _ffn_kernel(x_ref, g_ref, wu_ref, wd_ref, o_ref, h_scr):
    f = pl.program_id(1)

    @pl.when(f == 0)
    def _():
        x = x_ref[...]
        h_scr[...] = _rms(x, g_ref[...]).astype(BF16)
        o_ref[...] = x

    u = _dot(h_scr[...], wu_ref[...])
    a = jnp.square(jnp.maximum(u, 0.0)).astype(BF16)
    o_ref[...] += _dot(a, wd_ref[...])


def _ffn(x, g_ffn, w_up, w_down, layer):
    m, d = x.shape
    d_ff = w_up.shape[-1]
    tm = _row_tile(m, 512)
    tf = 1024
    return pl.pallas_call(
        _ffn_kernel,
        out_shape=jax.ShapeDtypeStruct((m, d), F32),
        grid=(m // tm, d_ff // tf),
        in_specs=[pl.BlockSpec((tm, d), lambda i, f: (i, 0)),
                  pl.BlockSpec((None, 1, d), lambda i, f: (layer, 0, 0)),
                  pl.BlockSpec((None, d, tf), lambda i, f: (layer, 0, f)),
                  pl.BlockSpec((None, tf, d), lambda i, f: (layer, f, 0))],
        out_specs=pl.BlockSpec((tm, d), lambda i, f: (i, 0)),
        scratch_shapes=[pltpu.VMEM((tm, d), BF16)],
        compiler_params=_params(("parallel", "arbitrary")),
        name="ffn",
    )(x, g_ffn, w_up, w_down)


def _layer(x, wts, layer, lm, la, cs, bias, sink, ck, cv, c0, n0, m0, cache_len):
    b, s, d = x.shape
    x2 = x.reshape(b * s, d)
    proj, gates = _in_proj(x2, wts["g_mix"], wts["w_main"], wts["w_gate"], layer)
    proj = proj.reshape(b, s, PROJ_W)
    gates = gates.reshape(b, s, GATE_W)
    hm, c_new, n_new, m_new = _mlstm(proj, gates, wts["gbias"], wts["g_mo"], wts["consts"][1], c0, n0, m0, layer, lm)
    ha, k_new, v_new = _swa(proj, ck, cv, wts["g_q"], wts["g_k"], wts["g_ao"], bias, sink, wts["consts"],
                            layer, la, cs, cache_len)
    x2 = _out_proj(x2, hm.reshape(b * s, M_WIDTH), ha.reshape(b * s, A_WIDTH), wts["w_out"], layer)
    x2 = _ffn(x2, wts["g_ffn"], wts["w_up"], wts["w_down"], layer)
    return x2.reshape(b, s, d), k_new, v_new, c_new, n_new, m_new


def _head_consts():
    hd = np.arange(A_KV_WIDTH) // A_HEAD_DIM
    e4 = (hd[:, None] == hd[None, :]).astype(np.float32)
    eye = np.eye(A_KV_WIDTH, dtype=np.float32)
    return tuple(jnp.asarray(a, BF16) for a in (e4, eye))


def kernel(x_prompt, x_sample, cache_k, cache_v, state_C, state_n, state_m, rel_bias, g_mix, w_in, b_i, b_f, g_q, g_k, sinks, g_mo, g_ao, w_out, g_ffn, w_up, w_down):
    depth = w_in.shape[0]
    bp, sp, d = x_prompt.shape
    bs, ls, _ = x_sample.shape
    n_win = cache_k.shape[2]
    assert sp % CHUNK == 0 and n_win == WINDOW and ls % 16 == 0 and ls <= CHUNK
    lm = M_CHUNK if sp % M_CHUNK == 0 else CHUNK

    o = np.cumsum((0, QK_W, QK_W, M_WIDTH, M_WIDTH, M_HEADS, M_HEADS, A_WIDTH, A_KV_WIDTH, A_KV_WIDTH))
    seg = [w_in[:, :, o[i]:o[i + 1]] for i in range(9)]
    aq = jnp.swapaxes(seg[6].reshape(depth, d, A_KV_HEADS, A_GROUP, A_HEAD_DIM), 2, 3).reshape(depth, d, A_WIDTH)
    w_main = jnp.concatenate([seg[0], seg[1], seg[2], seg[3], aq, seg[7], seg[8]], axis=-1).astype(BF16)
    w_gate = jnp.concatenate([seg[4], seg[5], jnp.zeros((depth, d, GATE_W - 2 * M_HEADS), F32)], axis=-1).astype(BF16)
    gbias = jnp.concatenate([b_i, b_f, jnp.zeros((depth, GATE_W - 2 * M_HEADS), F32)], axis=-1)

    def regroup(a):
        rest = a.shape[2:]
        a = a.reshape(depth, A_KV_HEADS, A_GROUP, A_HEAD_DIM, *rest)
        return jnp.swapaxes(a, 1, 2).reshape(depth, A_WIDTH, *rest)

    w_out_b = jnp.concatenate([w_out[:, :M_WIDTH], regroup(w_out[:, M_WIDTH:])], axis=1).astype(BF16)
    wts = {
        "g_mix": g_mix.reshape(depth, 1, d), "w_main": w_main, "w_gate": w_gate,
        "gbias": gbias.reshape(depth, 1, GATE_W), "g_mo": g_mo.reshape(depth, 1, M_WIDTH),
        "g_q": jnp.tile(g_q * A_SCALE, (1, A_HEADS)).reshape(depth, 1, A_WIDTH),
        "g_k": jnp.tile(g_k, (1, A_KV_HEADS)).reshape(depth, 1, A_KV_WIDTH),
        "g_ao": regroup(g_ao).reshape(depth, 1, A_WIDTH),
        "w_out": w_out_b, "g_ffn": g_ffn.reshape(depth, 1, d),
        "w_up": w_up.astype(BF16), "w_down": w_down.astype(BF16),
        "consts": _head_consts(),
    }

    bias = _bias_table(rel_bias)

    def bias_t(lq, cache_len):
        lk = WINDOW + lq
        t = bias[:, :lq, :lk].reshape(A_KV_HEADS, A_GROUP, lq, lk)
        t = t.transpose(0, 3, 1, 2).reshape(A_KV_HEADS, lk, A_GROUP * lq)
        n_prev = np.arange((WINDOW - cache_len) // lq + 1) * lq + cache_len
        dead = np.arange(lk)[None, :] < WINDOW - n_prev[:, None]
        return t[None] + jnp.asarray(np.where(dead, NEG_INF, 0.0)[:, None, :, None], F32)

    def sink_rows(lq):
        return jnp.repeat(sinks.reshape(depth, A_KV_HEADS, 1, A_GROUP), lq, axis=-1)

    zk = jnp.zeros((bp, WINDOW, A_KV_WIDTH), F32)
    zc = jnp.zeros((bp, M_HEADS, M_DV, M_DK), F32)
    zn = jnp.zeros((bp, M_HEADS, M_DK), F32)
    zm = jnp.zeros((bp, 1, M_HEADS), F32)

    xp, xs = x_prompt, x_sample
    bias_p, bias_s, sink_p, sink_s = bias_t(CHUNK, 0), bias_t(ls, WINDOW), sink_rows(CHUNK), sink_rows(ls)
    cs = A_CHUNKS_PER_STEP if sp % (A_CHUNKS_PER_STEP * CHUNK) == 0 else 1
    outs_p, outs_s = [], []
    for l in range(depth):
        xp, *st = _layer(xp, wts, l, lm, CHUNK, cs, bias_p, sink_p, zk, zk, zc, zn, zm, 0)
        outs_p.append(st)
        xs, *st = _layer(xs, wts, l, ls, ls, 1, bias_s, sink_s,
                         cache_k[l].reshape(bs, n_win, A_KV_WIDTH), cache_v[l].reshape(bs, n_win, A_KV_WIDTH),
                         state_C[l], state_n[l], state_m[l].reshape(bs, 1, M_HEADS), WINDOW)
        outs_s.append(st)

    def stack(outs, b):
        k, v, c, n, m = (jnp.stack([o_[i] for o_ in outs]) for i in range(5))
        return (k.reshape(depth, b, WINDOW, A_KV_HEADS, A_HEAD_DIM), v.reshape(depth, b, WINDOW, A_KV_HEADS, A_HEAD_DIM),
                c, n, m.reshape(depth, b, M_HEADS))

    return (xp, xs) + stack(outs_p, bp) + stack(outs_s, bs)
```

```python
import functools

import jax
import jax.numpy as jnp
import numpy as np
from jax import lax
from jax.experimental import pallas as pl
from jax.experimental.pallas import tpu as pltpu

F32 = jnp.float32
BF16 = jnp.bfloat16

CHUNK = 64
M_CHUNK = 256
A_CHUNKS_PER_STEP = 4
M_HEADS = 4
M_DK = 128
M_DV = 256
M_WIDTH = M_HEADS * M_DV
A_HEADS = 16
A_KV_HEADS = 4
A_HEAD_DIM = 64
A_GROUP = A_HEADS // A_KV_HEADS
A_WIDTH = A_HEADS * A_HEAD_DIM
A_KV_WIDTH = A_KV_HEADS * A_HEAD_DIM
WINDOW = 128
N_BUCKETS = 32
MAX_DISTANCE = 128
EPS = 1e-6
NEG_INF = -1e30
M_SCALE = M_DK ** -0.5
A_SCALE = A_HEAD_DIM ** -0.5

QK_W = M_HEADS * M_DK
PROJ_W = 2 * QK_W + 2 * M_WIDTH + A_WIDTH + 2 * A_KV_WIDTH
GATE_W = 128

V7X_VMEM_LIMIT = 56 * 1024 * 1024

NT_DIMS = (((1,), (1,)), ((), ()))


def _params(sem, vmem=V7X_VMEM_LIMIT):
    return pltpu.CompilerParams(dimension_semantics=sem, vmem_limit_bytes=vmem)


def _row_tile(m, cap):
    t = min(m, cap)
    while m % t:
        t //= 2
    return t


def _dot(a, b):
    return jnp.dot(a, b, preferred_element_type=F32)


def _dot_nt(a, b):
    return lax.dot_general(a, b, NT_DIMS, preferred_element_type=F32)


def _t5_bucket(rel):
    half = N_BUCKETS // 2
    exact = half // 2
    n = np.abs(rel)
    large = exact + (np.log(np.maximum(n, 1) / exact) / np.log(MAX_DISTANCE / exact) * (half - exact)).astype(np.int32)
    large = np.minimum(large, half - 1)
    return (rel > 0).astype(np.int32) * half + np.where(n < exact, n, large).astype(np.int32)


def _bias_kernel(rel_ref, map_ref, out_ref):
    bmap = map_ref[...]
    for h in range(A_HEADS):
        acc = jnp.zeros(bmap.shape, F32)
        for b in range(N_BUCKETS):
            acc = jnp.where(bmap == b, rel_ref[b, h], acc)
        out_ref[h] = acc


def _bias_table(rel_bias):
    lk = WINDOW + CHUNK
    rel = (np.arange(lk)[None, :] - WINDOW) - np.arange(CHUNK)[:, None]
    bmap = jnp.asarray(_t5_bucket(rel), jnp.int32)
    return pl.pallas_call(
        _bias_kernel,
        out_shape=jax.ShapeDtypeStruct((A_HEADS, CHUNK, lk), F32),
        in_specs=[pl.BlockSpec(memory_space=pltpu.SMEM),
                  pl.BlockSpec(memory_space=pltpu.VMEM)],
        out_specs=pl.BlockSpec(memory_space=pltpu.VMEM),
        name="t5_bias_table",
    )(rel_bias, bmap)


def _rms(x, g):
    return x * lax.rsqrt(jnp.mean(x * x, axis=-1, keepdims=True) + EPS) * g


def _in_proj_kernel(x_ref, g_ref, w_ref, o_ref, gate_ref, *, sub):
    for r in range(0, x_ref.shape[0], sub):
        h = _rms(x_ref[r:r + sub, :], g_ref[...]).astype(BF16)
        acc = _dot(h, w_ref[...])
        o_ref[r:r + sub, :] = acc[:, :PROJ_W].astype(o_ref.dtype)
        gate_ref[r:r + sub, :] = acc[:, PROJ_W:]


def _in_proj(x, g_mix, w_all, layer):
    m, d = x.shape
    tm = _row_tile(m, 512)
    n = PROJ_W + GATE_W
    return pl.pallas_call(
        functools.partial(_in_proj_kernel, sub=min(tm, 256)),
        out_shape=(jax.ShapeDtypeStruct((m, PROJ_W), BF16), jax.ShapeDtypeStruct((m, GATE_W), F32)),
        grid=(m // tm,),
        in_specs=[pl.BlockSpec((tm, d), lambda i: (i, 0)),
                  pl.BlockSpec((None, 1, d), lambda i: (layer, 0, 0)),
                  pl.BlockSpec((None, d, n), lambda i: (layer, 0, 0), pipeline_mode=pl.Buffered(1))],
        out_specs=(pl.BlockSpec((tm, PROJ_W), lambda i: (i, 0)),
                   pl.BlockSpec((tm, GATE_W), lambda i: (i, 0))),
        compiler_params=_params(("parallel",)),
        name="in_proj",
    )(x, g_mix, w_all)


def _mlstm_kernel(q_ref, k_ref, v_ref, og_ref, gate_ref, gbias_ref, gmo_ref, eye_ref, c0_ref, n0_ref, m0_ref,
                  hm_ref, cout_ref, nout_ref, mout_ref, c_scr, n_scr, m_scr, *, L):
    c = pl.program_id(1)
    last = pl.num_programs(1) - 1

    @pl.when(c == 0)
    def _():
        c_scr[...] = c0_ref[0]
        n_scr[...] = n0_ref[0]
        for h in range(M_HEADS):
            m_scr[h:h + 1, :] = jnp.broadcast_to(m0_ref[0, :, h:h + 1], (1, 128))

    a_all = gate_ref[0] + gbias_ref[...]
    logf = jnp.minimum(a_all, 0.0) - jnp.log(1.0 + jnp.exp(-jnp.abs(a_all)))
    row = lax.broadcasted_iota(jnp.int32, (L, GATE_W), 0)
    b_all = logf
    k = 1
    while k < L:
        b_all = b_all + jnp.where(row >= k, pltpu.roll(b_all, k, axis=0), 0.0)
        k *= 2
    b_sh = pltpu.roll(b_all, GATE_W - M_HEADS, axis=1)
    a_i = a_all - b_sh
    cmax = a_i
    k = 1
    while k < L:
        cmax = jnp.maximum(cmax, jnp.where(row >= k, pltpu.roll(cmax, k, axis=0), NEG_INF))
        k *= 2
    a_t = a_i.T
    tri = lax.broadcasted_iota(jnp.int32, (L, L), 0) >= lax.broadcasted_iota(jnp.int32, (L, L), 1)
    eye = eye_ref[...]

    for h in range(M_HEADS):
        b_col = b_sh[:, h:h + 1]
        a_col = a_i[:, h:h + 1]
        a_row = a_t[h:h + 1, :]
        m_prev = m_scr[h:h + 1, 0:1]
        m_run = jnp.maximum(cmax[:, h:h + 1], m_prev)
        decay_w = jnp.exp(jnp.where(tri, a_row - m_run, NEG_INF))
        qh = q_ref[0, :, h * M_DK:(h + 1) * M_DK]
        kh = k_ref[0, :, h * M_DK:(h + 1) * M_DK]
        vh = v_ref[0, :, h * M_DV:(h + 1) * M_DV]
        w = decay_w * (_dot_nt(qh, kh) * M_SCALE)
        g = jnp.exp(m_prev - m_run)
        cst = c_scr[h]
        n_row = n_scr[h:h + 1, :]
        num = _dot(w.astype(BF16), vh) + g * _dot_nt(qh, cst.astype(BF16))
        den = (jnp.sum(w, axis=1, keepdims=True)
               + g * jnp.sum(qh.astype(F32) * n_row, axis=1, keepdims=True))
        inv = 1.0 / jnp.maximum(jnp.abs(den), jnp.exp(-(b_col + m_run)))
        ms = jnp.mean(num * num, axis=1, keepdims=True)
        scale = inv * lax.rsqrt(inv * inv * ms + EPS)
        og = og_ref[0, :, h * M_DV:(h + 1) * M_DV].astype(F32)
        hm_ref[0, :, h * M_DV:(h + 1) * M_DV] = (
            num * scale * gmo_ref[:, h * M_DV:(h + 1) * M_DV] * jax.nn.sigmoid(og)).astype(hm_ref.dtype)

        b_last = b_col[L - 1:L, :]
        m_end = b_last + m_run[L - 1:L, :]
        wk = jnp.exp(b_last + a_col - m_end)
        decay = jnp.exp(b_last + m_prev - m_end)
        vw = (vh.astype(F32) * wk).astype(BF16)
        vw_t = _dot_nt(eye, vw).astype(BF16)
        c_new = decay * cst + M_SCALE * _dot(vw_t, kh)
        n_new = decay * n_row + M_SCALE * jnp.sum(kh.astype(F32) * wk, axis=0, keepdims=True)
        c_scr[h] = c_new
        n_scr[h:h + 1, :] = n_new
        m_scr[h:h + 1, :] = jnp.broadcast_to(m_end, (1, 128))

        @pl.when(c == last)
        def _():
            cout_ref[0, h] = c_new
            nout_ref[0, h:h + 1, :] = n_new
            mout_ref[0, :, h:h + 1] = m_end


def _mlstm(proj, gates, gbias, g_mo, eye, c0, n0, m0, layer, L):
    b, s, _ = proj.shape
    nc = s // L
    out_shape = (jax.ShapeDtypeStruct((b, s, M_WIDTH), BF16),
                 jax.ShapeDtypeStruct((b, M_HEADS, M_DV, M_DK), F32),
                 jax.ShapeDtypeStruct((b, M_HEADS, M_DK), F32),
                 jax.ShapeDtypeStruct((b, 1, M_HEADS), F32))
    return pl.pallas_call(
        functools.partial(_mlstm_kernel, L=L),
        out_shape=out_shape,
        grid=(b, nc),
        in_specs=[pl.BlockSpec((1, L, QK_W), lambda i, c: (i, c, 0)),
                  pl.BlockSpec((1, L, QK_W), lambda i, c: (i, c, 1)),
                  pl.BlockSpec((1, L, M_WIDTH), lambda i, c: (i, c, 1)),
                  pl.BlockSpec((1, L, M_WIDTH), lambda i, c: (i, c, 2)),
                  pl.BlockSpec((1, L, GATE_W), lambda i, c: (i, c, 0)),
                  pl.BlockSpec((None, 1, GATE_W), lambda i, c: (layer, 0, 0)),
                  pl.BlockSpec((None, 1, M_WIDTH), lambda i, c: (layer, 0, 0)),
                  pl.BlockSpec((M_DV, M_DV), lambda i, c: (0, 0)),
                  pl.BlockSpec((1, M_HEADS, M_DV, M_DK), lambda i, c: (i, 0, 0, 0)),
                  pl.BlockSpec((1, M_HEADS, M_DK), lambda i, c: (i, 0, 0)),
                  pl.BlockSpec((1, 1, M_HEADS), lambda i, c: (i, 0, 0))],
        out_specs=(pl.BlockSpec((1, L, M_WIDTH), lambda i, c: (i, c, 0)),
                   pl.BlockSpec((1, M_HEADS, M_DV, M_DK), lambda i, c: (i, 0, 0, 0)),
                   pl.BlockSpec((1, M_HEADS, M_DK), lambda i, c: (i, 0, 0)),
                   pl.BlockSpec((1, 1, M_HEADS), lambda i, c: (i, 0, 0))),
        scratch_shapes=[pltpu.VMEM((M_HEADS, M_DV, M_DK), F32),
                        pltpu.VMEM((M_HEADS, M_DK), F32),
                        pltpu.VMEM((M_HEADS, 128), F32)],
        compiler_params=_params(("arbitrary", "arbitrary")),
        name="mlstm",
    )(proj, proj, proj, proj, gates, gbias, g_mo, eye, c0, n0, m0)


def _swa_kernel(q_ref, k_ref, v_ref, ck_ref, cv_ref, gq_ref, gk_ref, gao_ref, bias_ref, sink_ref,
                e4_ref, eye_ref, ha_ref, cko_ref, cvo_ref, khist, vhist, *, L, CS, cache_len):
    c = pl.program_id(1)
    last = pl.num_programs(1) - 1
    T = CS * L
    lk = WINDOW + L
    gl = A_GROUP * L
    kv_lane = lax.broadcasted_iota(jnp.int32, (1, A_KV_WIDTH), 1) // A_HEAD_DIM

    def lane_masked(kn):
        knb = kn.astype(BF16)
        return [jnp.where(kv_lane == kv, knb, 0.0) for kv in range(A_KV_HEADS)]

    @pl.when(c == 0)
    def _():
        khist[...] = ck_ref[0]
        vhist[...] = cv_ref[0]

    q = q_ref[0].astype(F32)
    k = k_ref[0].astype(F32)
    sq = jnp.concatenate([q[:, g * A_KV_WIDTH:(g + 1) * A_KV_WIDTH] for g in range(A_GROUP)] + [k], axis=0)
    sq = sq * sq
    sq_hi = sq.astype(BF16)
    sq_lo = (sq - sq_hi.astype(F32)).astype(BF16)
    ss = _dot(jnp.concatenate([sq_hi, sq_lo], axis=0), e4_ref[...])
    rinv = lax.rsqrt((ss[:5 * T] + ss[5 * T:]) * (1.0 / A_HEAD_DIM) + EPS)
    qn = [(q[:, g * A_KV_WIDTH:(g + 1) * A_KV_WIDTH] * rinv[g * T:(g + 1) * T, :]
           * gq_ref[:, g * A_KV_WIDTH:(g + 1) * A_KV_WIDTH]).astype(BF16) for g in range(A_GROUP)]
    kn = k * rinv[A_GROUP * T:, :] * gk_ref[...]

    k_all = jnp.concatenate([khist[...], kn], axis=0)
    v_all = jnp.concatenate([vhist[...], v_ref[0].astype(F32)], axis=0)
    vb_all = v_all.astype(BF16)

    for j in range(CS):
        rows = slice(j * L, j * L + lk)
        qst = jnp.concatenate([qn[g][j * L:(j + 1) * L, :] for g in range(A_GROUP)], axis=0)
        km = jnp.concatenate(lane_masked(k_all[rows]), axis=0)
        s_all = _dot_nt(km, qst)
        v_t = _dot_nt(eye_ref[...], vb_all[rows]).astype(BF16)
        var = jnp.minimum((c * CS + j) * L, WINDOW - cache_len) // L
        o_t = []
        for kv in range(A_KV_HEADS):
            s = s_all[kv * lk:(kv + 1) * lk, :] + bias_ref[var, kv]
            sk = sink_ref[kv]
            mx = jnp.maximum(jnp.max(s, axis=0, keepdims=True), sk)
            p = jnp.exp(s - mx)
            den = jnp.sum(p, axis=0, keepdims=True) + jnp.exp(sk - mx)
            o = _dot(v_t[kv * A_HEAD_DIM:(kv + 1) * A_HEAD_DIM, :], p.astype(BF16))
            o_t.append(o * (1.0 / den))
        o_all = jnp.concatenate(o_t, axis=0).T
        ha = jnp.concatenate([o_all[g * L:(g + 1) * L, :] for g in range(A_GROUP)], axis=1)
        ha_ref[0, j * L:(j + 1) * L, :] = _rms(ha, gao_ref[...]).astype(ha_ref.dtype)

    k_win = k_all[T:T + WINDOW]
    v_win = v_all[T:T + WINDOW]
    khist[...] = k_win
    vhist[...] = v_win

    @pl.when(c == last)
    def _():
        cko_ref[0] = k_win
        cvo_ref[0] = v_win


def _swa(proj, ck, cv, g_q, g_k, g_ao, bias, sink, consts, layer, L, CS, cache_len):
    b, s, _ = proj.shape
    T = CS * L
    lk = WINDOW + L
    gl = A_GROUP * L
    nvar = bias.shape[0]
    kv_blk = (2 * QK_W + 2 * M_WIDTH + A_WIDTH) // A_KV_WIDTH
    q_blk = (2 * QK_W + 2 * M_WIDTH) // A_WIDTH
    e4, eye = consts
    out_shape = (jax.ShapeDtypeStruct((b, s, A_WIDTH), BF16),
                 jax.ShapeDtypeStruct((b, WINDOW, A_KV_WIDTH), F32),
                 jax.ShapeDtypeStruct((b, WINDOW, A_KV_WIDTH), F32))
    full = lambda *shape: pl.BlockSpec(shape, lambda i, c: (0,) * len(shape))
    return pl.pallas_call(
        functools.partial(_swa_kernel, L=L, CS=CS, cache_len=cache_len),
        out_shape=out_shape,
        grid=(b, s // T),
        in_specs=[pl.BlockSpec((1, T, A_WIDTH), lambda i, c: (i, c, q_blk)),
                  pl.BlockSpec((1, T, A_KV_WIDTH), lambda i, c: (i, c, kv_blk)),
                  pl.BlockSpec((1, T, A_KV_WIDTH), lambda i, c: (i, c, kv_blk + 1)),
                  pl.BlockSpec((1, WINDOW, A_KV_WIDTH), lambda i, c: (i, 0, 0)),
                  pl.BlockSpec((1, WINDOW, A_KV_WIDTH), lambda i, c: (i, 0, 0)),
                  pl.BlockSpec((None, 1, A_WIDTH), lambda i, c: (layer, 0, 0)),
                  pl.BlockSpec((None, 1, A_KV_WIDTH), lambda i, c: (layer, 0, 0)),
                  pl.BlockSpec((None, 1, A_WIDTH), lambda i, c: (layer, 0, 0)),
                  full(nvar, A_KV_HEADS, lk, gl),
                  pl.BlockSpec((None, A_KV_HEADS, 1, gl), lambda i, c: (layer, 0, 0, 0)),
                  full(A_KV_WIDTH, A_KV_WIDTH),
                  full(A_KV_WIDTH, A_KV_WIDTH)],
        out_specs=(pl.BlockSpec((1, T, A_WIDTH), lambda i, c: (i, c, 0)),
                   pl.BlockSpec((1, WINDOW, A_KV_WIDTH), lambda i, c: (i, 0, 0)),
                   pl.BlockSpec((1, WINDOW, A_KV_WIDTH), lambda i, c: (i, 0, 0))),
        scratch_shapes=[pltpu.VMEM((WINDOW, A_KV_WIDTH), F32),
                        pltpu.VMEM((WINDOW, A_KV_WIDTH), F32)],
        compiler_params=_params(("arbitrary", "arbitrary")),
        name="swa",
    )(proj, proj, proj, ck, cv, g_q, g_k, g_ao, bias, sink, e4, eye)


def _out_proj_kernel(x_ref, hm_ref, ha_ref, wm_ref, wa_ref, o_ref):
    o_ref[...] = x_ref[...] + _dot(hm_ref[...], wm_ref[...]) + _dot(ha_ref[...], wa_ref[...])


def _out_proj(x, hm, ha, w_out, layer):
    m, d = x.shape
    tm = _row_tile(m, 512)
    return pl.pallas_call(
        _out_proj_kernel,
        out_shape=jax.ShapeDtypeStruct((m, d), F32),
        grid=(m // tm,),
        in_specs=[pl.BlockSpec((tm, d), lambda i: (i, 0)),
                  pl.BlockSpec((tm, M_WIDTH), lambda i: (i, 0)),
                  pl.BlockSpec((tm, A_WIDTH), lambda i: (i, 0)),
                  pl.BlockSpec((None, M_WIDTH, d), lambda i: (layer, 0, 0)),
                  pl.BlockSpec((None, A_WIDTH, d), lambda i: (layer, 1, 0))],
        out_specs=pl.BlockSpec((tm, d), lambda i: (i, 0)),
        compiler_params=_params(("parallel",)),
        name="out_proj",
    )(x, hm, ha, w_out, w_out)


def _ffn_kernel(x_ref, g_ref, wu_ref, wd_ref, o_ref, h_scr):
    f = pl.program_id(1)

    @pl.when(f == 0)
    def _():
        x = x_ref[...]
        h_scr[...] = _rms(x, g_ref[...]).astype(BF16)
        o_ref[...] = x

    u = _dot(h_scr[...], wu_ref[...])
    a = jnp.square(jnp.maximum(u, 0.0)).astype(BF16)
    o_ref[...] += _dot(a, wd_ref[...])


def _ffn(x, g_ffn, w_up, w_down, layer):
    m, d = x.shape
    d_ff = w_up.shape[-1]
    tm = _row_tile(m, 512)
    tf = 1024
    return pl.pallas_call(
        _ffn_kernel,
        out_shape=jax.ShapeDtypeStruct((m, d), F32),
        grid=(m // tm, d_ff // tf),
        in_specs=[pl.BlockSpec((tm, d), lambda i, f: (i, 0)),
                  pl.BlockSpec((None, 1, d), lambda i, f: (layer, 0, 0)),
                  pl.BlockSpec((None, d, tf), lambda i, f: (layer, 0, f)),
                  pl.BlockSpec((None, tf, d), lambda i, f: (layer, f, 0))],
        out_specs=pl.BlockSpec((tm, d), lambda i, f: (i, 0)),
        scratch_shapes=[pltpu.VMEM((tm, d), BF16)],
        compiler_params=_params(("parallel", "arbitrary")),
        name="ffn",
    )(x, g_ffn, w_up, w_down)


def _layer(x, wts, layer, lm, la, cs, bias, sink, ck, cv, c0, n0, m0, cache_len):
    b, s, d = x.shape
    x2 = x.reshape(b * s, d)
    proj, gates = _in_proj(x2, wts["g_mix"], wts["w_all"], layer)
    proj = proj.reshape(b, s, PROJ_W)
    gates = gates.reshape(b, s, GATE_W)
    hm, c_new, n_new, m_new = _mlstm(proj, gates, wts["gbias"], wts["g_mo"], wts["consts"][1], c0, n0, m0, layer, lm)
    ha, k_new, v_new = _swa(proj, ck, cv, wts["g_q"], wts["g_k"], wts["g_ao"], bias, sink, wts["consts"],
                            layer, la, cs, cache_len)
    x2 = _out_proj(x2, hm.reshape(b * s, M_WIDTH), ha.reshape(b * s, A_WIDTH), wts["w_out"], layer)
    x2 = _ffn(x2, wts["g_ffn"], wts["w_up"], wts["w_down"], layer)
    return x2.reshape(b, s, d), k_new, v_new, c_new, n_new, m_new


def _head_consts():
    hd = np.arange(A_KV_WIDTH) // A_HEAD_DIM
    e4 = (hd[:, None] == hd[None, :]).astype(np.float32)
    eye = np.eye(A_KV_WIDTH, dtype=np.float32)
    return tuple(jnp.asarray(a, BF16) for a in (e4, eye))


def kernel(x_prompt, x_sample, cache_k, cache_v, state_C, state_n, state_m, rel_bias, g_mix, w_in, b_i, b_f, g_q, g_k, sinks, g_mo, g_ao, w_out, g_ffn, w_up, w_down):
    depth = w_in.shape[0]
    bp, sp, d = x_prompt.shape
    bs, ls, _ = x_sample.shape
    n_win = cache_k.shape[2]
    assert sp % CHUNK == 0 and n_win == WINDOW and ls % 16 == 0 and ls <= CHUNK
    lm = M_CHUNK if sp % M_CHUNK == 0 else CHUNK

    o = np.cumsum((0, QK_W, QK_W, M_WIDTH, M_WIDTH, M_HEADS, M_HEADS, A_WIDTH, A_KV_WIDTH, A_KV_WIDTH))
    seg = [w_in[:, :, o[i]:o[i + 1]] for i in range(9)]
    aq = jnp.swapaxes(seg[6].reshape(depth, d, A_KV_HEADS, A_GROUP, A_HEAD_DIM), 2, 3).reshape(depth, d, A_WIDTH)
    w_all = jnp.concatenate([seg[0], seg[1], seg[2], seg[3], aq, seg[7], seg[8], seg[4], seg[5],
                             jnp.zeros((depth, d, GATE_W - 2 * M_HEADS), F32)], axis=-1).astype(BF16)
    gbias = jnp.concatenate([b_i, b_f, jnp.zeros((depth, GATE_W - 2 * M_HEADS), F32)], axis=-1)

    def regroup(a):
        rest = a.shape[2:]
        a = a.reshape(depth, A_KV_HEADS, A_GROUP, A_HEAD_DIM, *rest)
        return jnp.swapaxes(a, 1, 2).reshape(depth, A_WIDTH, *rest)

    w_out_b = jnp.concatenate([w_out[:, :M_WIDTH], regroup(w_out[:, M_WIDTH:])], axis=1).astype(BF16)
    wts = {
        "g_mix": g_mix.reshape(depth, 1, d), "w_all": w_all,
        "gbias": gbias.reshape(depth, 1, GATE_W), "g_mo": g_mo.reshape(depth, 1, M_WIDTH),
        "g_q": jnp.tile(g_q * A_SCALE, (1, A_HEADS)).reshape(depth, 1, A_WIDTH),
        "g_k": jnp.tile(g_k, (1, A_KV_HEADS)).reshape(depth, 1, A_KV_WIDTH),
        "g_ao": regroup(g_ao).reshape(depth, 1, A_WIDTH),
        "w_out": w_out_b, "g_ffn": g_ffn.reshape(depth, 1, d),
        "w_up": w_up.astype(BF16), "w_down": w_down.astype(BF16),
        "consts": _head_consts(),
    }

    bias = _bias_table(rel_bias)

    def bias_t(lq, cache_len):
        lk = WINDOW + lq
        t = bias[:, :lq, :lk].reshape(A_KV_HEADS, A_GROUP, lq, lk)
        t = t.transpose(0, 3, 1, 2).reshape(A_KV_HEADS, lk, A_GROUP * lq)
        n_prev = np.arange((WINDOW - cache_len) // lq + 1) * lq + cache_len
        dead = np.arange(lk)[None, :] < WINDOW - n_prev[:, None]
        return t[None] + jnp.asarray(np.where(dead, NEG_INF, 0.0)[:, None, :, None], F32)

    def sink_rows(lq):
        return jnp.repeat(sinks.reshape(depth, A_KV_HEADS, 1, A_GROUP), lq, axis=-1)

    zk = jnp.zeros((bp, WINDOW, A_KV_WIDTH), F32)
    zc = jnp.zeros((bp, M_HEADS, M_DV, M_DK), F32)
    zn = jnp.zeros((bp, M_HEADS, M_DK), F32)
    zm = jnp.zeros((bp, 1, M_HEADS), F32)

    xp, xs = x_prompt, x_sample
    bias_p, bias_s, sink_p, sink_s = bias_t(CHUNK, 0), bias_t(ls, WINDOW), sink_rows(CHUNK), sink_rows(ls)
    cs = A_CHUNKS_PER_STEP if sp % (A_CHUNKS_PER_STEP * CHUNK) == 0 else 1
    outs_p, outs_s = [], []
    for l in range(depth):
        xp, *st = _layer(xp, wts, l, lm, CHUNK, cs, bias_p, sink_p, zk, zk, zc, zn, zm, 0)
        outs_p.append(st)
        xs, *st = _layer(xs, wts, l, ls, ls, 1, bias_s, sink_s,
                         cache_k[l].reshape(bs, n_win, A_KV_WIDTH), cache_v[l].reshape(bs, n_win, A_KV_WIDTH),
                         state_C[l], state_n[l], state_m[l].reshape(bs, 1, M_HEADS), WINDOW)
        outs_s.append(st)

    def stack(outs, b):
        k, v, c, n, m = (jnp.stack([o_[i] for o_ in outs]) for i in range(5))
        return (k.reshape(depth, b, WINDOW, A_KV_HEADS, A_HEAD_DIM), v.reshape(depth, b, WINDOW, A_KV_HEADS, A_HEAD_DIM),
                c, n, m.reshape(depth, b, M_HEADS))

    return (xp, xs) + stack(outs_p, bp) + stack(outs_s, bs)
```

```python
import functools

import jax
import jax.numpy as jnp
import numpy as np
from jax import lax
from jax.experimental import pallas as pl
from jax.experimental.pallas import tpu as pltpu

F32 = jnp.float32
BF16 = jnp.bfloat16

CHUNK = 64
M_CHUNK = 256
A_CHUNKS_PER_STEP = 4
M_HEADS = 4
M_DK = 128
M_DV = 256
M_WIDTH = M_HEADS * M_DV
A_HEADS = 16
A_KV_HEADS = 4
A_HEAD_DIM = 64
A_GROUP = A_HEADS // A_KV_HEADS
A_WIDTH = A_HEADS * A_HEAD_DIM
A_KV_WIDTH = A_KV_HEADS * A_HEAD_DIM
WINDOW = 128
N_BUCKETS = 32
MAX_DISTANCE = 128
EPS = 1e-6
NEG_INF = -1e30
M_SCALE = M_DK ** -0.5
A_SCALE = A_HEAD_DIM ** -0.5

QK_W = M_HEADS * M_DK
PROJ_W = 2 * QK_W + 2 * M_WIDTH + A_WIDTH + 2 * A_KV_WIDTH
GATE_W = 128

V7X_VMEM_LIMIT = 56 * 1024 * 1024

NT_DIMS = (((1,), (1,)), ((), ()))


def _params(sem, vmem=V7X_VMEM_LIMIT):
    return pltpu.CompilerParams(dimension_semantics=sem, vmem_limit_bytes=vmem)


def _row_tile(m, cap):
    t = min(m, cap)
    while m % t:
        t //= 2
    return t


def _dot(a, b):
    return jnp.dot(a, b, preferred_element_type=F32)


def _dot_nt(a, b):
    return lax.dot_general(a, b, NT_DIMS, preferred_element_type=F32)


def _t5_bucket(rel):
    half = N_BUCKETS // 2
    exact = half // 2
    n = np.abs(rel)
    large = exact + (np.log(np.maximum(n, 1) / exact) / np.log(MAX_DISTANCE / exact) * (half - exact)).astype(np.int32)
    large = np.minimum(large, half - 1)
    return (rel > 0).astype(np.int32) * half + np.where(n < exact, n, large).astype(np.int32)


def _bias_kernel(rel_ref, map_ref, out_ref):
    bmap = map_ref[...]
    for h in range(A_HEADS):
        acc = jnp.zeros(bmap.shape, F32)
        for b in range(N_BUCKETS):
            acc = jnp.where(bmap == b, rel_ref[b, h], acc)
        out_ref[h] = acc


def _bias_table(rel_bias):
    lk = WINDOW + CHUNK
    rel = (np.arange(lk)[None, :] - WINDOW) - np.arange(CHUNK)[:, None]
    bmap = jnp.asarray(_t5_bucket(rel), jnp.int32)
    return pl.pallas_call(
        _bias_kernel,
        out_shape=jax.ShapeDtypeStruct((A_HEADS, CHUNK, lk), F32),
        in_specs=[pl.BlockSpec(memory_space=pltpu.SMEM),
                  pl.BlockSpec(memory_space=pltpu.VMEM)],
        out_specs=pl.BlockSpec(memory_space=pltpu.VMEM),
        name="t5_bias_table",
    )(rel_bias, bmap)


def _rms(x, g):
    return x * lax.rsqrt(jnp.mean(x * x, axis=-1, keepdims=True) + EPS) * g


def _in_proj_kernel(x_ref, g_ref, w_ref, o_ref, gate_ref, *, sub):
    for r in range(0, x_ref.shape[0], sub):
        h = _rms(x_ref[r:r + sub, :], g_ref[...]).astype(BF16)
        acc = _dot(h, w_ref[...])
        o_ref[r:r + sub, :] = acc[:, :PROJ_W].astype(o_ref.dtype)
        gate_ref[r:r + sub, :] = acc[:, PROJ_W:]


def _in_proj(x, g_mix, w_all, layer):
    m, d = x.shape
    tm = _row_tile(m, 512)
    n = PROJ_W + GATE_W
    return pl.pallas_call(
        functools.partial(_in_proj_kernel, sub=min(tm, 256)),
        out_shape=(jax.ShapeDtypeStruct((m, PROJ_W), BF16), jax.ShapeDtypeStruct((m, GATE_W), F32)),
        grid=(m // tm,),
        in_specs=[pl.BlockSpec((tm, d), lambda i: (i, 0)),
                  pl.BlockSpec((None, 1, d), lambda i: (layer, 0, 0)),
                  pl.BlockSpec((None, d, n), lambda i: (layer, 0, 0), pipeline_mode=pl.Buffered(1))],
        out_specs=(pl.BlockSpec((tm, PROJ_W), lambda i: (i, 0)),
                   pl.BlockSpec((tm, GATE_W), lambda i: (i, 0))),
        compiler_params=_params(("parallel",)),
        name="in_proj",
    )(x, g_mix, w_all)


def _mlstm_kernel(q_ref, k_ref, v_ref, og_ref, gate_ref, gbias_ref, gmo_ref, eye_ref, c0_ref, n0_ref, m0_ref,
                  hm_ref, cout_ref, nout_ref, mout_ref, c_scr, n_scr, m_scr, *, L):
    c = pl.program_id(1)
    last = pl.num_programs(1) - 1

    @pl.when(c == 0)
    def _():
        c_scr[...] = c0_ref[0]
        n_scr[...] = n0_ref[0]
        for h in range(M_HEADS):
            m_scr[h:h + 1, :] = jnp.broadcast_to(m0_ref[0, :, h:h + 1], (1, 128))

    a_all = gate_ref[0] + gbias_ref[...]
    logf = jnp.minimum(a_all, 0.0) - jnp.log(1.0 + jnp.exp(-jnp.abs(a_all)))
    row = lax.broadcasted_iota(jnp.int32, (L, GATE_W), 0)
    b_all = logf
    k = 1
    while k < L:
        b_all = b_all + jnp.where(row >= k, pltpu.roll(b_all, k, axis=0), 0.0)
        k *= 2
    b_sh = pltpu.roll(b_all, GATE_W - M_HEADS, axis=1)
    a_i = a_all - b_sh
    cmax = a_i
    k = 1
    while k < L:
        cmax = jnp.maximum(cmax, jnp.where(row >= k, pltpu.roll(cmax, k, axis=0), NEG_INF))
        k *= 2
    a_t = a_i.T
    tri = lax.broadcasted_iota(jnp.int32, (L, L), 0) >= lax.broadcasted_iota(jnp.int32, (L, L), 1)
    eye = eye_ref[...]

    for h in range(M_HEADS):
        b_col = b_sh[:, h:h + 1]
        a_col = a_i[:, h:h + 1]
        a_row = a_t[h:h + 1, :]
        m_prev = m_scr[h:h + 1, 0:1]
        m_run = jnp.maximum(cmax[:, h:h + 1], m_prev)
        decay_w = jnp.exp(jnp.where(tri, a_row - m_run, NEG_INF))
        qh = q_ref[0, :, h * M_DK:(h + 1) * M_DK]
        kh = k_ref[0, :, h * M_DK:(h + 1) * M_DK]
        vh = v_ref[0, :, h * M_DV:(h + 1) * M_DV]
        w = decay_w * (_dot_nt(qh, kh) * M_SCALE)
        g = jnp.exp(m_prev - m_run)
        cst = c_scr[h]
        n_row = n_scr[h:h + 1, :]
        num = _dot(w.astype(BF16), vh) + g * _dot_nt(qh, cst.astype(BF16))
        den = (jnp.sum(w, axis=1, keepdims=True)
               + g * jnp.sum(qh.astype(F32) * n_row, axis=1, keepdims=True))
        inv = 1.0 / jnp.maximum(jnp.abs(den), jnp.exp(-(b_col + m_run)))
        ms = jnp.mean(num * num, axis=1, keepdims=True)
        scale = inv * lax.rsqrt(inv * inv * ms + EPS)
        og = og_ref[0, :, h * M_DV:(h + 1) * M_DV].astype(F32)
        hm_ref[0, :, h * M_DV:(h + 1) * M_DV] = (
            num * scale * gmo_ref[:, h * M_DV:(h + 1) * M_DV] * jax.nn.sigmoid(og)).astype(hm_ref.dtype)

        b_last = b_col[L - 1:L, :]
        m_end = b_last + m_run[L - 1:L, :]
        wk = jnp.exp(b_last + a_col - m_end)
        decay = jnp.exp(b_last + m_prev - m_end)
        vw = (vh.astype(F32) * wk).astype(BF16)
        vw_t = _dot_nt(eye, vw).astype(BF16)
        c_new = decay * cst + M_SCALE * _dot(vw_t, kh)
        n_new = decay * n_row + M_SCALE * jnp.sum(kh.astype(F32) * wk, axis=0, keepdims=True)
        c_scr[h] = c_new
        n_scr[h:h + 1, :] = n_new
        m_scr[h:h + 1, :] = jnp.broadcast_to(m_end, (1, 128))

        @pl.when(c == last)
        def _():
            cout_ref[0, h] = c_new
            nout_ref[0, h:h + 1, :] = n_new
            mout_ref[0, :, h:h + 1] = m_end


def _mlstm(proj, gates, gbias, g_mo, eye, c0, n0, m0, layer, L):
    b, s, _ = proj.shape
    nc = s // L
    out_shape = (jax.ShapeDtypeStruct((b, s, M_WIDTH), BF16),
                 jax.ShapeDtypeStruct((b, M_HEADS, M_DV, M_DK), F32),
                 jax.ShapeDtypeStruct((b, M_HEADS, M_DK), F32),
                 jax.ShapeDtypeStruct((b, 1, M_HEADS), F32))
    return pl.pallas_call(
        functools.partial(_mlstm_kernel, L=L),
        out_shape=out_shape,
        grid=(b, nc),
        in_specs=[pl.BlockSpec((1, L, QK_W), lambda i, c: (i, c, 0)),
                  pl.BlockSpec((1, L, QK_W), lambda i, c: (i, c, 1)),
                  pl.BlockSpec((1, L, M_WIDTH), lambda i, c: (i, c, 1)),
                  pl.BlockSpec((1, L, M_WIDTH), lambda i, c: (i, c, 2)),
                  pl.BlockSpec((1, L, GATE_W), lambda i, c: (i, c, 0)),
                  pl.BlockSpec((None, 1, GATE_W), lambda i, c: (layer, 0, 0)),
                  pl.BlockSpec((None, 1, M_WIDTH), lambda i, c: (layer, 0, 0)),
                  pl.BlockSpec((M_DV, M_DV), lambda i, c: (0, 0)),
                  pl.BlockSpec((1, M_HEADS, M_DV, M_DK), lambda i, c: (i, 0, 0, 0)),
                  pl.BlockSpec((1, M_HEADS, M_DK), lambda i, c: (i, 0, 0)),
                  pl.BlockSpec((1, 1, M_HEADS), lambda i, c: (i, 0, 0))],
        out_specs=(pl.BlockSpec((1, L, M_WIDTH), lambda i, c: (i, c, 0)),
                   pl.BlockSpec((1, M_HEADS, M_DV, M_DK), lambda i, c: (i, 0, 0, 0)),
                   pl.BlockSpec((1, M_HEADS, M_DK), lambda i, c: (i, 0, 0)),
                   pl.BlockSpec((1, 1, M_HEADS), lambda i, c: (i, 0, 0))),
        scratch_shapes=[pltpu.VMEM((M_HEADS, M_DV, M_DK), F32),
                        pltpu.VMEM((M_HEADS, M_DK), F32),
                        pltpu.VMEM((M_HEADS, 128), F32)],
        compiler_params=_params(("arbitrary", "arbitrary")),
        name="mlstm",
    )(proj, proj, proj, proj, gates, gbias, g_mo, eye, c0, n0, m0)


def _swa_body(c, is_last, q_ref, k_ref, v_ref, ck_ref, cv_ref, gq_ref, gk_ref, gao_ref, bias_ref, sink_ref,
              e4_ref, eye_ref, cko_ref, cvo_ref, khist, vhist, write_ha, *, L, CS, cache_len, side_work=None):
    T = CS * L
    lk = WINDOW + L
    gl = A_GROUP * L
    kv_lane = lax.broadcasted_iota(jnp.int32, (1, A_KV_WIDTH), 1) // A_HEAD_DIM

    def lane_masked(kn):
        knb = kn.astype(BF16)
        return [jnp.where(kv_lane == kv, knb, 0.0) for kv in range(A_KV_HEADS)]

    @pl.when(c == 0)
    def _():
        khist[...] = ck_ref[0]
        vhist[...] = cv_ref[0]

    q = q_ref[0].astype(F32)
    k = k_ref[0].astype(F32)
    sq = jnp.concatenate([q[:, g * A_KV_WIDTH:(g + 1) * A_KV_WIDTH] for g in range(A_GROUP)] + [k], axis=0)
    sq = sq * sq
    sq_hi = sq.astype(BF16)
    sq_lo = (sq - sq_hi.astype(F32)).astype(BF16)
    ss = _dot(jnp.concatenate([sq_hi, sq_lo], axis=0), e4_ref[...])
    rinv = lax.rsqrt((ss[:5 * T] + ss[5 * T:]) * (1.0 / A_HEAD_DIM) + EPS)
    qn = [(q[:, g * A_KV_WIDTH:(g + 1) * A_KV_WIDTH] * rinv[g * T:(g + 1) * T, :]
           * gq_ref[:, g * A_KV_WIDTH:(g + 1) * A_KV_WIDTH]).astype(BF16) for g in range(A_GROUP)]
    kn = k * rinv[A_GROUP * T:, :] * gk_ref[...]

    k_all = jnp.concatenate([khist[...], kn], axis=0)
    v_all = jnp.concatenate([vhist[...], v_ref[0].astype(F32)], axis=0)
    vb_all = v_all.astype(BF16)

    for j in range(CS):
        rows = slice(j * L, j * L + lk)
        qst = jnp.concatenate([qn[g][j * L:(j + 1) * L, :] for g in range(A_GROUP)], axis=0)
        km = jnp.concatenate(lane_masked(k_all[rows]), axis=0)
        s_all = _dot_nt(km, qst)
        v_t = _dot_nt(eye_ref[...], vb_all[rows]).astype(BF16)
        var = jnp.minimum((c * CS + j) * L, WINDOW - cache_len) // L
        o_t = []
        if side_work is not None:
            side_work(j, CS)
        for kv in range(A_KV_HEADS):
            s = s_all[kv * lk:(kv + 1) * lk, :] + bias_ref[var, kv]
            sk = sink_ref[kv]
            mx = jnp.maximum(jnp.max(s, axis=0, keepdims=True), sk)
            p = jnp.exp(s - mx)
            den = jnp.sum(p, axis=0, keepdims=True) + jnp.exp(sk - mx)
            o = _dot(v_t[kv * A_HEAD_DIM:(kv + 1) * A_HEAD_DIM, :], p.astype(BF16))
            o_t.append(o * (1.0 / den))
        o_all = jnp.concatenate(o_t, axis=0).T
        ha = jnp.concatenate([o_all[g * L:(g + 1) * L, :] for g in range(A_GROUP)], axis=1)
        write_ha(j, _rms(ha, gao_ref[...]).astype(BF16))

    k_win = k_all[T:T + WINDOW]
    v_win = v_all[T:T + WINDOW]
    khist[...] = k_win
    vhist[...] = v_win

    @pl.when(is_last)
    def _():
        cko_ref[0] = k_win
        cvo_ref[0] = v_win


def _swa_kernel(*refs, L, CS, cache_len):
    ins, (ha_ref, cko_ref, cvo_ref, khist, vhist) = refs[:12], refs[12:]

    def write_ha(j, rows):
        ha_ref[0, j * L:(j + 1) * L, :] = rows

    _swa_body(pl.program_id(1), pl.program_id(1) == pl.num_programs(1) - 1, *ins, cko_ref, cvo_ref, khist, vhist,
              write_ha, L=L, CS=CS, cache_len=cache_len)


def _swa_out_kernel(*refs, L, CS, nc, n):
    ins, (x_ref, hm_ref, wm_ref, wa_ref), (o_ref, cko_ref, cvo_ref, khist, vhist, ha_prev) = (
        refs[:12], refs[12:16], refs[16:])
    s = pl.program_id(0)
    c = jnp.minimum(s, n - 1) % nc

    @pl.when(s == 0)
    def _():
        ha_prev[...] = jnp.zeros(ha_prev.shape, ha_prev.dtype)

    hm = hm_ref[...]
    ha_old = ha_prev[...]

    def project(j, parts):
        w = o_ref.shape[1] // parts
        cols = slice(j * w, (j + 1) * w)
        o_ref[:, cols] = x_ref[:, cols] + _dot(hm, wm_ref[:, cols]) + _dot(ha_old, wa_ref[:, cols])

    def write_ha(j, rows):
        ha_prev[j * L:(j + 1) * L, :] = rows

    _swa_body(c, jnp.logical_and(s < n, c == nc - 1), *ins, cko_ref, cvo_ref, khist, vhist,
              write_ha, L=L, CS=CS, cache_len=0, side_work=project)


def _swa_out(proj, x, hm, w_out, g_q, g_k, g_ao, bias, sink, consts, layer, L, CS):
    b, s, _ = proj.shape
    m, d = x.shape
    T = CS * L
    nc = s // T
    n = b * nc
    lk = WINDOW + L
    gl = A_GROUP * L
    nvar = bias.shape[0]
    kv_blk = (2 * QK_W + 2 * M_WIDTH + A_WIDTH) // A_KV_WIDTH
    q_blk = (2 * QK_W + 2 * M_WIDTH) // A_WIDTH
    e4, eye = consts
    zk = jnp.zeros((b, WINDOW, A_KV_WIDTH), F32)
    cur = lambda s_: jnp.minimum(s_, n - 1)
    prev = lambda s_: jnp.maximum(s_ - 1, 0)
    full = lambda *shape: pl.BlockSpec(shape, lambda s_: (0,) * len(shape))
    out_shape = (jax.ShapeDtypeStruct((m, d), F32),
                 jax.ShapeDtypeStruct((b, WINDOW, A_KV_WIDTH), F32),
                 jax.ShapeDtypeStruct((b, WINDOW, A_KV_WIDTH), F32))
    return pl.pallas_call(
        functools.partial(_swa_out_kernel, L=L, CS=CS, nc=nc, n=n),
        out_shape=out_shape,
        grid=(n + 1,),
        in_specs=[pl.BlockSpec((1, T, A_WIDTH), lambda s_: (cur(s_) // nc, cur(s_) % nc, q_blk)),
                  pl.BlockSpec((1, T, A_KV_WIDTH), lambda s_: (cur(s_) // nc, cur(s_) % nc, kv_blk)),
                  pl.BlockSpec((1, T, A_KV_WIDTH), lambda s_: (cur(s_) // nc, cur(s_) % nc, kv_blk + 1)),
                  pl.BlockSpec((1, WINDOW, A_KV_WIDTH), lambda s_: (cur(s_) // nc, 0, 0)),
                  pl.BlockSpec((1, WINDOW, A_KV_WIDTH), lambda s_: (cur(s_) // nc, 0, 0)),
                  pl.BlockSpec((None, 1, A_WIDTH), lambda s_: (layer, 0, 0)),
                  pl.BlockSpec((None, 1, A_KV_WIDTH), lambda s_: (layer, 0, 0)),
                  pl.BlockSpec((None, 1, A_WIDTH), lambda s_: (layer, 0, 0)),
                  full(nvar, A_KV_HEADS, lk, gl),
                  pl.BlockSpec((None, A_KV_HEADS, 1, gl), lambda s_: (layer, 0, 0, 0)),
                  full(A_KV_WIDTH, A_KV_WIDTH),
                  full(A_KV_WIDTH, A_KV_WIDTH),
                  pl.BlockSpec((T, d), lambda s_: (prev(s_), 0)),
                  pl.BlockSpec((T, M_WIDTH), lambda s_: (prev(s_), 0)),
                  pl.BlockSpec((None, M_WIDTH, d), lambda s_: (layer, 0, 0), pipeline_mode=pl.Buffered(1)),
                  pl.BlockSpec((None, A_WIDTH, d), lambda s_: (layer, 1, 0), pipeline_mode=pl.Buffered(1))],
        out_specs=(pl.BlockSpec((T, d), lambda s_: (prev(s_), 0)),
                   pl.BlockSpec((1, WINDOW, A_KV_WIDTH), lambda s_: (cur(s_) // nc, 0, 0)),
                   pl.BlockSpec((1, WINDOW, A_KV_WIDTH), lambda s_: (cur(s_) // nc, 0, 0))),
        scratch_shapes=[pltpu.VMEM((WINDOW, A_KV_WIDTH), F32),
                        pltpu.VMEM((WINDOW, A_KV_WIDTH), F32),
                        pltpu.VMEM((T, A_WIDTH), BF16)],
        compiler_params=_params(("arbitrary",)),
        name="swa_out",
    )(proj, proj, proj, zk, zk, g_q, g_k, g_ao, bias, sink, e4, eye, x, hm, w_out, w_out)


def _swa(proj, ck, cv, g_q, g_k, g_ao, bias, sink, consts, layer, L, CS, cache_len):
    b, s, _ = proj.shape
    T = CS * L
    lk = WINDOW + L
    gl = A_GROUP * L
    nvar = bias.shape[0]
    kv_blk = (2 * QK_W + 2 * M_WIDTH + A_WIDTH) // A_KV_WIDTH
    q_blk = (2 * QK_W + 2 * M_WIDTH) // A_WIDTH
    e4, eye = consts
    out_shape = (jax.ShapeDtypeStruct((b, s, A_WIDTH), BF16),
                 jax.ShapeDtypeStruct((b, WINDOW, A_KV_WIDTH), F32),
                 jax.ShapeDtypeStruct((b, WINDOW, A_KV_WIDTH), F32))
    full = lambda *shape: pl.BlockSpec(shape, lambda i, c: (0,) * len(shape))
    return pl.pallas_call(
        functools.partial(_swa_kernel, L=L, CS=CS, cache_len=cache_len),
        out_shape=out_shape,
        grid=(b, s // T),
        in_specs=[pl.BlockSpec((1, T, A_WIDTH), lambda i, c: (i, c, q_blk)),
                  pl.BlockSpec((1, T, A_KV_WIDTH), lambda i, c: (i, c, kv_blk)),
                  pl.BlockSpec((1, T, A_KV_WIDTH), lambda i, c: (i, c, kv_blk + 1)),
                  pl.BlockSpec((1, WINDOW, A_KV_WIDTH), lambda i, c: (i, 0, 0)),
                  pl.BlockSpec((1, WINDOW, A_KV_WIDTH), lambda i, c: (i, 0, 0)),
                  pl.BlockSpec((None, 1, A_WIDTH), lambda i, c: (layer, 0, 0)),
                  pl.BlockSpec((None, 1, A_KV_WIDTH), lambda i, c: (layer, 0, 0)),
                  pl.BlockSpec((None, 1, A_WIDTH), lambda i, c: (layer, 0, 0)),
                  full(nvar, A_KV_HEADS, lk, gl),
                  pl.BlockSpec((None, A_KV_HEADS, 1, gl), lambda i, c: (layer, 0, 0, 0)),
                  full(A_KV_WIDTH, A_KV_WIDTH),
                  full(A_KV_WIDTH, A_KV_WIDTH)],
        out_specs=(pl.BlockSpec((1, T, A_WIDTH), lambda i, c: (i, c, 0)),
                   pl.BlockSpec((1, WINDOW, A_KV_WIDTH), lambda i, c: (i, 0, 0)),
                   pl.BlockSpec((1, WINDOW, A_KV_WIDTH), lambda i, c: (i, 0, 0))),
        scratch_shapes=[pltpu.VMEM((WINDOW, A_KV_WIDTH), F32),
                        pltpu.VMEM((WINDOW, A_KV_WIDTH), F32)],
        compiler_params=_params(("arbitrary", "arbitrary")),
        name="swa",
    )(proj, proj, proj, ck, cv, g_q, g_k, g_ao, bias, sink, e4, eye)


def _out_proj_kernel(x_ref, hm_ref, ha_ref, wm_ref, wa_ref, o_ref):
    o_ref[...] = x_ref[...] + _dot(hm_ref[...], wm_ref[...]) + _dot(ha_ref[...], wa_ref[...])


def _out_proj(x, hm, ha, w_out, layer):
    m, d = x.shape
    tm = _row_tile(m, 512)
    return pl.pallas_call(
        _out_proj_kernel,
        out_shape=jax.ShapeDtypeStruct((m, d), F32),
        grid=(m // tm,),
        in_specs=[pl.BlockSpec((tm, d), lambda i: (i, 0)),
                  pl.BlockSpec((tm, M_WIDTH), lambda i: (i, 0)),
                  pl.BlockSpec((tm, A_WIDTH), lambda i: (i, 0)),
                  pl.BlockSpec((None, M_WIDTH, d), lambda i: (layer, 0, 0)),
                  pl.BlockSpec((None, A_WIDTH, d), lambda i: (layer, 1, 0))],
        out_specs=pl.BlockSpec((tm, d), lambda i: (i, 0)),
        compiler_params=_params(("parallel",)),
        name="out_proj",
    )(x, hm, ha, w_out, w_out)


def _ffn_kernel(x_ref, g_ref, wu_ref, wd_ref, o_ref, h_scr):
    f = pl.program_id(1)

    @pl.when(f == 0)
    def _():
        x = x_ref[...]
        h_scr[...] = _rms(x, g_ref[...]).astype(BF16)
        o_ref[...] = x

    u = _dot(h_scr[...], wu_ref[...])
    a = jnp.square(jnp.maximum(u, 0.0)).astype(BF16)
    o_ref[...] += _dot(a, wd_ref[...])


def _ffn(x, g_ffn, w_up, w_down, layer):
    m, d = x.shape
    d_ff = w_up.shape[-1]
    tm = _row_tile(m, 512)
    tf = 1024
    return pl.pallas_call(
        _ffn_kernel,
        out_shape=jax.ShapeDtypeStruct((m, d), F32),
        grid=(m // tm, d_ff // tf),
        in_specs=[pl.BlockSpec((tm, d), lambda i, f: (i, 0)),
                  pl.BlockSpec((None, 1, d), lambda i, f: (layer, 0, 0)),
                  pl.BlockSpec((None, d, tf), lambda i, f: (layer, 0, f)),
                  pl.BlockSpec((None, tf, d), lambda i, f: (layer, f, 0))],
        out_specs=pl.BlockSpec((tm, d), lambda i, f: (i, 0)),
        scratch_shapes=[pltpu.VMEM((tm, d), BF16)],
        compiler_params=_params(("parallel", "arbitrary")),
        name="ffn",
    )(x, g_ffn, w_up, w_down)


def _layer(x, wts, layer, lm, la, cs, bias, sink, ck, cv, c0, n0, m0, cache_len):
    b, s, d = x.shape
    x2 = x.reshape(b * s, d)
    proj, gates = _in_proj(x2, wts["g_mix"], wts["w_all"], layer)
    proj = proj.reshape(b, s, PROJ_W)
    gates = gates.reshape(b, s, GATE_W)
    hm, c_new, n_new, m_new = _mlstm(proj, gates, wts["gbias"], wts["g_mo"], wts["consts"][1], c0, n0, m0, layer, lm)
    hm = hm.reshape(b * s, M_WIDTH)
    if ck is None:
        x2, k_new, v_new = _swa_out(proj, x2, hm, wts["w_out"], wts["g_q"], wts["g_k"], wts["g_ao"], bias, sink,
                                    wts["consts"], layer, la, cs)
    else:
        ha, k_new, v_new = _swa(proj, ck, cv, wts["g_q"], wts["g_k"], wts["g_ao"], bias, sink, wts["consts"],
                                layer, la, cs, cache_len)
        x2 = _out_proj(x2, hm, ha.reshape(b * s, A_WIDTH), wts["w_out"], layer)
    x2 = _ffn(x2, wts["g_ffn"], wts["w_up"], wts["w_down"], layer)
    return x2.reshape(b, s, d), k_new, v_new, c_new, n_new, m_new


def _head_consts():
    hd = np.arange(A_KV_WIDTH) // A_HEAD_DIM
    e4 = (hd[:, None] == hd[None, :]).astype(np.float32)
    eye = np.eye(A_KV_WIDTH, dtype=np.float32)
    return tuple(jnp.asarray(a, BF16) for a in (e4, eye))


def kernel(x_prompt, x_sample, cache_k, cache_v, state_C, state_n, state_m, rel_bias, g_mix, w_in, b_i, b_f, g_q, g_k, sinks, g_mo, g_ao, w_out, g_ffn, w_up, w_down):
    depth = w_in.shape[0]
    bp, sp, d = x_prompt.shape
    bs, ls, _ = x_sample.shape
    n_win = cache_k.shape[2]
    assert sp % CHUNK == 0 and n_win == WINDOW and ls % 16 == 0 and ls <= CHUNK
    lm = M_CHUNK if sp % M_CHUNK == 0 else CHUNK

    o = np.cumsum((0, QK_W, QK_W, M_WIDTH, M_WIDTH, M_HEADS, M_HEADS, A_WIDTH, A_KV_WIDTH, A_KV_WIDTH))
    seg = [w_in[:, :, o[i]:o[i + 1]] for i in range(9)]
    aq = jnp.swapaxes(seg[6].reshape(depth, d, A_KV_HEADS, A_GROUP, A_HEAD_DIM), 2, 3).reshape(depth, d, A_WIDTH)
    w_all = jnp.concatenate([seg[0], seg[1], seg[2], seg[3], aq, seg[7], seg[8], seg[4], seg[5],
                             jnp.zeros((depth, d, GATE_W - 2 * M_HEADS), F32)], axis=-1).astype(BF16)
    gbias = jnp.concatenate([b_i, b_f, jnp.zeros((depth, GATE_W - 2 * M_HEADS), F32)], axis=-1)

    def regroup(a):
        rest = a.shape[2:]
        a = a.reshape(depth, A_KV_HEADS, A_GROUP, A_HEAD_DIM, *rest)
        return jnp.swapaxes(a, 1, 2).reshape(depth, A_WIDTH, *rest)

    w_out_b = jnp.concatenate([w_out[:, :M_WIDTH], regroup(w_out[:, M_WIDTH:])], axis=1).astype(BF16)
    wts = {
        "g_mix": g_mix.reshape(depth, 1, d), "w_all": w_all,
        "gbias": gbias.reshape(depth, 1, GATE_W), "g_mo": g_mo.reshape(depth, 1, M_WIDTH),
        "g_q": jnp.tile(g_q * A_SCALE, (1, A_HEADS)).reshape(depth, 1, A_WIDTH),
        "g_k": jnp.tile(g_k, (1, A_KV_HEADS)).reshape(depth, 1, A_KV_WIDTH),
        "g_ao": regroup(g_ao).reshape(depth, 1, A_WIDTH),
        "w_out": w_out_b, "g_ffn": g_ffn.reshape(depth, 1, d),
        "w_up": w_up.astype(BF16), "w_down": w_down.astype(BF16),
        "consts": _head_consts(),
    }

    bias = _bias_table(rel_bias)

    def bias_t(lq, cache_len):
        lk = WINDOW + lq
        t = bias[:, :lq, :lk].reshape(A_KV_HEADS, A_GROUP, lq, lk)
        t = t.transpose(0, 3, 1, 2).reshape(A_KV_HEADS, lk, A_GROUP * lq)
        n_prev = np.arange((WINDOW - cache_len) // lq + 1) * lq + cache_len
        dead = np.arange(lk)[None, :] < WINDOW - n_prev[:, None]
        return t[None] + jnp.asarray(np.where(dead, NEG_INF, 0.0)[:, None, :, None], F32)

    def sink_rows(lq):
        return jnp.repeat(sinks.reshape(depth, A_KV_HEADS, 1, A_GROUP), lq, axis=-1)

    zc =jnp.zeros((bp, M_HEADS, M_DV, M_DK), F32)
    zn = jnp.zeros((bp, M_HEADS, M_DK), F32)
    zm = jnp.zeros((bp, 1, M_HEADS), F32)

    xp, xs = x_prompt, x_sample
    bias_p, bias_s, sink_p, sink_s = bias_t(CHUNK, 0), bias_t(ls, WINDOW), sink_rows(CHUNK), sink_rows(ls)
    cs = A_CHUNKS_PER_STEP if sp % (A_CHUNKS_PER_STEP * CHUNK) == 0 else 1
    outs_p, outs_s = [], []
    for l in range(depth):
        xp, *st = _layer(xp, wts, l, lm, CHUNK, cs, bias_p, sink_p, None, None, zc, zn, zm, 0)
        outs_p.append(st)
        xs, *st = _layer(xs, wts, l, ls, ls, 1, bias_s, sink_s,
                         cache_k[l].reshape(bs, n_win, A_KV_WIDTH), cache_v[l].reshape(bs, n_win, A_KV_WIDTH),
                         state_C[l], state_n[l], state_m[l].reshape(bs, 1, M_HEADS), WINDOW)
        outs_s.append(st)

    def stack(outs, b):
        k, v, c, n, m = (jnp.stack([o_[i] for o_ in outs]) for i in range(5))
        return (k.reshape(depth, b, WINDOW, A_KV_HEADS, A_HEAD_DIM), v.reshape(depth, b, WINDOW, A_KV_HEADS, A_HEAD_DIM),
                c, n, m.reshape(depth, b, M_HEADS))

    return (xp, xs) + stack(outs_p, bp) + stack(outs_s, bs)
```

```python
import functools

import jax
import jax.numpy as jnp
import numpy as np
from jax import lax
from jax.experimental import pallas as pl
from jax.experimental.pallas import tpu as pltpu

F32 = jnp.float32
BF16 = jnp.bfloat16

CHUNK = 64
M_CHUNK = 256
A_CHUNKS_PER_STEP = 4
M_HEADS = 4
M_DK = 128
M_DV = 256
M_WIDTH = M_HEADS * M_DV
A_HEADS = 16
A_KV_HEADS = 4
A_HEAD_DIM = 64
A_GROUP = A_HEADS // A_KV_HEADS
A_WIDTH = A_HEADS * A_HEAD_DIM
A_KV_WIDTH = A_KV_HEADS * A_HEAD_DIM
WINDOW = 128
N_BUCKETS = 32
MAX_DISTANCE = 128
EPS = 1e-6
NEG_INF = -1e30
M_SCALE = M_DK ** -0.5
A_SCALE = A_HEAD_DIM ** -0.5

HEAD_W = 2 * M_DK + 2 * M_DV
M_PROJ_W = M_HEADS * HEAD_W
A_PROJ_W = A_WIDTH + 2 * A_KV_WIDTH
PROJ_W = M_PROJ_W + A_PROJ_W
GATE_W = 128

V7X_VMEM_LIMIT = 56 * 1024 * 1024

NT_DIMS = (((1,), (1,)), ((), ()))


def _params(sem, vmem=V7X_VMEM_LIMIT):
    return pltpu.CompilerParams(dimension_semantics=sem, vmem_limit_bytes=vmem)


def _row_tile(m, cap):
    t = min(m, cap)
    while m % t:
        t //= 2
    return t


def _dot(a, b):
    return jnp.dot(a, b, preferred_element_type=F32)


def _dot_nt(a, b):
    return lax.dot_general(a, b, NT_DIMS, preferred_element_type=F32)


def _t5_bucket(rel):
    half = N_BUCKETS // 2
    exact = half // 2
    n = np.abs(rel)
    large = exact + (np.log(np.maximum(n, 1) / exact) / np.log(MAX_DISTANCE / exact) * (half - exact)).astype(np.int32)
    large = np.minimum(large, half - 1)
    return (rel > 0).astype(np.int32) * half + np.where(n < exact, n, large).astype(np.int32)


def _bias_kernel(rel_ref, map_ref, out_ref):
    bmap = map_ref[...]
    for h in range(A_HEADS):
        acc = jnp.zeros(bmap.shape, F32)
        for b in range(N_BUCKETS):
            acc = jnp.where(bmap == b, rel_ref[b, h], acc)
        out_ref[h] = acc


def _bias_table(rel_bias):
    lk = WINDOW + CHUNK
    rel = (np.arange(lk)[None, :] - WINDOW) - np.arange(CHUNK)[:, None]
    bmap = jnp.asarray(_t5_bucket(rel), jnp.int32)
    return pl.pallas_call(
        _bias_kernel,
        out_shape=jax.ShapeDtypeStruct((A_HEADS, CHUNK, lk), F32),
        in_specs=[pl.BlockSpec(memory_space=pltpu.SMEM),
                  pl.BlockSpec(memory_space=pltpu.VMEM)],
        out_specs=pl.BlockSpec(memory_space=pltpu.VMEM),
        name="t5_bias_table",
    )(rel_bias, bmap)


def _rms(x, g):
    return x * lax.rsqrt(jnp.mean(x * x, axis=-1, keepdims=True) + EPS) * g


def _in_proj_kernel(x_ref, g_ref, w_ref, o_ref, gate_ref, *, sub):
    for r in range(0, x_ref.shape[0], sub):
        h = _rms(x_ref[r:r + sub, :], g_ref[...]).astype(BF16)
        acc = _dot(h, w_ref[...])
        o_ref[r:r + sub, :] = acc[:, :PROJ_W].astype(o_ref.dtype)
        gate_ref[r:r + sub, :] = acc[:, PROJ_W:]


def _in_proj(x, g_mix, w_all, layer):
    m, d = x.shape
    tm = _row_tile(m, 512)
    n = PROJ_W + GATE_W
    return pl.pallas_call(
        functools.partial(_in_proj_kernel, sub=min(tm, 256)),
        out_shape=(jax.ShapeDtypeStruct((m, PROJ_W), BF16), jax.ShapeDtypeStruct((m, GATE_W), F32)),
        grid=(m // tm,),
        in_specs=[pl.BlockSpec((tm, d), lambda i: (i, 0)),
                  pl.BlockSpec((None, 1, d), lambda i: (layer, 0, 0)),
                  pl.BlockSpec((None, d, n), lambda i: (layer, 0, 0), pipeline_mode=pl.Buffered(1))],
        out_specs=(pl.BlockSpec((tm, PROJ_W), lambda i: (i, 0)),
                   pl.BlockSpec((tm, GATE_W), lambda i: (i, 0))),
        compiler_params=_params(("parallel",)),
        name="in_proj",
    )(x, g_mix, w_all)


def _mlstm_body(head_inputs, gates, gbias_ref, gmo_ref, eye_ref, c0_ref, n0_ref, m0_ref,
                hm_ref, cout_ref, nout_ref, mout_ref, c_scr, n_scr, m_scr, *, L):
    c = pl.program_id(1)
    last = pl.num_programs(1) - 1

    @pl.when(c == 0)
    def _():
        c_scr[...] = c0_ref[0]
        n_scr[...] = n0_ref[0]
        for h in range(M_HEADS):
            m_scr[h:h + 1, :] = jnp.broadcast_to(m0_ref[0, :, h:h + 1], (1, 128))

    a_all = gates + gbias_ref[...]
    logf = jnp.minimum(a_all, 0.0) - jnp.log(1.0 + jnp.exp(-jnp.abs(a_all)))
    row = lax.broadcasted_iota(jnp.int32, (L, GATE_W), 0)
    b_all = logf
    k = 1
    while k < L:
        b_all = b_all + jnp.where(row >= k, pltpu.roll(b_all, k, axis=0), 0.0)
        k *= 2
    b_sh = pltpu.roll(b_all, GATE_W - M_HEADS, axis=1)
    a_i = a_all - b_sh
    cmax = a_i
    k = 1
    while k < L:
        cmax = jnp.maximum(cmax, jnp.where(row >= k, pltpu.roll(cmax, k, axis=0), NEG_INF))
        k *= 2
    a_t = a_i.T
    tri = lax.broadcasted_iota(jnp.int32, (L, L), 0) >= lax.broadcasted_iota(jnp.int32, (L, L), 1)
    eye = eye_ref[...]

    for h in range(M_HEADS):
        b_col = b_sh[:, h:h + 1]
        a_col = a_i[:, h:h + 1]
        a_row = a_t[h:h + 1, :]
        m_prev = m_scr[h:h + 1, 0:1]
        m_run = jnp.maximum(cmax[:, h:h + 1], m_prev)
        decay_w = jnp.exp(jnp.where(tri, a_row - m_run, NEG_INF))
        qh, kh, vh, og = head_inputs(h)
        w = decay_w * (_dot_nt(qh, kh) * M_SCALE)
        g = jnp.exp(m_prev - m_run)
        cst = c_scr[h]
        n_row = n_scr[h:h + 1, :]
        num = _dot(w.astype(BF16), vh) + g * _dot_nt(qh, cst.astype(BF16))
        den = (jnp.sum(w, axis=1, keepdims=True)
               + g * jnp.sum(qh.astype(F32) * n_row, axis=1, keepdims=True))
        inv = 1.0 / jnp.maximum(jnp.abs(den), jnp.exp(-(b_col + m_run)))
        ms = jnp.mean(num * num, axis=1, keepdims=True)
        scale = inv * lax.rsqrt(inv * inv * ms + EPS)
        hm_ref[0, :, h * M_DV:(h + 1) * M_DV] = (
            num * scale * gmo_ref[:, h * M_DV:(h + 1) * M_DV] * jax.nn.sigmoid(og)).astype(hm_ref.dtype)

        b_last = b_col[L - 1:L, :]
        m_end = b_last + m_run[L - 1:L, :]
        wk = jnp.exp(b_last + a_col - m_end)
        decay = jnp.exp(b_last + m_prev - m_end)
        vw = (vh.astype(F32) * wk).astype(BF16)
        vw_t = _dot_nt(eye, vw).astype(BF16)
        c_new = decay * cst + M_SCALE * _dot(vw_t, kh)
        n_new = decay * n_row + M_SCALE * jnp.sum(kh.astype(F32) * wk, axis=0, keepdims=True)
        c_scr[h] = c_new
        n_scr[h:h + 1, :] = n_new
        m_scr[h:h + 1, :] = jnp.broadcast_to(m_end, (1, 128))

        @pl.when(c == last)
        def _():
            cout_ref[0, h] = c_new
            nout_ref[0, h:h + 1, :] = n_new
            mout_ref[0, :, h:h + 1] = m_end


def _mlstm_kernel(p_ref, gate_ref, *rest, L):
    def head_inputs(h):
        o = h * HEAD_W
        return (p_ref[0, :, o:o + M_DK], p_ref[0, :, o + M_DK:o + 2 * M_DK],
                p_ref[0, :, o + 2 * M_DK:o + 2 * M_DK + M_DV], p_ref[0, :, o + 2 * M_DK + M_DV:o + HEAD_W].astype(F32))

    _mlstm_body(head_inputs, gate_ref[0], *rest, L=L)


def _proj_mlstm_kernel(x_ref, g_ref, w_ref, gbias_ref, gmo_ref, eye_ref, c0_ref, n0_ref, m0_ref,
                       hm_ref, ap_ref, cout_ref, nout_ref, mout_ref, c_scr, n_scr, m_scr, *, L):
    hn = _rms(x_ref[0], g_ref[...]).astype(BF16)

    def head_inputs(h):
        o = h * HEAD_W
        ph = _dot(hn, w_ref[:, o:o + HEAD_W])
        return (ph[:, :M_DK].astype(BF16), ph[:, M_DK:2 * M_DK].astype(BF16),
                ph[:, 2 * M_DK:2 * M_DK + M_DV].astype(BF16), ph[:, 2 * M_DK + M_DV:])

    _mlstm_body(head_inputs, _dot(hn, w_ref[:, PROJ_W:]), gbias_ref, gmo_ref, eye_ref, c0_ref, n0_ref, m0_ref,
                hm_ref, cout_ref, nout_ref, mout_ref, c_scr, n_scr, m_scr, L=L)
    ap_ref[0] = _dot(hn, w_ref[:, M_PROJ_W:PROJ_W]).astype(ap_ref.dtype)


def _mlstm_specs(b, layer, L):
    state = [pl.BlockSpec((1, M_HEADS, M_DV, M_DK), lambda i, c: (i, 0, 0, 0)),
             pl.BlockSpec((1, M_HEADS, M_DK), lambda i, c: (i, 0, 0)),
             pl.BlockSpec((1, 1, M_HEADS), lambda i, c: (i, 0, 0))]
    params = [pl.BlockSpec((None, 1, GATE_W), lambda i, c: (layer, 0, 0)),
              pl.BlockSpec((None, 1, M_WIDTH), lambda i, c: (layer, 0, 0)),
              pl.BlockSpec((M_DV, M_DV), lambda i, c: (0, 0))]
    state_shapes = [jax.ShapeDtypeStruct((b, M_HEADS, M_DV, M_DK), F32),
                    jax.ShapeDtypeStruct((b, M_HEADS, M_DK), F32),
                    jax.ShapeDtypeStruct((b, 1, M_HEADS), F32)]
    scratch = [pltpu.VMEM((M_HEADS, M_DV, M_DK), F32), pltpu.VMEM((M_HEADS, M_DK), F32), pltpu.VMEM((M_HEADS, 128), F32)]
    return params + state, state, state_shapes, scratch


def _mlstm(proj, gates, gbias, g_mo, eye, c0, n0, m0, layer, L):
    b, s, _ = proj.shape
    ins, state_out, state_shapes, scratch = _mlstm_specs(b, layer, L)
    return pl.pallas_call(
        functools.partial(_mlstm_kernel, L=L),
        out_shape=[jax.ShapeDtypeStruct((b, s, M_WIDTH), BF16)] + state_shapes,
        grid=(b, s // L),
        in_specs=[pl.BlockSpec((1, L, M_PROJ_W), lambda i, c: (i, c, 0)),
                  pl.BlockSpec((1, L, GATE_W), lambda i, c: (i, c, 0))] + ins,
        out_specs=[pl.BlockSpec((1, L, M_WIDTH), lambda i, c: (i, c, 0))] + state_out,
        scratch_shapes=scratch,
        compiler_params=_params(("arbitrary", "arbitrary")),
        name="mlstm",
    )(proj, gates, gbias, g_mo, eye, c0, n0, m0)


def _proj_mlstm(x, g_mix, w_all, gbias, g_mo, eye, c0, n0, m0, layer, L):
    b, s, d = x.shape
    ins, state_out, state_shapes, scratch = _mlstm_specs(b, layer, L)
    return pl.pallas_call(
        functools.partial(_proj_mlstm_kernel, L=L),
        out_shape=[jax.ShapeDtypeStruct((b, s, M_WIDTH), BF16), jax.ShapeDtypeStruct((b, s, A_PROJ_W), BF16)] + state_shapes,
        grid=(b, s // L),
        in_specs=[pl.BlockSpec((1, L, d), lambda i, c: (i, c, 0)),
                  pl.BlockSpec((None, 1, d), lambda i, c: (layer, 0, 0)),
                  pl.BlockSpec((None, d, PROJ_W + GATE_W), lambda i, c: (layer, 0, 0), pipeline_mode=pl.Buffered(1))] + ins,
        out_specs=[pl.BlockSpec((1, L, M_WIDTH), lambda i, c: (i, c, 0)),
                   pl.BlockSpec((1, L, A_PROJ_W), lambda i, c: (i, c, 0))] + state_out,
        scratch_shapes=scratch,
        compiler_params=_params(("arbitrary", "arbitrary")),
        name="proj_mlstm",
    )(x, g_mix, w_all, gbias, g_mo, eye, c0, n0, m0)


def _swa_body(c, is_last, q_ref, k_ref, v_ref, ck_ref, cv_ref, gq_ref, gk_ref, gao_ref, bias_ref, sink_ref,
              e4_ref, eye_ref, cko_ref, cvo_ref, khist, vhist, write_ha, *, L, CS, cache_len, side_work=None):
    T = CS * L
    lk = WINDOW + L
    gl = A_GROUP * L
    kv_lane = lax.broadcasted_iota(jnp.int32, (1, A_KV_WIDTH), 1) // A_HEAD_DIM

    def lane_masked(kn):
        knb = kn.astype(BF16)
        return [jnp.where(kv_lane == kv, knb, 0.0) for kv in range(A_KV_HEADS)]

    @pl.when(c == 0)
    def _():
        khist[...] = ck_ref[0]
        vhist[...] = cv_ref[0]

    q = q_ref[0].astype(F32)
    k = k_ref[0].astype(F32)
    sq = jnp.concatenate([q[:, g * A_KV_WIDTH:(g + 1) * A_KV_WIDTH] for g in range(A_GROUP)] + [k], axis=0)
    sq = sq * sq
    sq_hi = sq.astype(BF16)
    sq_lo = (sq - sq_hi.astype(F32)).astype(BF16)
    ss = _dot(jnp.concatenate([sq_hi, sq_lo], axis=0), e4_ref[...])
    rinv = lax.rsqrt((ss[:5 * T] + ss[5 * T:]) * (1.0 / A_HEAD_DIM) + EPS)
    qn = [(q[:, g * A_KV_WIDTH:(g + 1) * A_KV_WIDTH] * rinv[g * T:(g + 1) * T, :]
           * gq_ref[:, g * A_KV_WIDTH:(g + 1) * A_KV_WIDTH]).astype(BF16) for g in range(A_GROUP)]
    kn = k * rinv[A_GROUP * T:, :] * gk_ref[...]

    k_all = jnp.concatenate([khist[...], kn], axis=0)
    v_all = jnp.concatenate([vhist[...], v_ref[0].astype(F32)], axis=0)
    vb_all = v_all.astype(BF16)

    for j in range(CS):
        rows = slice(j * L, j * L + lk)
        qst = jnp.concatenate([qn[g][j * L:(j + 1) * L, :] for g in range(A_GROUP)], axis=0)
        km = jnp.concatenate(lane_masked(k_all[rows]), axis=0)
        s_all = _dot_nt(km, qst)
        v_t = _dot_nt(eye_ref[...], vb_all[rows]).astype(BF16)
        var = jnp.minimum((c * CS + j) * L, WINDOW - cache_len) // L
        o_t = []
        if side_work is not None:
            side_work(j, CS)
        for kv in range(A_KV_HEADS):
            s = s_all[kv * lk:(kv + 1) * lk, :] + bias_ref[var, kv]
            sk = sink_ref[kv]
            mx = jnp.maximum(jnp.max(s, axis=0, keepdims=True), sk)
            p = jnp.exp(s - mx)
            den = jnp.sum(p, axis=0, keepdims=True) + jnp.exp(sk - mx)
            o = _dot(v_t[kv * A_HEAD_DIM:(kv + 1) * A_HEAD_DIM, :], p.astype(BF16))
            o_t.append(o * (1.0 / den))
        o_all = jnp.concatenate(o_t, axis=0).T
        ha = jnp.concatenate([o_all[g * L:(g + 1) * L, :] for g in range(A_GROUP)], axis=1)
        write_ha(j, _rms(ha, gao_ref[...]).astype(BF16))

    k_win = k_all[T:T + WINDOW]
    v_win = v_all[T:T + WINDOW]
    khist[...] = k_win
    vhist[...] = v_win

    @pl.when(is_last)
    def _():
        cko_ref[0] = k_win
        cvo_ref[0] = v_win


def _swa_kernel(*refs, L, CS, cache_len):
    ins, (ha_ref, cko_ref, cvo_ref, khist, vhist) = refs[:12], refs[12:]

    def write_ha(j, rows):
        ha_ref[0, j * L:(j + 1) * L, :] = rows

    _swa_body(pl.program_id(1), pl.program_id(1) == pl.num_programs(1) - 1, *ins, cko_ref, cvo_ref, khist, vhist,
              write_ha, L=L, CS=CS, cache_len=cache_len)


def _swa_out_kernel(*refs, L, CS, nc, n):
    ins, (x_ref, hm_ref, wm_ref, wa_ref), (o_ref, cko_ref, cvo_ref, khist, vhist, ha_prev) = (
        refs[:12], refs[12:16], refs[16:])
    s = pl.program_id(0)
    c = jnp.minimum(s, n - 1) % nc

    @pl.when(s == 0)
    def _():
        ha_prev[...] = jnp.zeros(ha_prev.shape, ha_prev.dtype)

    hm = hm_ref[...]
    ha_old = ha_prev[...]

    def project(j, parts):
        w = o_ref.shape[1] // parts
        cols = slice(j * w, (j + 1) * w)
        o_ref[:, cols] = x_ref[:, cols] + _dot(hm, wm_ref[:, cols]) + _dot(ha_old, wa_ref[:, cols])

    def write_ha(j, rows):
        ha_prev[j * L:(j + 1) * L, :] = rows

    _swa_body(c, jnp.logical_and(s < n, c == nc - 1), *ins, cko_ref, cvo_ref, khist, vhist,
              write_ha, L=L, CS=CS, cache_len=0, side_work=project)


def _swa_out(proj, col0, x, hm, w_out, g_q, g_k, g_ao, bias, sink, consts, layer, L, CS):
    b, s, _ = proj.shape
    m, d = x.shape
    T = CS * L
    nc = s // T
    n = b * nc
    lk = WINDOW + L
    gl = A_GROUP * L
    nvar = bias.shape[0]
    q_blk, kv_blk = col0 // A_WIDTH, (col0 + A_WIDTH) // A_KV_WIDTH
    e4, eye = consts
    zk = jnp.zeros((b, WINDOW, A_KV_WIDTH), F32)
    cur = lambda s_: jnp.minimum(s_, n - 1)
    prev = lambda s_: jnp.maximum(s_ - 1, 0)
    full = lambda *shape: pl.BlockSpec(shape, lambda s_: (0,) * len(shape))
    out_shape = (jax.ShapeDtypeStruct((m, d), F32),
                 jax.ShapeDtypeStruct((b, WINDOW, A_KV_WIDTH), F32),
                 jax.ShapeDtypeStruct((b, WINDOW, A_KV_WIDTH), F32))
    return pl.pallas_call(
        functools.partial(_swa_out_kernel, L=L, CS=CS, nc=nc, n=n),
        out_shape=out_shape,
        grid=(n + 1,),
        in_specs=[pl.BlockSpec((1, T, A_WIDTH), lambda s_: (cur(s_) // nc, cur(s_) % nc, q_blk)),
                  pl.BlockSpec((1, T, A_KV_WIDTH), lambda s_: (cur(s_) // nc, cur(s_) % nc, kv_blk)),
                  pl.BlockSpec((1, T, A_KV_WIDTH), lambda s_: (cur(s_) // nc, cur(s_) % nc, kv_blk + 1)),
                  pl.BlockSpec((1, WINDOW, A_KV_WIDTH), lambda s_: (cur(s_) // nc, 0, 0)),
                  pl.BlockSpec((1, WINDOW, A_KV_WIDTH), lambda s_: (cur(s_) // nc, 0, 0)),
                  pl.BlockSpec((None, 1, A_WIDTH), lambda s_: (layer, 0, 0)),
                  pl.BlockSpec((None, 1, A_KV_WIDTH), lambda s_: (layer, 0, 0)),
                  pl.BlockSpec((None, 1, A_WIDTH), lambda s_: (layer, 0, 0)),
                  full(nvar, A_KV_HEADS, lk, gl),
                  pl.BlockSpec((None, A_KV_HEADS, 1, gl), lambda s_: (layer, 0, 0, 0)),
                  full(A_KV_WIDTH, A_KV_WIDTH),
                  full(A_KV_WIDTH, A_KV_WIDTH),
                  pl.BlockSpec((T, d), lambda s_: (prev(s_), 0)),
                  pl.BlockSpec((T, M_WIDTH), lambda s_: (prev(s_), 0)),
                  pl.BlockSpec((None, M_WIDTH, d), lambda s_: (layer, 0, 0), pipeline_mode=pl.Buffered(1)),
                  pl.BlockSpec((None, A_WIDTH, d), lambda s_: (layer, 1, 0), pipeline_mode=pl.Buffered(1))],
        out_specs=(pl.BlockSpec((T, d), lambda s_: (prev(s_), 0)),
                   pl.BlockSpec((1, WINDOW, A_KV_WIDTH), lambda s_: (cur(s_) // nc, 0, 0)),
                   pl.BlockSpec((1, WINDOW, A_KV_WIDTH), lambda s_: (cur(s_) // nc, 0, 0))),
        scratch_shapes=[pltpu.VMEM((WINDOW, A_KV_WIDTH), F32),
                        pltpu.VMEM((WINDOW, A_KV_WIDTH), F32),
                        pltpu.VMEM((T, A_WIDTH), BF16)],
        compiler_params=_params(("arbitrary",)),
        name="swa_out",
    )(proj, proj, proj, zk, zk, g_q, g_k, g_ao, bias, sink, e4, eye, x, hm, w_out, w_out)


def _swa(proj, col0, ck, cv, g_q, g_k, g_ao, bias, sink, consts, layer, L, CS, cache_len):
    b, s, _ = proj.shape
    T = CS * L
    lk = WINDOW + L
    gl = A_GROUP * L
    nvar = bias.shape[0]
    q_blk, kv_blk = col0 // A_WIDTH, (col0 + A_WIDTH) // A_KV_WIDTH
    e4, eye = consts
    out_shape = (jax.ShapeDtypeStruct((b, s, A_WIDTH), BF16),
                 jax.ShapeDtypeStruct((b, WINDOW, A_KV_WIDTH), F32),
                 jax.ShapeDtypeStruct((b, WINDOW, A_KV_WIDTH), F32))
    full = lambda *shape: pl.BlockSpec(shape, lambda i, c: (0,) * len(shape))
    return pl.pallas_call(
        functools.partial(_swa_kernel, L=L, CS=CS, cache_len=cache_len),
        out_shape=out_shape,
        grid=(b, s // T),
        in_specs=[pl.BlockSpec((1, T, A_WIDTH), lambda i, c: (i, c, q_blk)),
                  pl.BlockSpec((1, T, A_KV_WIDTH), lambda i, c: (i, c, kv_blk)),
                  pl.BlockSpec((1, T, A_KV_WIDTH), lambda i, c: (i, c, kv_blk + 1)),
                  pl.BlockSpec((1, WINDOW, A_KV_WIDTH), lambda i, c: (i, 0, 0)),
                  pl.BlockSpec((1, WINDOW, A_KV_WIDTH), lambda i, c: (i, 0, 0)),
                  pl.BlockSpec((None, 1, A_WIDTH), lambda i, c: (layer, 0, 0)),
                  pl.BlockSpec((None, 1, A_KV_WIDTH), lambda i, c: (layer, 0, 0)),
                  pl.BlockSpec((None, 1, A_WIDTH), lambda i, c: (layer, 0, 0)),
                  full(nvar, A_KV_HEADS, lk, gl),
                  pl.BlockSpec((None, A_KV_HEADS, 1, gl), lambda i, c: (layer, 0, 0, 0)),
                  full(A_KV_WIDTH, A_KV_WIDTH),
                  full(A_KV_WIDTH, A_KV_WIDTH)],
        out_specs=(pl.BlockSpec((1, T, A_WIDTH), lambda i, c: (i, c, 0)),
                   pl.BlockSpec((1, WINDOW, A_KV_WIDTH), lambda i, c: (i, 0, 0)),
                   pl.BlockSpec((1, WINDOW, A_KV_WIDTH), lambda i, c: (i, 0, 0))),
        scratch_shapes=[pltpu.VMEM((WINDOW, A_KV_WIDTH), F32),
                        pltpu.VMEM((WINDOW, A_KV_WIDTH), F32)],
        compiler_params=_params(("arbitrary", "arbitrary")),
        name="swa",
    )(proj, proj, proj, ck, cv, g_q, g_k, g_ao, bias, sink, e4, eye)


def _out_proj_kernel(x_ref, hm_ref, ha_ref, wm_ref, wa_ref, o_ref):
    o_ref[...] = x_ref[...] + _dot(hm_ref[...], wm_ref[...]) + _dot(ha_ref[...], wa_ref[...])


def _out_proj(x, hm, ha, w_out, layer):
    m, d = x.shape
    tm = _row_tile(m, 512)
    return pl.pallas_call(
        _out_proj_kernel,
        out_shape=jax.ShapeDtypeStruct((m, d), F32),
        grid=(m // tm,),
        in_specs=[pl.BlockSpec((tm, d), lambda i: (i, 0)),
                  pl.BlockSpec((tm, M_WIDTH), lambda i: (i, 0)),
                  pl.BlockSpec((tm, A_WIDTH), lambda i: (i, 0)),
                  pl.BlockSpec((None, M_WIDTH, d), lambda i: (layer, 0, 0)),
                  pl.BlockSpec((None, A_WIDTH, d), lambda i: (layer, 1, 0))],
        out_specs=pl.BlockSpec((tm, d), lambda i: (i, 0)),
        compiler_params=_params(("parallel",)),
        name="out_proj",
    )(x, hm, ha, w_out, w_out)


def _ffn_kernel(x_ref, g_ref, wu_ref, wd_ref, o_ref, h_scr):
    f = pl.program_id(1)

    @pl.when(f == 0)
    def _():
        x = x_ref[...]
        h_scr[...] = _rms(x, g_ref[...]).astype(BF16)
        o_ref[...] = x

    u = _dot(h_scr[...], wu_ref[...])
    a = jnp.square(jnp.maximum(u, 0.0)).astype(BF16)
    o_ref[...] += _dot(a, wd_ref[...])


def _ffn(x, g_ffn, w_up, w_down, layer):
    m, d = x.shape
    d_ff = w_up.shape[-1]
    tm = _row_tile(m, 512)
    tf = 1024
    return pl.pallas_call(
        _ffn_kernel,
        out_shape=jax.ShapeDtypeStruct((m, d), F32),
        grid=(m // tm, d_ff // tf),
        in_specs=[pl.BlockSpec((tm, d), lambda i, f: (i, 0)),
                  pl.BlockSpec((None, 1, d), lambda i, f: (layer, 0, 0)),
                  pl.BlockSpec((None, d, tf), lambda i, f: (layer, 0, f)),
                  pl.BlockSpec((None, tf, d), lambda i, f: (layer, f, 0))],
        out_specs=pl.BlockSpec((tm, d), lambda i, f: (i, 0)),
        scratch_shapes=[pltpu.VMEM((tm, d), BF16)],
        compiler_params=_params(("parallel", "arbitrary")),
        name="ffn",
    )(x, g_ffn, w_up, w_down)


def _layer(x, wts, layer, lm, la, cs, bias, sink, ck, cv, c0, n0, m0, cache_len):
    b, s, d = x.shape
    x2 = x.reshape(b * s, d)
    eye = wts["consts"][1]
    if ck is None:
        hm, aproj, c_new, n_new, m_new = _proj_mlstm(x, wts["g_mix"], wts["w_all"], wts["gbias"], wts["g_mo"], eye,
                                                     c0, n0, m0, layer, lm)
        x2, k_new, v_new = _swa_out(aproj, 0, x2, hm.reshape(b * s, M_WIDTH), wts["w_out"], wts["g_q"], wts["g_k"],
                                    wts["g_ao"], bias, sink, wts["consts"], layer, la, cs)
    else:
        proj, gates = _in_proj(x2, wts["g_mix"], wts["w_all"], layer)
        proj = proj.reshape(b, s, PROJ_W)
        hm, c_new, n_new, m_new = _mlstm(proj, gates.reshape(b, s, GATE_W), wts["gbias"], wts["g_mo"], eye,
                                         c0, n0, m0, layer, lm)
        ha, k_new, v_new = _swa(proj, M_PROJ_W, ck, cv, wts["g_q"], wts["g_k"], wts["g_ao"], bias, sink, wts["consts"],
                                layer, la, cs, cache_len)
        x2 = _out_proj(x2, hm.reshape(b * s, M_WIDTH), ha.reshape(b * s, A_WIDTH), wts["w_out"], layer)
    x2 = _ffn(x2, wts["g_ffn"], wts["w_up"], wts["w_down"], layer)
    return x2.reshape(b, s, d), k_new, v_new, c_new, n_new, m_new


def _head_consts():
    hd = np.arange(A_KV_WIDTH) // A_HEAD_DIM
    e4 = (hd[:, None] == hd[None, :]).astype(np.float32)
    eye = np.eye(A_KV_WIDTH, dtype=np.float32)
    return tuple(jnp.asarray(a, BF16) for a in (e4, eye))


def kernel(x_prompt, x_sample, cache_k, cache_v, state_C, state_n, state_m, rel_bias, g_mix, w_in, b_i, b_f, g_q, g_k, sinks, g_mo, g_ao, w_out, g_ffn, w_up, w_down):
    depth = w_in.shape[0]
    bp, sp, d = x_prompt.shape
    bs, ls, _ = x_sample.shape
    n_win = cache_k.shape[2]
    assert sp % CHUNK == 0 and n_win == WINDOW and ls % 16 == 0 and ls <= CHUNK
    lm = M_CHUNK if sp % M_CHUNK == 0 else CHUNK

    o = np.cumsum((0, M_HEADS * M_DK, M_HEADS * M_DK, M_WIDTH, M_WIDTH, M_HEADS, M_HEADS, A_WIDTH, A_KV_WIDTH, A_KV_WIDTH))
    seg = [w_in[:, :, o[i]:o[i + 1]] for i in range(9)]
    aq = jnp.swapaxes(seg[6].reshape(depth, d, A_KV_HEADS, A_GROUP, A_HEAD_DIM), 2, 3).reshape(depth, d, A_WIDTH)
    heads = [p for h in range(M_HEADS) for p in (seg[0][..., h * M_DK:(h + 1) * M_DK], seg[1][..., h * M_DK:(h + 1) * M_DK],
                                                 seg[2][..., h * M_DV:(h + 1) * M_DV], seg[3][..., h * M_DV:(h + 1) * M_DV])]
    w_all = jnp.concatenate(heads + [aq, seg[7], seg[8], seg[4], seg[5],
                                     jnp.zeros((depth, d, GATE_W - 2 * M_HEADS), F32)], axis=-1).astype(BF16)
    gbias = jnp.concatenate([b_i, b_f, jnp.zeros((depth, GATE_W - 2 * M_HEADS), F32)], axis=-1)

    def regroup(a):
        rest = a.shape[2:]
        a = a.reshape(depth, A_KV_HEADS, A_GROUP, A_HEAD_DIM, *rest)
        return jnp.swapaxes(a, 1, 2).reshape(depth, A_WIDTH, *rest)

    w_out_b = jnp.concatenate([w_out[:, :M_WIDTH], regroup(w_out[:, M_WIDTH:])], axis=1).astype(BF16)
    wts = {
        "g_mix": g_mix.reshape(depth, 1, d), "w_all": w_all,
        "gbias": gbias.reshape(depth, 1, GATE_W), "g_mo": g_mo.reshape(depth, 1, M_WIDTH),
        "g_q": jnp.tile(g_q * A_SCALE, (1, A_HEADS)).reshape(depth, 1, A_WIDTH),
        "g_k": jnp.tile(g_k, (1, A_KV_HEADS)).reshape(depth, 1, A_KV_WIDTH),
        "g_ao": regroup(g_ao).reshape(depth, 1, A_WIDTH),
        "w_out": w_out_b, "g_ffn": g_ffn.reshape(depth, 1, d),
        "w_up": w_up.astype(BF16), "w_down": w_down.astype(BF16),
        "consts": _head_consts(),
    }

    bias = _bias_table(rel_bias)

    def bias_t(lq, cache_len):
        lk = WINDOW + lq
        t = bias[:, :lq, :lk].reshape(A_KV_HEADS, A_GROUP, lq, lk)
        t = t.transpose(0, 3, 1, 2).reshape(A_KV_HEADS, lk, A_GROUP * lq)
        n_prev = np.arange((WINDOW - cache_len) // lq + 1) * lq + cache_len
        dead = np.arange(lk)[None, :] < WINDOW - n_prev[:, None]
        return t[None] + jnp.asarray(np.where(dead, NEG_INF, 0.0)[:, None, :, None], F32)

    def sink_rows(lq):
        return jnp.repeat(sinks.reshape(depth, A_KV_HEADS, 1, A_GROUP), lq, axis=-1)

    zc =jnp.zeros((bp, M_HEADS, M_DV, M_DK), F32)
    zn = jnp.zeros((bp, M_HEADS, M_DK), F32)
    zm = jnp.zeros((bp, 1, M_HEADS), F32)

    xp, xs = x_prompt, x_sample
    bias_p, bias_s, sink_p, sink_s = bias_t(CHUNK, 0), bias_t(ls, WINDOW), sink_rows(CHUNK), sink_rows(ls)
    cs = A_CHUNKS_PER_STEP if sp % (A_CHUNKS_PER_STEP * CHUNK) == 0 else 1
    outs_p, outs_s = [], []
    for l in range(depth):
        xp, *st = _layer(xp, wts, l, lm, CHUNK, cs, bias_p, sink_p, None, None, zc, zn, zm, 0)
        outs_p.append(st)
        xs, *st = _layer(xs, wts, l, ls, ls, 1, bias_s, sink_s,
                         cache_k[l].reshape(bs, n_win, A_KV_WIDTH), cache_v[l].reshape(bs, n_win, A_KV_WIDTH),
                         state_C[l], state_n[l], state_m[l].reshape(bs, 1, M_HEADS), WINDOW)
        outs_s.append(st)

    def stack(outs, b):
        k, v, c, n, m = (jnp.stack([o_[i] for o_ in outs]) for i in range(5))
        return (k.reshape(depth, b, WINDOW, A_KV_HEADS, A_HEAD_DIM), v.reshape(depth, b, WINDOW, A_KV_HEADS, A_HEAD_DIM),
                c, n, m.reshape(depth, b, M_HEADS))

    return (xp, xs) + stack(outs_p, bp) + stack(outs_s, bs)
```

```python
import functools

import jax
import jax.numpy as jnp
import numpy as np
from jax import lax
from jax.experimental import pallas as pl
from jax.experimental.pallas import tpu as pltpu

F32 = jnp.float32
BF16 = jnp.bfloat16

CHUNK = 64
M_CHUNK = 256
A_CHUNKS_PER_STEP = 4
M_HEADS = 4
M_DK = 128
M_DV = 256
M_WIDTH = M_HEADS * M_DV
A_HEADS = 16
A_KV_HEADS = 4
A_HEAD_DIM = 64
A_GROUP = A_HEADS // A_KV_HEADS
A_WIDTH = A_HEADS * A_HEAD_DIM
A_KV_WIDTH = A_KV_HEADS * A_HEAD_DIM
WINDOW = 128
N_BUCKETS = 32
MAX_DISTANCE = 128
EPS = 1e-6
NEG_INF = -1e30
M_SCALE = M_DK ** -0.5
A_SCALE = A_HEAD_DIM ** -0.5

HEAD_W = 2 * M_DK + 2 * M_DV
M_PROJ_W = M_HEADS * HEAD_W
A_PROJ_W = A_WIDTH + 2 * A_KV_WIDTH
PROJ_W = M_PROJ_W + A_PROJ_W
GATE_W = 128

V7X_VMEM_LIMIT = 56 * 1024 * 1024

NT_DIMS = (((1,), (1,)), ((), ()))


def _params(sem, vmem=V7X_VMEM_LIMIT):
    return pltpu.CompilerParams(dimension_semantics=sem, vmem_limit_bytes=vmem)


def _row_tile(m, cap):
    t = min(m, cap)
    while m % t:
        t //= 2
    return t


def _dot(a, b):
    return jnp.dot(a, b, preferred_element_type=F32)


def _dot_nt(a, b):
    return lax.dot_general(a, b, NT_DIMS, preferred_element_type=F32)


def _t5_bucket(rel):
    half = N_BUCKETS // 2
    exact = half // 2
    n = np.abs(rel)
    large = exact + (np.log(np.maximum(n, 1) / exact) / np.log(MAX_DISTANCE / exact) * (half - exact)).astype(np.int32)
    large = np.minimum(large, half - 1)
    return (rel > 0).astype(np.int32) * half + np.where(n < exact, n, large).astype(np.int32)


def _bias_kernel(rel_ref, map_ref, out_ref):
    bmap = map_ref[...]
    for h in range(A_HEADS):
        acc = jnp.zeros(bmap.shape, F32)
        for b in range(N_BUCKETS):
            acc = jnp.where(bmap == b, rel_ref[b, h], acc)
        out_ref[h] = acc


def _bias_table(rel_bias):
    lk = WINDOW + CHUNK
    rel = (np.arange(lk)[None, :] - WINDOW) - np.arange(CHUNK)[:, None]
    bmap = jnp.asarray(_t5_bucket(rel), jnp.int32)
    return pl.pallas_call(
        _bias_kernel,
        out_shape=jax.ShapeDtypeStruct((A_HEADS, CHUNK, lk), F32),
        in_specs=[pl.BlockSpec(memory_space=pltpu.SMEM),
                  pl.BlockSpec(memory_space=pltpu.VMEM)],
        out_specs=pl.BlockSpec(memory_space=pltpu.VMEM),
        name="t5_bias_table",
    )(rel_bias, bmap)


def _rms(x, g):
    return x * lax.rsqrt(jnp.mean(x * x, axis=-1, keepdims=True) + EPS) * g


def _in_proj_kernel(x_ref, g_ref, w_ref, o_ref, gate_ref, *, sub):
    for r in range(0, x_ref.shape[0], sub):
        h = _rms(x_ref[r:r + sub, :], g_ref[...]).astype(BF16)
        acc = _dot(h, w_ref[...])
        o_ref[r:r + sub, :] = acc[:, :PROJ_W].astype(o_ref.dtype)
        gate_ref[r:r + sub, :] = acc[:, PROJ_W:]


def _in_proj(x, g_mix, w_all, layer):
    m, d = x.shape
    tm = _row_tile(m, 512)
    n = PROJ_W + GATE_W
    return pl.pallas_call(
        functools.partial(_in_proj_kernel, sub=min(tm, 256)),
        out_shape=(jax.ShapeDtypeStruct((m, PROJ_W), BF16), jax.ShapeDtypeStruct((m, GATE_W), F32)),
        grid=(m // tm,),
        in_specs=[pl.BlockSpec((tm, d), lambda i: (i, 0)),
                  pl.BlockSpec((None, 1, d), lambda i: (layer, 0, 0)),
                  pl.BlockSpec((None, d, n), lambda i: (layer, 0, 0), pipeline_mode=pl.Buffered(1))],
        out_specs=(pl.BlockSpec((tm, PROJ_W), lambda i: (i, 0)),
                   pl.BlockSpec((tm, GATE_W), lambda i: (i, 0))),
        compiler_params=_params(("parallel",)),
        name="in_proj",
    )(x, g_mix, w_all)


def _mlstm_body(head_inputs, gates, gbias_ref, gmo_ref, eye_ref, c0_ref, n0_ref, m0_ref,
                hm_ref, cout_ref, nout_ref, mout_ref, c_scr, n_scr, m_scr, *, L):
    c = pl.program_id(1)
    last = pl.num_programs(1) - 1

    @pl.when(c == 0)
    def _():
        c_scr[...] = c0_ref[0]
        n_scr[...] = n0_ref[0]
        for h in range(M_HEADS):
            m_scr[h:h + 1, :] = jnp.broadcast_to(m0_ref[0, :, h:h + 1], (1, 128))

    a_all = gates + gbias_ref[...]
    logf = jnp.minimum(a_all, 0.0) - jnp.log(1.0 + jnp.exp(-jnp.abs(a_all)))
    row = lax.broadcasted_iota(jnp.int32, (L, GATE_W), 0)
    b_all = logf
    k = 1
    while k < L:
        b_all = b_all + jnp.where(row >= k, pltpu.roll(b_all, k, axis=0), 0.0)
        k *= 2
    b_sh = pltpu.roll(b_all, GATE_W - M_HEADS, axis=1)
    a_i = a_all - b_sh
    cmax = a_i
    k = 1
    while k < L:
        cmax = jnp.maximum(cmax, jnp.where(row >= k, pltpu.roll(cmax, k, axis=0), NEG_INF))
        k *= 2
    a_t = a_i.T
    tri = lax.broadcasted_iota(jnp.int32, (L, L), 0) >= lax.broadcasted_iota(jnp.int32, (L, L), 1)
    eye = eye_ref[...]

    for h in range(M_HEADS):
        b_col = b_sh[:, h:h + 1]
        a_col = a_i[:, h:h + 1]
        a_row = a_t[h:h + 1, :]
        m_prev = m_scr[h:h + 1, 0:1]
        m_run = jnp.maximum(cmax[:, h:h + 1], m_prev)
        decay_w = jnp.exp(jnp.where(tri, a_row - m_run, NEG_INF))
        qh, kh, vh, og = head_inputs(h)
        w = decay_w * (_dot_nt(qh, kh) * M_SCALE)
        g = jnp.exp(m_prev - m_run)
        cst = c_scr[h]
        n_row = n_scr[h:h + 1, :]
        num = _dot(w.astype(BF16), vh) + g * _dot_nt(qh, cst.astype(BF16))
        den = (jnp.sum(w, axis=1, keepdims=True)
               + g * jnp.sum(qh.astype(F32) * n_row, axis=1, keepdims=True))
        inv = 1.0 / jnp.maximum(jnp.abs(den), jnp.exp(-(b_col + m_run)))
        ms = jnp.mean(num * num, axis=1, keepdims=True)
        scale = inv * lax.rsqrt(inv * inv * ms + EPS)
        hm_ref[0, :, h * M_DV:(h + 1) * M_DV] = (
            num * scale * gmo_ref[:, h * M_DV:(h + 1) * M_DV] * jax.nn.sigmoid(og)).astype(hm_ref.dtype)

        b_last = b_col[L - 1:L, :]
        m_end = b_last + m_run[L - 1:L, :]
        wk = jnp.exp(b_last + a_col - m_end)
        decay = jnp.exp(b_last + m_prev - m_end)
        vw = (vh.astype(F32) * wk).astype(BF16)
        vw_t = _dot_nt(eye, vw).astype(BF16)
        c_new = decay * cst + M_SCALE * _dot(vw_t, kh)
        n_new = decay * n_row + M_SCALE * jnp.sum(kh.astype(F32) * wk, axis=0, keepdims=True)
        c_scr[h] = c_new
        n_scr[h:h + 1, :] = n_new
        m_scr[h:h + 1, :] = jnp.broadcast_to(m_end, (1, 128))

        @pl.when(c == last)
        def _():
            cout_ref[0, h] = c_new
            nout_ref[0, h:h + 1, :] = n_new
            mout_ref[0, :, h:h + 1] = m_end


def _mlstm_kernel(p_ref, gate_ref, *rest, L):
    def head_inputs(h):
        o = h * HEAD_W
        return (p_ref[0, :, o:o + M_DK], p_ref[0, :, o + M_DK:o + 2 * M_DK],
                p_ref[0, :, o + 2 * M_DK:o + 2 * M_DK + M_DV], p_ref[0, :, o + 2 * M_DK + M_DV:o + HEAD_W].astype(F32))

    _mlstm_body(head_inputs, gate_ref[0], *rest, L=L)


def _proj_mlstm_kernel(x_ref, g_ref, w_ref, gbias_ref, gmo_ref, eye_ref, c0_ref, n0_ref, m0_ref,
                       hm_ref, ap_ref, cout_ref, nout_ref, mout_ref, c_scr, n_scr, m_scr, *, L):
    hn = _rms(x_ref[0], g_ref[...]).astype(BF16)

    def head_inputs(h):
        o = h * HEAD_W
        ph = _dot(hn, w_ref[:, o:o + HEAD_W])
        return (ph[:, :M_DK].astype(BF16), ph[:, M_DK:2 * M_DK].astype(BF16),
                ph[:, 2 * M_DK:2 * M_DK + M_DV].astype(BF16), ph[:, 2 * M_DK + M_DV:])

    _mlstm_body(head_inputs, _dot(hn, w_ref[:, PROJ_W:]), gbias_ref, gmo_ref, eye_ref, c0_ref, n0_ref, m0_ref,
                hm_ref, cout_ref, nout_ref, mout_ref, c_scr, n_scr, m_scr, L=L)
    ap_ref[0] = _dot(hn, w_ref[:, M_PROJ_W:PROJ_W]).astype(ap_ref.dtype)


def _mlstm_specs(b, layer, L):
    state = [pl.BlockSpec((1, M_HEADS, M_DV, M_DK), lambda i, c: (i, 0, 0, 0)),
             pl.BlockSpec((1, M_HEADS, M_DK), lambda i, c: (i, 0, 0)),
             pl.BlockSpec((1, 1, M_HEADS), lambda i, c: (i, 0, 0))]
    params = [pl.BlockSpec((None, 1, GATE_W), lambda i, c: (layer, 0, 0)),
              pl.BlockSpec((None, 1, M_WIDTH), lambda i, c: (layer, 0, 0)),
              pl.BlockSpec((M_DV, M_DV), lambda i, c: (0, 0))]
    state_shapes = [jax.ShapeDtypeStruct((b, M_HEADS, M_DV, M_DK), F32),
                    jax.ShapeDtypeStruct((b, M_HEADS, M_DK), F32),
                    jax.ShapeDtypeStruct((b, 1, M_HEADS), F32)]
    scratch = [pltpu.VMEM((M_HEADS, M_DV, M_DK), F32), pltpu.VMEM((M_HEADS, M_DK), F32), pltpu.VMEM((M_HEADS, 128), F32)]
    return params + state, state, state_shapes, scratch


def _mlstm(proj, gates, gbias, g_mo, eye, c0, n0, m0, layer, L):
    b, s, _ = proj.shape
    ins, state_out, state_shapes, scratch = _mlstm_specs(b, layer, L)
    return pl.pallas_call(
        functools.partial(_mlstm_kernel, L=L),
        out_shape=[jax.ShapeDtypeStruct((b, s, M_WIDTH), BF16)] + state_shapes,
        grid=(b, s // L),
        in_specs=[pl.BlockSpec((1, L, M_PROJ_W), lambda i, c: (i, c, 0)),
                  pl.BlockSpec((1, L, GATE_W), lambda i, c: (i, c, 0))] + ins,
        out_specs=[pl.BlockSpec((1, L, M_WIDTH), lambda i, c: (i, c, 0))] + state_out,
        scratch_shapes=scratch,
        compiler_params=_params(("arbitrary", "arbitrary")),
        name="mlstm",
    )(proj, gates, gbias, g_mo, eye, c0, n0, m0)


def _proj_mlstm(x, g_mix, w_all, gbias, g_mo, eye, c0, n0, m0, layer, L):
    b, s, d = x.shape
    ins, state_out, state_shapes, scratch = _mlstm_specs(b, layer, L)
    return pl.pallas_call(
        functools.partial(_proj_mlstm_kernel, L=L),
        out_shape=[jax.ShapeDtypeStruct((b, s, M_WIDTH), BF16), jax.ShapeDtypeStruct((b, s, A_PROJ_W), BF16)] + state_shapes,
        grid=(b, s // L),
        in_specs=[pl.BlockSpec((1, L, d), lambda i, c: (i, c, 0)),
                  pl.BlockSpec((None, 1, d), lambda i, c: (layer, 0, 0)),
                  pl.BlockSpec((None, d, PROJ_W + GATE_W), lambda i, c: (layer, 0, 0), pipeline_mode=pl.Buffered(1))] + ins,
        out_specs=[pl.BlockSpec((1, L, M_WIDTH), lambda i, c: (i, c, 0)),
                   pl.BlockSpec((1, L, A_PROJ_W), lambda i, c: (i, c, 0))] + state_out,
        scratch_shapes=scratch,
        compiler_params=_params(("arbitrary", "arbitrary")),
        name="proj_mlstm",
    )(x, g_mix, w_all, gbias, g_mo, eye, c0, n0, m0)


def _swa_body(c, is_last, q_ref, k_ref, v_ref, ck_ref, cv_ref, gq_ref, gk_ref, gao_ref, bias_ref, sink_ref,
              e4_ref, eye_ref, cko_ref, cvo_ref, khist, vhist, write_ha, *, L, CS, cache_len, side_work=None):
    T = CS * L
    lk = WINDOW + L
    gl = A_GROUP * L
    kv_lane = lax.broadcasted_iota(jnp.int32, (1, A_KV_WIDTH), 1) // A_HEAD_DIM

    def lane_masked(kn):
        knb = kn.astype(BF16)
        return [jnp.where(kv_lane == kv, knb, 0.0) for kv in range(A_KV_HEADS)]

    @pl.when(c == 0)
    def _():
        khist[...] = ck_ref[0]
        vhist[...] = cv_ref[0]

    q = q_ref[0].astype(F32)
    k = k_ref[0].astype(F32)
    sq = jnp.concatenate([q[:, g * A_KV_WIDTH:(g + 1) * A_KV_WIDTH] for g in range(A_GROUP)] + [k], axis=0)
    sq = sq * sq
    sq_hi = sq.astype(BF16)
    sq_lo = (sq - sq_hi.astype(F32)).astype(BF16)
    ss = _dot(jnp.concatenate([sq_hi, sq_lo], axis=0), e4_ref[...])
    rinv = lax.rsqrt((ss[:5 * T] + ss[5 * T:]) * (1.0 / A_HEAD_DIM) + EPS)
    qn = [(q[:, g * A_KV_WIDTH:(g + 1) * A_KV_WIDTH] * rinv[g * T:(g + 1) * T, :]
           * gq_ref[:, g * A_KV_WIDTH:(g + 1) * A_KV_WIDTH]).astype(BF16) for g in range(A_GROUP)]
    kn = k * rinv[A_GROUP * T:, :] * gk_ref[...]

    k_all = jnp.concatenate([khist[...], kn], axis=0)
    v_all = jnp.concatenate([vhist[...], v_ref[0].astype(F32)], axis=0)
    vb_all = v_all.astype(BF16)

    for j in range(CS):
        rows = slice(j * L, j * L + lk)
        qst = jnp.concatenate([qn[g][j * L:(j + 1) * L, :] for g in range(A_GROUP)], axis=0)
        km = jnp.concatenate(lane_masked(k_all[rows]), axis=0)
        s_all = _dot_nt(km, qst)
        v_t = _dot_nt(eye_ref[...], vb_all[rows]).astype(BF16)
        var = jnp.minimum((c * CS + j) * L, WINDOW - cache_len) // L
        o_t = []
        if side_work is not None:
            side_work(j, CS)
        for kv in range(A_KV_HEADS):
            s = s_all[kv * lk:(kv + 1) * lk, :] + bias_ref[var, kv]
            sk = sink_ref[kv]
            mx = jnp.maximum(jnp.max(s, axis=0, keepdims=True), sk)
            p = jnp.exp(s - mx)
            den = jnp.sum(p, axis=0, keepdims=True) + jnp.exp(sk - mx)
            o = _dot(v_t[kv * A_HEAD_DIM:(kv + 1) * A_HEAD_DIM, :], p.astype(BF16))
            o_t.append(o * (1.0 / den))
        o_all = jnp.concatenate(o_t, axis=0).T
        ha = jnp.concatenate([o_all[g * L:(g + 1) * L, :] for g in range(A_GROUP)], axis=1)
        write_ha(j, _rms(ha, gao_ref[...]).astype(BF16))

    k_win = k_all[T:T + WINDOW]
    v_win = v_all[T:T + WINDOW]
    khist[...] = k_win
    vhist[...] = v_win

    @pl.when(is_last)
    def _():
        cko_ref[0] = k_win
        cvo_ref[0] = v_win


def _swa_kernel(*refs, L, CS, cache_len):
    ins, (ha_ref, cko_ref, cvo_ref, khist, vhist) = refs[:12], refs[12:]

    def write_ha(j, rows):
        ha_ref[0, j * L:(j + 1) * L, :] = rows

    _swa_body(pl.program_id(1), pl.program_id(1) == pl.num_programs(1) - 1, *ins, cko_ref, cvo_ref, khist, vhist,
              write_ha, L=L, CS=CS, cache_len=cache_len)


def _swa_out_kernel(*refs, L, CS, nc, n):
    ins, (x_ref, hm_ref, wm_ref, wa_ref), (o_ref, cko_ref, cvo_ref, khist, vhist, ha_prev) = (
        refs[:12], refs[12:16], refs[16:])
    s = pl.program_id(0)
    c = jnp.minimum(s, n - 1) % nc

    @pl.when(s == 0)
    def _():
        ha_prev[...] = jnp.zeros(ha_prev.shape, ha_prev.dtype)

    hm = hm_ref[...]
    ha_old = ha_prev[...]

    def project(j, parts):
        w = o_ref.shape[1] // parts
        cols = slice(j * w, (j + 1) * w)
        o_ref[:, cols] = x_ref[:, cols] + _dot(hm, wm_ref[:, cols]) + _dot(ha_old, wa_ref[:, cols])

    def write_ha(j, rows):
        ha_prev[j * L:(j + 1) * L, :] = rows

    _swa_body(c, jnp.logical_and(s < n, c == nc - 1), *ins, cko_ref, cvo_ref, khist, vhist,
              write_ha, L=L, CS=CS, cache_len=0, side_work=project)


def _swa_out(proj, col0, x, hm, w_out, g_q, g_k, g_ao, bias, sink, consts, layer, L, CS):
    b, s, _ = proj.shape
    m, d = x.shape
    T = CS * L
    nc = s // T
    n = b * nc
    lk = WINDOW + L
    gl = A_GROUP * L
    nvar = bias.shape[0]
    q_blk, kv_blk = col0 // A_WIDTH, (col0 + A_WIDTH) // A_KV_WIDTH
    e4, eye = consts
    zk = jnp.zeros((b, WINDOW, A_KV_WIDTH), F32)
    cur = lambda s_: jnp.minimum(s_, n - 1)
    prev = lambda s_: jnp.maximum(s_ - 1, 0)
    full = lambda *shape: pl.BlockSpec(shape, lambda s_: (0,) * len(shape))
    out_shape = (jax.ShapeDtypeStruct((m, d), F32),
                 jax.ShapeDtypeStruct((b, WINDOW, A_KV_WIDTH), F32),
                 jax.ShapeDtypeStruct((b, WINDOW, A_KV_WIDTH), F32))
    return pl.pallas_call(
        functools.partial(_swa_out_kernel, L=L, CS=CS, nc=nc, n=n),
        out_shape=out_shape,
        grid=(n + 1,),
        in_specs=[pl.BlockSpec((1, T, A_WIDTH), lambda s_: (cur(s_) // nc, cur(s_) % nc, q_blk)),
                  pl.BlockSpec((1, T, A_KV_WIDTH), lambda s_: (cur(s_) // nc, cur(s_) % nc, kv_blk)),
                  pl.BlockSpec((1, T, A_KV_WIDTH), lambda s_: (cur(s_) // nc, cur(s_) % nc, kv_blk + 1)),
                  pl.BlockSpec((1, WINDOW, A_KV_WIDTH), lambda s_: (cur(s_) // nc, 0, 0)),
                  pl.BlockSpec((1, WINDOW, A_KV_WIDTH), lambda s_: (cur(s_) // nc, 0, 0)),
                  pl.BlockSpec((None, 1, A_WIDTH), lambda s_: (layer, 0, 0)),
                  pl.BlockSpec((None, 1, A_KV_WIDTH), lambda s_: (layer, 0, 0)),
                  pl.BlockSpec((None, 1, A_WIDTH), lambda s_: (layer, 0, 0)),
                  full(nvar, A_KV_HEADS, lk, gl),
                  pl.BlockSpec((None, A_KV_HEADS, 1, gl), lambda s_: (layer, 0, 0, 0)),
                  full(A_KV_WIDTH, A_KV_WIDTH),
                  full(A_KV_WIDTH, A_KV_WIDTH),
                  pl.BlockSpec((T, d), lambda s_: (prev(s_), 0)),
                  pl.BlockSpec((T, M_WIDTH), lambda s_: (prev(s_), 0)),
                  pl.BlockSpec((None, M_WIDTH, d), lambda s_: (layer, 0, 0), pipeline_mode=pl.Buffered(1)),
                  pl.BlockSpec((None, A_WIDTH, d), lambda s_: (layer, 1, 0), pipeline_mode=pl.Buffered(1))],
        out_specs=(pl.BlockSpec((T, d), lambda s_: (prev(s_), 0)),
                   pl.BlockSpec((1, WINDOW, A_KV_WIDTH), lambda s_: (cur(s_) // nc, 0, 0)),
                   pl.BlockSpec((1, WINDOW, A_KV_WIDTH), lambda s_: (cur(s_) // nc, 0, 0))),
        scratch_shapes=[pltpu.VMEM((WINDOW, A_KV_WIDTH), F32),
                        pltpu.VMEM((WINDOW, A_KV_WIDTH), F32),
                        pltpu.VMEM((T, A_WIDTH), BF16)],
        compiler_params=_params(("arbitrary",)),
        name="swa_out",
    )(proj, proj, proj, zk, zk, g_q, g_k, g_ao, bias, sink, e4, eye, x, hm, w_out, w_out)


def _swa(proj, col0, ck, cv, g_q, g_k, g_ao, bias, sink, consts, layer, L, CS, cache_len):
    b, s, _ = proj.shape
    T = CS * L
    lk = WINDOW + L
    gl = A_GROUP * L
    nvar = bias.shape[0]
    q_blk, kv_blk = col0 // A_WIDTH, (col0 + A_WIDTH) // A_KV_WIDTH
    e4, eye = consts
    out_shape = (jax.ShapeDtypeStruct((b, s, A_WIDTH), BF16),
                 jax.ShapeDtypeStruct((b, WINDOW, A_KV_WIDTH), F32),
                 jax.ShapeDtypeStruct((b, WINDOW, A_KV_WIDTH), F32))
    full = lambda *shape: pl.BlockSpec(shape, lambda i, c: (0,) * len(shape))
    return pl.pallas_call(
        functools.partial(_swa_kernel, L=L, CS=CS, cache_len=cache_len),
        out_shape=out_shape,
        grid=(b, s // T),
        in_specs=[pl.BlockSpec((1, T, A_WIDTH), lambda i, c: (i, c, q_blk)),
                  pl.BlockSpec((1, T, A_KV_WIDTH), lambda i, c: (i, c, kv_blk)),
                  pl.BlockSpec((1, T, A_KV_WIDTH), lambda i, c: (i, c, kv_blk + 1)),
                  pl.BlockSpec((1, WINDOW, A_KV_WIDTH), lambda i, c: (i, 0, 0)),
                  pl.BlockSpec((1, WINDOW, A_KV_WIDTH), lambda i, c: (i, 0, 0)),
                  pl.BlockSpec((None, 1, A_WIDTH), lambda i, c: (layer, 0, 0)),
                  pl.BlockSpec((None, 1, A_KV_WIDTH), lambda i, c: (layer, 0, 0)),
                  pl.BlockSpec((None, 1, A_WIDTH), lambda i, c: (layer, 0, 0)),
                  full(nvar, A_KV_HEADS, lk, gl),
                  pl.BlockSpec((None, A_KV_HEADS, 1, gl), lambda i, c: (layer, 0, 0, 0)),
                  full(A_KV_WIDTH, A_KV_WIDTH),
                  full(A_KV_WIDTH, A_KV_WIDTH)],
        out_specs=(pl.BlockSpec((1, T, A_WIDTH), lambda i, c: (i, c, 0)),
                   pl.BlockSpec((1, WINDOW, A_KV_WIDTH), lambda i, c: (i, 0, 0)),
                   pl.BlockSpec((1, WINDOW, A_KV_WIDTH), lambda i, c: (i, 0, 0))),
        scratch_shapes=[pltpu.VMEM((WINDOW, A_KV_WIDTH), F32),
                        pltpu.VMEM((WINDOW, A_KV_WIDTH), F32)],
        compiler_params=_params(("arbitrary", "arbitrary")),
        name="swa",
    )(proj, proj, proj, ck, cv, g_q, g_k, g_ao, bias, sink, e4, eye)


def _out_proj_kernel(x_ref, hm_ref, ha_ref, wm_ref, wa_ref, o_ref):
    o_ref[...] = x_ref[...] + _dot(hm_ref[...], wm_ref[...]) + _dot(ha_ref[...], wa_ref[...])


def _out_proj(x, hm, ha, w_out, layer):
    m, d = x.shape
    tm = _row_tile(m, 512)
    return pl.pallas_call(
        _out_proj_kernel,
        out_shape=jax.ShapeDtypeStruct((m, d), F32),
        grid=(m // tm,),
        in_specs=[pl.BlockSpec((tm, d), lambda i: (i, 0)),
                  pl.BlockSpec((tm, M_WIDTH), lambda i: (i, 0)),
                  pl.BlockSpec((tm, A_WIDTH), lambda i: (i, 0)),
                  pl.BlockSpec((None, M_WIDTH, d), lambda i: (layer, 0, 0)),
                  pl.BlockSpec((None, A_WIDTH, d), lambda i: (layer, 1, 0))],
        out_specs=pl.BlockSpec((tm, d), lambda i: (i, 0)),
        compiler_params=_params(("parallel",)),
        name="out_proj",
    )(x, hm, ha, w_out, w_out)


def _ffn_kernel(x_ref, g_ref, wu_ref, wd_ref, o_ref, h_scr):
    f = pl.program_id(1)

    @pl.when(f == 0)
    def _():
        x = x_ref[...]
        h_scr[...] = _rms(x, g_ref[...]).astype(BF16)
        o_ref[...] = x

    u = _dot(h_scr[...], wu_ref[...])
    a = jnp.square(jnp.maximum(u, 0.0)).astype(BF16)
    o_ref[...] += _dot(a, wd_ref[...])


def _ffn(x, g_ffn, w_up, w_down, layer):
    m, d = x.shape
    d_ff = w_up.shape[-1]
    tm = _row_tile(m, 1024)
    tf = 512 if tm == 1024 else 1024
    return pl.pallas_call(
        _ffn_kernel,
        out_shape=jax.ShapeDtypeStruct((m, d), F32),
        grid=(m // tm, d_ff // tf),
        in_specs=[pl.BlockSpec((tm, d), lambda i, f: (i, 0)),
                  pl.BlockSpec((None, 1, d), lambda i, f: (layer, 0, 0)),
                  pl.BlockSpec((None, d, tf), lambda i, f: (layer, 0, f)),
                  pl.BlockSpec((None, tf, d), lambda i, f: (layer, f, 0))],
        out_specs=pl.BlockSpec((tm, d), lambda i, f: (i, 0)),
        scratch_shapes=[pltpu.VMEM((tm, d), BF16)],
        compiler_params=_params(("parallel", "arbitrary")),
        name="ffn",
    )(x, g_ffn, w_up, w_down)


def _layer(x, wts, layer, lm, la, cs, bias, sink, ck, cv, c0, n0, m0, cache_len):
    b, s, d = x.shape
    x2 = x.reshape(b * s, d)
    eye = wts["consts"][1]
    if ck is None:
        hm, aproj, c_new, n_new, m_new = _proj_mlstm(x, wts["g_mix"], wts["w_all"], wts["gbias"], wts["g_mo"], eye,
                                                     c0, n0, m0, layer, lm)
        x2, k_new, v_new = _swa_out(aproj, 0, x2, hm.reshape(b * s, M_WIDTH), wts["w_out"], wts["g_q"], wts["g_k"],
                                    wts["g_ao"], bias, sink, wts["consts"], layer, la, cs)
    else:
        proj, gates = _in_proj(x2, wts["g_mix"], wts["w_all"], layer)
        proj = proj.reshape(b, s, PROJ_W)
        hm, c_new, n_new, m_new = _mlstm(proj, gates.reshape(b, s, GATE_W), wts["gbias"], wts["g_mo"], eye,
                                         c0, n0, m0, layer, lm)
        ha, k_new, v_new = _swa(proj, M_PROJ_W, ck, cv, wts["g_q"], wts["g_k"], wts["g_ao"], bias, sink, wts["consts"],
                                layer, la, cs, cache_len)
        x2 = _out_proj(x2, hm.reshape(b * s, M_WIDTH), ha.reshape(b * s, A_WIDTH), wts["w_out"], layer)
    x2 = _ffn(x2, wts["g_ffn"], wts["w_up"], wts["w_down"], layer)
    return x2.reshape(b, s, d), k_new, v_new, c_new, n_new, m_new


def _head_consts():
    hd = np.arange(A_KV_WIDTH) // A_HEAD_DIM
    e4 = (hd[:, None] == hd[None, :]).astype(np.float32)
    eye = np.eye(A_KV_WIDTH, dtype=np.float32)
    return tuple(jnp.asarray(a, BF16) for a in (e4, eye))


def kernel(x_prompt, x_sample, cache_k, cache_v, state_C, state_n, state_m, rel_bias, g_mix, w_in, b_i, b_f, g_q, g_k, sinks, g_mo, g_ao, w_out, g_ffn, w_up, w_down):
    depth = w_in.shape[0]
    bp, sp, d = x_prompt.shape
    bs, ls, _ = x_sample.shape
    n_win = cache_k.shape[2]
    assert sp % CHUNK == 0 and n_win == WINDOW and ls % 16 == 0 and ls <= CHUNK
    lm = M_CHUNK if sp % M_CHUNK == 0 else CHUNK

    o = np.cumsum((0, M_HEADS * M_DK, M_HEADS * M_DK, M_WIDTH, M_WIDTH, M_HEADS, M_HEADS, A_WIDTH, A_KV_WIDTH, A_KV_WIDTH))
    seg = [w_in[:, :, o[i]:o[i + 1]] for i in range(9)]
    aq = jnp.swapaxes(seg[6].reshape(depth, d, A_KV_HEADS, A_GROUP, A_HEAD_DIM), 2, 3).reshape(depth, d, A_WIDTH)
    heads = [p for h in range(M_HEADS) for p in (seg[0][..., h * M_DK:(h + 1) * M_DK], seg[1][..., h * M_DK:(h + 1) * M_DK],
                                                 seg[2][..., h * M_DV:(h + 1) * M_DV], seg[3][..., h * M_DV:(h + 1) * M_DV])]
    w_all = jnp.concatenate(heads + [aq, seg[7], seg[8], seg[4], seg[5],
                                     jnp.zeros((depth, d, GATE_W - 2 * M_HEADS), F32)], axis=-1).astype(BF16)
    gbias = jnp.concatenate([b_i, b_f, jnp.zeros((depth, GATE_W - 2 * M_HEADS), F32)], axis=-1)

    def regroup(a):
        rest = a.shape[2:]
        a = a.reshape(depth, A_KV_HEADS, A_GROUP, A_HEAD_DIM, *rest)
        return jnp.swapaxes(a, 1, 2).reshape(depth, A_WIDTH, *rest)

    w_out_b = jnp.concatenate([w_out[:, :M_WIDTH], regroup(w_out[:, M_WIDTH:])], axis=1).astype(BF16)
    wts = {
        "g_mix": g_mix.reshape(depth, 1, d), "w_all": w_all,
        "gbias": gbias.reshape(depth, 1, GATE_W), "g_mo": g_mo.reshape(depth, 1, M_WIDTH),
        "g_q": jnp.tile(g_q * A_SCALE, (1, A_HEADS)).reshape(depth, 1, A_WIDTH),
        "g_k": jnp.tile(g_k, (1, A_KV_HEADS)).reshape(depth, 1, A_KV_WIDTH),
        "g_ao": regroup(g_ao).reshape(depth, 1, A_WIDTH),
        "w_out": w_out_b, "g_ffn": g_ffn.reshape(depth, 1, d),
        "w_up": w_up.astype(BF16), "w_down": w_down.astype(BF16),
        "consts": _head_consts(),
    }

    bias = _bias_table(rel_bias)

    def bias_t(lq, cache_len):
        lk = WINDOW + lq
        t = bias[:, :lq, :lk].reshape(A_KV_HEADS, A_GROUP, lq, lk)
        t = t.transpose(0, 3, 1, 2).reshape(A_KV_HEADS, lk, A_GROUP * lq)
        n_prev = np.arange((WINDOW - cache_len) // lq + 1) * lq + cache_len
        dead = np.arange(lk)[None, :] < WINDOW - n_prev[:, None]
        return t[None] + jnp.asarray(np.where(dead, NEG_INF, 0.0)[:, None, :, None], F32)

    def sink_rows(lq):
        return jnp.repeat(sinks.reshape(depth, A_KV_HEADS, 1, A_GROUP), lq, axis=-1)

    zc =jnp.zeros((bp, M_HEADS, M_DV, M_DK), F32)
    zn = jnp.zeros((bp, M_HEADS, M_DK), F32)
    zm = jnp.zeros((bp, 1, M_HEADS), F32)

    xp, xs = x_prompt, x_sample
    bias_p, bias_s, sink_p, sink_s = bias_t(CHUNK, 0), bias_t(ls, WINDOW), sink_rows(CHUNK), sink_rows(ls)
    cs = A_CHUNKS_PER_STEP if sp % (A_CHUNKS_PER_STEP * CHUNK) == 0 else 1
    outs_p, outs_s = [], []
    for l in range(depth):
        xp, *st = _layer(xp, wts, l, lm, CHUNK, cs, bias_p, sink_p, None, None, zc, zn, zm, 0)
        outs_p.append(st)
        xs, *st = _layer(xs, wts, l, ls, ls, 1, bias_s, sink_s,
                         cache_k[l].reshape(bs, n_win, A_KV_WIDTH), cache_v[l].reshape(bs, n_win, A_KV_WIDTH),
                         state_C[l], state_n[l], state_m[l].reshape(bs, 1, M_HEADS), WINDOW)
        outs_s.append(st)

    def stack(outs, b):
        k, v, c, n, m = (jnp.stack([o_[i] for o_ in outs]) for i in range(5))
        return (k.reshape(depth, b, WINDOW, A_KV_HEADS, A_HEAD_DIM), v.reshape(depth, b, WINDOW, A_KV_HEADS, A_HEAD_DIM),
                c, n, m.reshape(depth, b, M_HEADS))

    return (xp, xs) + stack(outs_p, bp) + stack(outs_s, bs)
```

```python
import functools

import jax
import jax.numpy as jnp
import numpy as np
from jax import lax
from jax.experimental import pallas as pl
from jax.experimental.pallas import tpu as pltpu

F32 = jnp.float32
BF16 = jnp.bfloat16

CHUNK = 64
M_CHUNK = 256
A_CHUNKS_PER_STEP = 4
M_HEADS = 4
M_DK = 128
M_DV = 256
M_WIDTH = M_HEADS * M_DV
A_HEADS = 16
A_KV_HEADS = 4
A_HEAD_DIM = 64
A_GROUP = A_HEADS // A_KV_HEADS
A_WIDTH = A_HEADS * A_HEAD_DIM
A_KV_WIDTH = A_KV_HEADS * A_HEAD_DIM
WINDOW = 128
N_BUCKETS = 32
MAX_DISTANCE = 128
EPS = 1e-6
NEG_INF = -1e30
M_SCALE = M_DK ** -0.5
A_SCALE = A_HEAD_DIM ** -0.5

HEAD_W = 2 * M_DK + 2 * M_DV
M_PROJ_W = M_HEADS * HEAD_W
A_PROJ_W = A_WIDTH + 2 * A_KV_WIDTH
PROJ_W = M_PROJ_W + A_PROJ_W
GATE_W = 128

V7X_VMEM_LIMIT = 56 * 1024 * 1024

NT_DIMS = (((1,), (1,)), ((), ()))


def _params(sem, vmem=V7X_VMEM_LIMIT):
    return pltpu.CompilerParams(dimension_semantics=sem, vmem_limit_bytes=vmem)


def _row_tile(m, cap):
    t = min(m, cap)
    while m % t:
        t //= 2
    return t


def _dot(a, b):
    return jnp.dot(a, b, preferred_element_type=F32)


def _dot_nt(a, b):
    return lax.dot_general(a, b, NT_DIMS, preferred_element_type=F32)


def _t5_bucket(rel):
    half = N_BUCKETS // 2
    exact = half // 2
    n = np.abs(rel)
    large = exact + (np.log(np.maximum(n, 1) / exact) / np.log(MAX_DISTANCE / exact) * (half - exact)).astype(np.int32)
    large = np.minimum(large, half - 1)
    return (rel > 0).astype(np.int32) * half + np.where(n < exact, n, large).astype(np.int32)


def _bias_kernel(rel_ref, map_ref, out_ref):
    bmap = map_ref[...]
    for h in range(A_HEADS):
        acc = jnp.zeros(bmap.shape, F32)
        for b in range(N_BUCKETS):
            acc = jnp.where(bmap == b, rel_ref[b, h], acc)
        out_ref[h] = acc


def _bias_table(rel_bias):
    lk = WINDOW + CHUNK
    rel = (np.arange(lk)[None, :] - WINDOW) - np.arange(CHUNK)[:, None]
    bmap = jnp.asarray(_t5_bucket(rel), jnp.int32)
    return pl.pallas_call(
        _bias_kernel,
        out_shape=jax.ShapeDtypeStruct((A_HEADS, CHUNK, lk), F32),
        in_specs=[pl.BlockSpec(memory_space=pltpu.SMEM),
                  pl.BlockSpec(memory_space=pltpu.VMEM)],
        out_specs=pl.BlockSpec(memory_space=pltpu.VMEM),
        name="t5_bias_table",
    )(rel_bias, bmap)


def _rms(x, g):
    return x * lax.rsqrt(jnp.mean(x * x, axis=-1, keepdims=True) + EPS) * g


def _in_proj_kernel(x_ref, g_ref, w_ref, o_ref, gate_ref, *, sub):
    for r in range(0, x_ref.shape[0], sub):
        h = _rms(x_ref[r:r + sub, :], g_ref[...]).astype(BF16)
        acc = _dot(h, w_ref[...])
        o_ref[r:r + sub, :] = acc[:, :PROJ_W].astype(o_ref.dtype)
        gate_ref[r:r + sub, :] = acc[:, PROJ_W:]


def _in_proj(x, g_mix, w_all, layer):
    m, d = x.shape
    tm = _row_tile(m, 512)
    n = PROJ_W + GATE_W
    return pl.pallas_call(
        functools.partial(_in_proj_kernel, sub=min(tm, 256)),
        out_shape=(jax.ShapeDtypeStruct((m, PROJ_W), BF16), jax.ShapeDtypeStruct((m, GATE_W), F32)),
        grid=(m // tm,),
        in_specs=[pl.BlockSpec((tm, d), lambda i: (i, 0)),
                  pl.BlockSpec((None, 1, d), lambda i: (layer, 0, 0)),
                  pl.BlockSpec((None, d, n), lambda i: (layer, 0, 0), pipeline_mode=pl.Buffered(1))],
        out_specs=(pl.BlockSpec((tm, PROJ_W), lambda i: (i, 0)),
                   pl.BlockSpec((tm, GATE_W), lambda i: (i, 0))),
        compiler_params=_params(("parallel",)),
        name="in_proj",
    )(x, g_mix, w_all)


def _mlstm_body(head_inputs, gates, gbias_ref, gmo_ref, eye_ref, c0_ref, n0_ref, m0_ref,
                hm_ref, cout_ref, nout_ref, mout_ref, c_scr, n_scr, m_scr, *, L):
    c = pl.program_id(1)
    last = pl.num_programs(1) - 1

    @pl.when(c == 0)
    def _():
        c_scr[...] = c0_ref[0]
        n_scr[...] = n0_ref[0]
        for h in range(M_HEADS):
            m_scr[h:h + 1, :] = jnp.broadcast_to(m0_ref[0, :, h:h + 1], (1, 128))

    a_all = gates + gbias_ref[...]
    logf = jnp.minimum(a_all, 0.0) - jnp.log(1.0 + jnp.exp(-jnp.abs(a_all)))
    row = lax.broadcasted_iota(jnp.int32, (L, GATE_W), 0)
    b_all = logf
    k = 1
    while k < L:
        b_all = b_all + jnp.where(row >= k, pltpu.roll(b_all, k, axis=0), 0.0)
        k *= 2
    b_sh = pltpu.roll(b_all, GATE_W - M_HEADS, axis=1)
    a_i = a_all - b_sh
    cmax = a_i
    k = 1
    while k < L:
        cmax = jnp.maximum(cmax, jnp.where(row >= k, pltpu.roll(cmax, k, axis=0), NEG_INF))
        k *= 2
    a_t = a_i.T
    tri = lax.broadcasted_iota(jnp.int32, (L, L), 0) >= lax.broadcasted_iota(jnp.int32, (L, L), 1)
    eye = eye_ref[...]

    for h in range(M_HEADS):
        b_col = b_sh[:, h:h + 1]
        a_col = a_i[:, h:h + 1]
        a_row = a_t[h:h + 1, :]
        m_prev = m_scr[h:h + 1, 0:1]
        m_run = jnp.maximum(cmax[:, h:h + 1], m_prev)
        decay_w = jnp.exp(jnp.where(tri, a_row - m_run, NEG_INF))
        qh, kh, vh, og = head_inputs(h)
        w = decay_w * (_dot_nt(qh, kh) * M_SCALE)
        g = jnp.exp(m_prev - m_run)
        cst = c_scr[h]
        n_row = n_scr[h:h + 1, :]
        num = _dot(w.astype(BF16), vh) + g * _dot_nt(qh, cst.astype(BF16))
        den = (jnp.sum(w, axis=1, keepdims=True)
               + g * jnp.sum(qh.astype(F32) * n_row, axis=1, keepdims=True))
        inv = 1.0 / jnp.maximum(jnp.abs(den), jnp.exp(-(b_col + m_run)))
        ms = jnp.mean(num * num, axis=1, keepdims=True)
        scale = inv * lax.rsqrt(inv * inv * ms + EPS)
        hm_ref[0, :, h * M_DV:(h + 1) * M_DV] = (
            num * scale * gmo_ref[:, h * M_DV:(h + 1) * M_DV] * jax.nn.sigmoid(og)).astype(hm_ref.dtype)

        b_last = b_col[L - 1:L, :]
        m_end = b_last + m_run[L - 1:L, :]
        wk = jnp.exp(b_last + a_col - m_end)
        decay = jnp.exp(b_last + m_prev - m_end)
        vw = (vh.astype(F32) * wk).astype(BF16)
        vw_t = _dot_nt(eye, vw).astype(BF16)
        c_new = decay * cst + M_SCALE * _dot(vw_t, kh)
        n_new = decay * n_row + M_SCALE * jnp.sum(kh.astype(F32) * wk, axis=0, keepdims=True)
        c_scr[h] = c_new
        n_scr[h:h + 1, :] = n_new
        m_scr[h:h + 1, :] = jnp.broadcast_to(m_end, (1, 128))

        @pl.when(c == last)
        def _():
            cout_ref[0, h] = c_new
            nout_ref[0, h:h + 1, :] = n_new
            mout_ref[0, :, h:h + 1] = m_end


def _mlstm_kernel(p_ref, gate_ref, *rest, L):
    def head_inputs(h):
        o = h * HEAD_W
        return (p_ref[0, :, o:o + M_DK], p_ref[0, :, o + M_DK:o + 2 * M_DK],
                p_ref[0, :, o + 2 * M_DK:o + 2 * M_DK + M_DV], p_ref[0, :, o + 2 * M_DK + M_DV:o + HEAD_W].astype(F32))

    _mlstm_body(head_inputs, gate_ref[0], *rest, L=L)


def _proj_mlstm_kernel(x_ref, g_ref, w_ref, gbias_ref, gmo_ref, eye_ref, c0_ref, n0_ref, m0_ref,
                       hm_ref, ap_ref, cout_ref, nout_ref, mout_ref, c_scr, n_scr, m_scr, *, L):
    hn = _rms(x_ref[0], g_ref[...]).astype(BF16)

    def head_inputs(h):
        o = h * HEAD_W
        ph = _dot(hn, w_ref[:, o:o + HEAD_W])
        return (ph[:, :M_DK].astype(BF16), ph[:, M_DK:2 * M_DK].astype(BF16),
                ph[:, 2 * M_DK:2 * M_DK + M_DV].astype(BF16), ph[:, 2 * M_DK + M_DV:])

    _mlstm_body(head_inputs, _dot(hn, w_ref[:, PROJ_W:]), gbias_ref, gmo_ref, eye_ref, c0_ref, n0_ref, m0_ref,
                hm_ref, cout_ref, nout_ref, mout_ref, c_scr, n_scr, m_scr, L=L)
    ap_ref[0] = _dot(hn, w_ref[:, M_PROJ_W:PROJ_W]).astype(ap_ref.dtype)


def _mlstm_specs(b, layer, L):
    state = [pl.BlockSpec((1, M_HEADS, M_DV, M_DK), lambda i, c: (i, 0, 0, 0)),
             pl.BlockSpec((1, M_HEADS, M_DK), lambda i, c: (i, 0, 0)),
             pl.BlockSpec((1, 1, M_HEADS), lambda i, c: (i, 0, 0))]
    params = [pl.BlockSpec((None, 1, GATE_W), lambda i, c: (layer, 0, 0)),
              pl.BlockSpec((None, 1, M_WIDTH), lambda i, c: (layer, 0, 0)),
              pl.BlockSpec((M_DV, M_DV), lambda i, c: (0, 0))]
    state_shapes = [jax.ShapeDtypeStruct((b, M_HEADS, M_DV, M_DK), F32),
                    jax.ShapeDtypeStruct((b, M_HEADS, M_DK), F32),
                    jax.ShapeDtypeStruct((b, 1, M_HEADS), F32)]
    scratch = [pltpu.VMEM((M_HEADS, M_DV, M_DK), F32), pltpu.VMEM((M_HEADS, M_DK), F32), pltpu.VMEM((M_HEADS, 128), F32)]
    return params + state, state, state_shapes, scratch


def _mlstm(proj, gates, gbias, g_mo, eye, c0, n0, m0, layer, L):
    b, s, _ = proj.shape
    ins, state_out, state_shapes, scratch = _mlstm_specs(b, layer, L)
    return pl.pallas_call(
        functools.partial(_mlstm_kernel, L=L),
        out_shape=[jax.ShapeDtypeStruct((b, s, M_WIDTH), BF16)] + state_shapes,
        grid=(b, s // L),
        in_specs=[pl.BlockSpec((1, L, M_PROJ_W), lambda i, c: (i, c, 0)),
                  pl.BlockSpec((1, L, GATE_W), lambda i, c: (i, c, 0))] + ins,
        out_specs=[pl.BlockSpec((1, L, M_WIDTH), lambda i, c: (i, c, 0))] + state_out,
        scratch_shapes=scratch,
        compiler_params=_params(("arbitrary", "arbitrary")),
        name="mlstm",
    )(proj, gates, gbias, g_mo, eye, c0, n0, m0)


def _proj_mlstm(x, g_mix, w_all, gbias, g_mo, eye, c0, n0, m0, layer, L):
    b, s, d = x.shape
    ins, state_out, state_shapes, scratch = _mlstm_specs(b, layer, L)
    return pl.pallas_call(
        functools.partial(_proj_mlstm_kernel, L=L),
        out_shape=[jax.ShapeDtypeStruct((b, s, M_WIDTH), BF16), jax.ShapeDtypeStruct((b, s, A_PROJ_W), BF16)] + state_shapes,
        grid=(b, s // L),
        in_specs=[pl.BlockSpec((1, L, d), lambda i, c: (i, c, 0)),
                  pl.BlockSpec((None, 1, d), lambda i, c: (layer, 0, 0)),
                  pl.BlockSpec((None, d, PROJ_W + GATE_W), lambda i, c: (layer, 0, 0), pipeline_mode=pl.Buffered(1))] + ins,
        out_specs=[pl.BlockSpec((1, L, M_WIDTH), lambda i, c: (i, c, 0)),
                   pl.BlockSpec((1, L, A_PROJ_W), lambda i, c: (i, c, 0))] + state_out,
        scratch_shapes=scratch,
        compiler_params=_params(("arbitrary", "arbitrary")),
        name="proj_mlstm",
    )(x, g_mix, w_all, gbias, g_mo, eye, c0, n0, m0)


def _swa_body(c, is_last, q_ref, k_ref, v_ref, ck_ref, cv_ref, gq_ref, gk_ref, gao_ref, bias_ref, sink_ref,
              e4_ref, eye_ref, cko_ref, cvo_ref, khist, vhist, write_ha, *, L, CS, cache_len, side_work=None):
    T = CS * L
    lk = WINDOW + L
    gl = A_GROUP * L
    kv_lane = lax.broadcasted_iota(jnp.int32, (1, A_KV_WIDTH), 1) // A_HEAD_DIM

    def lane_masked(kn):
        knb = kn.astype(BF16)
        return [jnp.where(kv_lane == kv, knb, 0.0) for kv in range(A_KV_HEADS)]

    @pl.when(c == 0)
    def _():
        khist[...] = ck_ref[0]
        vhist[...] = cv_ref[0]

    q = q_ref[0].astype(F32)
    k = k_ref[0].astype(F32)
    sq = jnp.concatenate([q[:, g * A_KV_WIDTH:(g + 1) * A_KV_WIDTH] for g in range(A_GROUP)] + [k], axis=0)
    sq = sq * sq
    sq_hi = sq.astype(BF16)
    sq_lo = (sq - sq_hi.astype(F32)).astype(BF16)
    ss = _dot(jnp.concatenate([sq_hi, sq_lo], axis=0), e4_ref[...])
    rinv = lax.rsqrt((ss[:5 * T] + ss[5 * T:]) * (1.0 / A_HEAD_DIM) + EPS)
    qn = [(q[:, g * A_KV_WIDTH:(g + 1) * A_KV_WIDTH] * rinv[g * T:(g + 1) * T, :]
           * gq_ref[:, g * A_KV_WIDTH:(g + 1) * A_KV_WIDTH]).astype(BF16) for g in range(A_GROUP)]
    kn = k * rinv[A_GROUP * T:, :] * gk_ref[...]

    k_all = jnp.concatenate([khist[...], kn], axis=0)
    v_all = jnp.concatenate([vhist[...], v_ref[0].astype(F32)], axis=0)
    vb_all = v_all.astype(BF16)

    for j in range(CS):
        rows = slice(j * L, j * L + lk)
        qst = jnp.concatenate([qn[g][j * L:(j + 1) * L, :] for g in range(A_GROUP)], axis=0)
        km = jnp.concatenate(lane_masked(k_all[rows]), axis=0)
        s_all = _dot_nt(km, qst)
        v_t = _dot_nt(eye_ref[...], vb_all[rows]).astype(BF16)
        var = jnp.minimum((c * CS + j) * L, WINDOW - cache_len) // L
        o_t = []
        if side_work is not None:
            side_work(j, CS)
        for kv in range(A_KV_HEADS):
            s = s_all[kv * lk:(kv + 1) * lk, :] + bias_ref[var, kv]
            sk = sink_ref[kv]
            mx = jnp.maximum(jnp.max(s, axis=0, keepdims=True), sk)
            p = jnp.exp(s - mx)
            den = jnp.sum(p, axis=0, keepdims=True) + jnp.exp(sk - mx)
            o = _dot(v_t[kv * A_HEAD_DIM:(kv + 1) * A_HEAD_DIM, :], p.astype(BF16))
            o_t.append(o * (1.0 / den))
        o_all = jnp.concatenate(o_t, axis=0).T
        ha = jnp.concatenate([o_all[g * L:(g + 1) * L, :] for g in range(A_GROUP)], axis=1)
        write_ha(j, _rms(ha, gao_ref[...]).astype(BF16))

    k_win = k_all[T:T + WINDOW]
    v_win = v_all[T:T + WINDOW]
    khist[...] = k_win
    vhist[...] = v_win

    @pl.when(is_last)
    def _():
        cko_ref[0] = k_win
        cvo_ref[0] = v_win


def _swa_kernel(*refs, L, CS, cache_len):
    ins, (ha_ref, cko_ref, cvo_ref, khist, vhist) = refs[:12], refs[12:]

    def write_ha(j, rows):
        ha_ref[0, j * L:(j + 1) * L, :] = rows

    _swa_body(pl.program_id(1), pl.program_id(1) == pl.num_programs(1) - 1, *ins, cko_ref, cvo_ref, khist, vhist,
              write_ha, L=L, CS=CS, cache_len=cache_len)


def _swa_out_kernel(*refs, L, CS, nc, n):
    ins, (x_ref, hm_ref, wm_ref, wa_ref), (o_ref, cko_ref, cvo_ref, khist, vhist, ha_prev) = (
        refs[:12], refs[12:16], refs[16:])
    s = pl.program_id(0)
    c = jnp.minimum(s, n - 1) % nc

    @pl.when(s == 0)
    def _():
        ha_prev[...] = jnp.zeros(ha_prev.shape, ha_prev.dtype)

    hm = hm_ref[...]
    ha_old = ha_prev[...]

    def project(j, parts):
        w = o_ref.shape[1] // parts
        cols = slice(j * w, (j + 1) * w)
        o_ref[:, cols] = x_ref[:, cols] + _dot(hm, wm_ref[:, cols]) + _dot(ha_old, wa_ref[:, cols])

    def write_ha(j, rows):
        ha_prev[j * L:(j + 1) * L, :] = rows

    _swa_body(c, jnp.logical_and(s < n, c == nc - 1), *ins, cko_ref, cvo_ref, khist, vhist,
              write_ha, L=L, CS=CS, cache_len=0, side_work=project)


def _swa_out(proj, col0, x, hm, w_out, g_q, g_k, g_ao, bias, sink, consts, layer, L, CS):
    b, s, _ = proj.shape
    m, d = x.shape
    T = CS * L
    nc = s // T
    n = b * nc
    lk = WINDOW + L
    gl = A_GROUP * L
    nvar = bias.shape[0]
    q_blk, kv_blk = col0 // A_WIDTH, (col0 + A_WIDTH) // A_KV_WIDTH
    e4, eye = consts
    zk = jnp.zeros((b, WINDOW, A_KV_WIDTH), F32)
    cur = lambda s_: jnp.minimum(s_, n - 1)
    prev = lambda s_: jnp.maximum(s_ - 1, 0)
    full = lambda *shape: pl.BlockSpec(shape, lambda s_: (0,) * len(shape))
    out_shape = (jax.ShapeDtypeStruct((m, d), F32),
                 jax.ShapeDtypeStruct((b, WINDOW, A_KV_WIDTH), F32),
                 jax.ShapeDtypeStruct((b, WINDOW, A_KV_WIDTH), F32))
    return pl.pallas_call(
        functools.partial(_swa_out_kernel, L=L, CS=CS, nc=nc, n=n),
        out_shape=out_shape,
        grid=(n + 1,),
        in_specs=[pl.BlockSpec((1, T, A_WIDTH), lambda s_: (cur(s_) // nc, cur(s_) % nc, q_blk)),
                  pl.BlockSpec((1, T, A_KV_WIDTH), lambda s_: (cur(s_) // nc, cur(s_) % nc, kv_blk)),
                  pl.BlockSpec((1, T, A_KV_WIDTH), lambda s_: (cur(s_) // nc, cur(s_) % nc, kv_blk + 1)),
                  pl.BlockSpec((1, WINDOW, A_KV_WIDTH), lambda s_: (cur(s_) // nc, 0, 0)),
                  pl.BlockSpec((1, WINDOW, A_KV_WIDTH), lambda s_: (cur(s_) // nc, 0, 0)),
                  pl.BlockSpec((None, 1, A_WIDTH), lambda s_: (layer, 0, 0)),
                  pl.BlockSpec((None, 1, A_KV_WIDTH), lambda s_: (layer, 0, 0)),
                  pl.BlockSpec((None, 1, A_WIDTH), lambda s_: (layer, 0, 0)),
                  full(nvar, A_KV_HEADS, lk, gl),
                  pl.BlockSpec((None, A_KV_HEADS, 1, gl), lambda s_: (layer, 0, 0, 0)),
                  full(A_KV_WIDTH, A_KV_WIDTH),
                  full(A_KV_WIDTH, A_KV_WIDTH),
                  pl.BlockSpec((T, d), lambda s_: (prev(s_), 0)),
                  pl.BlockSpec((T, M_WIDTH), lambda s_: (prev(s_), 0)),
                  pl.BlockSpec((None, M_WIDTH, d), lambda s_: (layer, 0, 0), pipeline_mode=pl.Buffered(1)),
                  pl.BlockSpec((None, A_WIDTH, d), lambda s_: (layer, 1, 0), pipeline_mode=pl.Buffered(1))],
        out_specs=(pl.BlockSpec((T, d), lambda s_: (prev(s_), 0)),
                   pl.BlockSpec((1, WINDOW, A_KV_WIDTH), lambda s_: (cur(s_) // nc, 0, 0)),
                   pl.BlockSpec((1, WINDOW, A_KV_WIDTH), lambda s_: (cur(s_) // nc, 0, 0))),
        scratch_shapes=[pltpu.VMEM((WINDOW, A_KV_WIDTH), F32),
                        pltpu.VMEM((WINDOW, A_KV_WIDTH), F32),
                        pltpu.VMEM((T, A_WIDTH), BF16)],
        compiler_params=_params(("arbitrary",)),
        name="swa_out",
    )(proj, proj, proj, zk, zk, g_q, g_k, g_ao, bias, sink, e4, eye, x, hm, w_out, w_out)


def _swa(proj, col0, ck, cv, g_q, g_k, g_ao, bias, sink, consts, layer, L, CS, cache_len):
    b, s, _ = proj.shape
    T = CS * L
    lk = WINDOW + L
    gl = A_GROUP * L
    nvar = bias.shape[0]
    q_blk, kv_blk = col0 // A_WIDTH, (col0 + A_WIDTH) // A_KV_WIDTH
    e4, eye = consts
    out_shape = (jax.ShapeDtypeStruct((b, s, A_WIDTH), BF16),
                 jax.ShapeDtypeStruct((b, WINDOW, A_KV_WIDTH), F32),
                 jax.ShapeDtypeStruct((b, WINDOW, A_KV_WIDTH), F32))
    full = lambda *shape: pl.BlockSpec(shape, lambda i, c: (0,) * len(shape))
    return pl.pallas_call(
        functools.partial(_swa_kernel, L=L, CS=CS, cache_len=cache_len),
        out_shape=out_shape,
        grid=(b, s // T),
        in_specs=[pl.BlockSpec((1, T, A_WIDTH), lambda i, c: (i, c, q_blk)),
                  pl.BlockSpec((1, T, A_KV_WIDTH), lambda i, c: (i, c, kv_blk)),
                  pl.BlockSpec((1, T, A_KV_WIDTH), lambda i, c: (i, c, kv_blk + 1)),
                  pl.BlockSpec((1, WINDOW, A_KV_WIDTH), lambda i, c: (i, 0, 0)),
                  pl.BlockSpec((1, WINDOW, A_KV_WIDTH), lambda i, c: (i, 0, 0)),
                  pl.BlockSpec((None, 1, A_WIDTH), lambda i, c: (layer, 0, 0)),
                  pl.BlockSpec((None, 1, A_KV_WIDTH), lambda i, c: (layer, 0, 0)),
                  pl.BlockSpec((None, 1, A_WIDTH), lambda i, c: (layer, 0, 0)),
                  full(nvar, A_KV_HEADS, lk, gl),
                  pl.BlockSpec((None, A_KV_HEADS, 1, gl), lambda i, c: (layer, 0, 0, 0)),
                  full(A_KV_WIDTH, A_KV_WIDTH),
                  full(A_KV_WIDTH, A_KV_WIDTH)],
        out_specs=(pl.BlockSpec((1, T, A_WIDTH), lambda i, c: (i, c, 0)),
                   pl.BlockSpec((1, WINDOW, A_KV_WIDTH), lambda i, c: (i, 0, 0)),
                   pl.BlockSpec((1, WINDOW, A_KV_WIDTH), lambda i, c: (i, 0, 0))),
        scratch_shapes=[pltpu.VMEM((WINDOW, A_KV_WIDTH), F32),
                        pltpu.VMEM((WINDOW, A_KV_WIDTH), F32)],
        compiler_params=_params(("arbitrary", "arbitrary")),
        name="swa",
    )(proj, proj, proj, ck, cv, g_q, g_k, g_ao, bias, sink, e4, eye)


def _out_proj_kernel(x_ref, hm_ref, ha_ref, wm_ref, wa_ref, o_ref):
    o_ref[...] = x_ref[...] + _dot(hm_ref[...], wm_ref[...]) + _dot(ha_ref[...], wa_ref[...])


def _out_proj(x, hm, ha, w_out, layer):
    m, d = x.shape
    tm = _row_tile(m, 512)
    return pl.pallas_call(
        _out_proj_kernel,
        out_shape=jax.ShapeDtypeStruct((m, d), F32),
        grid=(m // tm,),
        in_specs=[pl.BlockSpec((tm, d), lambda i: (i, 0)),
                  pl.BlockSpec((tm, M_WIDTH), lambda i: (i, 0)),
                  pl.BlockSpec((tm, A_WIDTH), lambda i: (i, 0)),
                  pl.BlockSpec((None, M_WIDTH, d), lambda i: (layer, 0, 0)),
                  pl.BlockSpec((None, A_WIDTH, d), lambda i: (layer, 1, 0))],
        out_specs=pl.BlockSpec((tm, d), lambda i: (i, 0)),
        compiler_params=_params(("parallel",)),
        name="out_proj",
    )(x, hm, ha, w_out, w_out)


def _ffn_kernel(x_ref, g_ref, wu_ref, wd_ref, o_ref, h_scr):
    f = pl.program_id(1)

    @pl.when(f == 0)
    def _():
        x = x_ref[...]
        h_scr[...] = _rms(x, g_ref[...]).astype(BF16)
        o_ref[...] = x

    u = _dot(h_scr[...], wu_ref[...])
    a = jnp.square(jnp.maximum(u, 0.0)).astype(BF16)
    o_ref[...] += _dot(a, wd_ref[...])


def _ffn(x, g_ffn, w_up, w_down, layer):
    m, d = x.shape
    d_ff = w_up.shape[-1]
    tm = _row_tile(m, 512)
    tf = 1024
    return pl.pallas_call(
        _ffn_kernel,
        out_shape=jax.ShapeDtypeStruct((m, d), F32),
        grid=(m // tm, d_ff // tf),
        in_specs=[pl.BlockSpec((tm, d), lambda i, f: (i, 0)),
                  pl.BlockSpec((None, 1, d), lambda i, f: (layer, 0, 0)),
                  pl.BlockSpec((None, d, tf), lambda i, f: (layer, 0, f)),
                  pl.BlockSpec((None, tf, d), lambda i, f: (layer, f, 0))],
        out_specs=pl.BlockSpec((tm, d), lambda i, f: (i, 0)),
        scratch_shapes=[pltpu.VMEM((tm, d), BF16)],
        compiler_params=_params(("parallel", "arbitrary")),
        name="ffn",
    )(x, g_ffn, w_up, w_down)


def _layer(x, wts, layer, lm, la, cs, bias, sink, ck, cv, c0, n0, m0, cache_len):
    b, s, d = x.shape
    x2 = x.reshape(b * s, d)
    eye = wts["consts"][1]
    if ck is None:
        hm, aproj, c_new, n_new, m_new = _proj_mlstm(x, wts["g_mix"], wts["w_all"], wts["gbias"], wts["g_mo"], eye,
                                                     c0, n0, m0, layer, lm)
        x2, k_new, v_new = _swa_out(aproj, 0, x2, hm.reshape(b * s, M_WIDTH), wts["w_out"], wts["g_q"], wts["g_k"],
                                    wts["g_ao"], bias, sink, wts["consts"], layer, la, cs)
    else:
        proj, gates = _in_proj(x2, wts["g_mix"], wts["w_all"], layer)
        proj = proj.reshape(b, s, PROJ_W)
        hm, c_new, n_new, m_new = _mlstm(proj, gates.reshape(b, s, GATE_W), wts["gbias"], wts["g_mo"], eye,
                                         c0, n0, m0, layer, lm)
        ha, k_new, v_new = _swa(proj, M_PROJ_W, ck, cv, wts["g_q"], wts["g_k"], wts["g_ao"], bias, sink, wts["consts"],
                                layer, la, cs, cache_len)
        x2 = _out_proj(x2, hm.reshape(b * s, M_WIDTH), ha.reshape(b * s, A_WIDTH), wts["w_out"], layer)
    x2 = _ffn(x2, wts["g_ffn"], wts["w_up"], wts["w_down"], layer)
    return x2.reshape(b, s, d), k_new, v_new, c_new, n_new, m_new


def _w_in_columns():
    o = np.cumsum((0, M_HEADS * M_DK, M_HEADS * M_DK, M_WIDTH, M_WIDTH, M_HEADS, M_HEADS, A_WIDTH, A_KV_WIDTH, A_KV_WIDTH))
    mq, mk, mv, mo, mi, mf, aq, ak, av = (int(v) for v in o[:9])
    runs = []
    for h in range(M_HEADS):
        runs += [(mq + h * M_DK, M_DK), (mk + h * M_DK, M_DK), (mv + h * M_DV, M_DV), (mo + h * M_DV, M_DV)]
    for g in range(A_GROUP):
        runs += [(aq + (kv * A_GROUP + g) * A_HEAD_DIM, A_HEAD_DIM) for kv in range(A_KV_HEADS)]
    return runs + [(ak, A_KV_WIDTH), (av, A_KV_WIDTH), (mi, M_HEADS), (mf, M_HEADS)]


def _w_in_layout_kernel(w_ref, o_ref):
    dst = 0
    for src, width in _w_in_columns():
        o_ref[:, dst:dst + width] = w_ref[:, src:src + width].astype(o_ref.dtype)
        dst += width
    o_ref[:, dst:] = jnp.zeros((o_ref.shape[0], o_ref.shape[1] - dst), o_ref.dtype)


def _w_in_layout(w_in):
    depth, d, d_in = w_in.shape
    tk = _row_tile(d, 256)
    return pl.pallas_call(
        _w_in_layout_kernel,
        out_shape=jax.ShapeDtypeStruct((depth, d, PROJ_W + GATE_W), BF16),
        grid=(depth, d // tk),
        in_specs=[pl.BlockSpec((None, tk, d_in), lambda l, i: (l, i, 0))],
        out_specs=pl.BlockSpec((None, tk, PROJ_W + GATE_W), lambda l, i: (l, i, 0)),
        compiler_params=_params(("parallel", "parallel")),
        name="w_in_layout",
    )(w_in)


def _head_consts():
    hd = np.arange(A_KV_WIDTH) // A_HEAD_DIM
    e4 = (hd[:, None] == hd[None, :]).astype(np.float32)
    eye = np.eye(A_KV_WIDTH, dtype=np.float32)
    return tuple(jnp.asarray(a, BF16) for a in (e4, eye))


def kernel(x_prompt, x_sample, cache_k, cache_v, state_C, state_n, state_m, rel_bias, g_mix, w_in, b_i, b_f, g_q, g_k, sinks, g_mo, g_ao, w_out, g_ffn, w_up, w_down):
    depth = w_in.shape[0]
    bp, sp, d = x_prompt.shape
    bs, ls, _ = x_sample.shape
    n_win = cache_k.shape[2]
    assert sp % CHUNK == 0 and n_win == WINDOW and ls % 16 == 0 and ls <= CHUNK
    lm = M_CHUNK if sp % M_CHUNK == 0 else CHUNK

    w_all = _w_in_layout(w_in)
    gbias = jnp.concatenate([b_i, b_f, jnp.zeros((depth, GATE_W - 2 * M_HEADS), F32)], axis=-1)

    def regroup(a):
        rest = a.shape[2:]
        a = a.reshape(depth, A_KV_HEADS, A_GROUP, A_HEAD_DIM, *rest)
        return jnp.swapaxes(a, 1, 2).reshape(depth, A_WIDTH, *rest)

    w_out_b = jnp.concatenate([w_out[:, :M_WIDTH], regroup(w_out[:, M_WIDTH:])], axis=1).astype(BF16)
    wts = {
        "g_mix": g_mix.reshape(depth, 1, d), "w_all": w_all,
        "gbias": gbias.reshape(depth, 1, GATE_W), "g_mo": g_mo.reshape(depth, 1, M_WIDTH),
        "g_q": jnp.tile(g_q * A_SCALE, (1, A_HEADS)).reshape(depth, 1, A_WIDTH),
        "g_k": jnp.tile(g_k, (1, A_KV_HEADS)).reshape(depth, 1, A_KV_WIDTH),
        "g_ao": regroup(g_ao).reshape(depth, 1, A_WIDTH),
        "w_out": w_out_b, "g_ffn": g_ffn.reshape(depth, 1, d),
        "w_up": w_up.astype(BF16), "w_down": w_down.astype(BF16),
        "consts": _head_consts(),
    }

    bias = _bias_table(rel_bias)

    def bias_t(lq, cache_len):
        lk = WINDOW + lq
        t = bias[:, :lq, :lk].reshape(A_KV_HEADS, A_GROUP, lq, lk)
        t = t.transpose(0, 3, 1, 2).reshape(A_KV_HEADS, lk, A_GROUP * lq)
        n_prev = np.arange((WINDOW - cache_len) // lq + 1) * lq + cache_len
        dead = np.arange(lk)[None, :] < WINDOW - n_prev[:, None]
        return t[None] + jnp.asarray(np.where(dead, NEG_INF, 0.0)[:, None, :, None], F32)

    def sink_rows(lq):
        return jnp.repeat(sinks.reshape(depth, A_KV_HEADS, 1, A_GROUP), lq, axis=-1)

    zc =jnp.zeros((bp, M_HEADS, M_DV, M_DK), F32)
    zn = jnp.zeros((bp, M_HEADS, M_DK), F32)
    zm = jnp.zeros((bp, 1, M_HEADS), F32)

    xp, xs = x_prompt, x_sample
    bias_p, bias_s, sink_p, sink_s = bias_t(CHUNK, 0), bias_t(ls, WINDOW), sink_rows(CHUNK), sink_rows(ls)
    cs = A_CHUNKS_PER_STEP if sp % (A_CHUNKS_PER_STEP * CHUNK) == 0 else 1
    outs_p, outs_s = [], []
    for l in range(depth):
        xp, *st = _layer(xp, wts, l, lm, CHUNK, cs, bias_p, sink_p, None, None, zc, zn, zm, 0)
        outs_p.append(st)
        xs, *st = _layer(xs, wts, l, ls, ls, 1, bias_s, sink_s,
                         cache_k[l].reshape(bs, n_win, A_KV_WIDTH), cache_v[l].reshape(bs, n_win, A_KV_WIDTH),
                         state_C[l], state_n[l], state_m[l].reshape(bs, 1, M_HEADS), WINDOW)
        outs_s.append(st)

    def stack(outs, b):
        k, v, c, n, m = (jnp.stack([o_[i] for o_ in outs]) for i in range(5))
        return (k.reshape(depth, b, WINDOW, A_KV_HEADS, A_HEAD_DIM), v.reshape(depth, b, WINDOW, A_KV_HEADS, A_HEAD_DIM),
                c, n, m.reshape(depth, b, M_HEADS))

    return (xp, xs) + stack(outs_p, bp) + stack(outs_s, bs)
```

```python
import functools

import jax
import jax.numpy as jnp
import numpy as np
from jax import lax
from jax.experimental import pallas as pl
from jax.experimental.pallas import tpu as pltpu

F32 = jnp.float32
BF16 = jnp.bfloat16

CHUNK = 64
M_CHUNK = 256
A_CHUNKS_PER_STEP = 4
M_HEADS = 4
M_DK = 128
M_DV = 256
M_WIDTH = M_HEADS * M_DV
A_HEADS = 16
A_KV_HEADS = 4
A_HEAD_DIM = 64
A_GROUP = A_HEADS // A_KV_HEADS
A_WIDTH = A_HEADS * A_HEAD_DIM
A_KV_WIDTH = A_KV_HEADS * A_HEAD_DIM
WINDOW = 128
N_BUCKETS = 32
MAX_DISTANCE = 128
EPS = 1e-6
NEG_INF = -1e30
M_SCALE = M_DK ** -0.5
A_SCALE = A_HEAD_DIM ** -0.5

HEAD_W = 2 * M_DK + 2 * M_DV
M_PROJ_W = M_HEADS * HEAD_W
A_PROJ_W = A_WIDTH + 2 * A_KV_WIDTH
PROJ_W = M_PROJ_W + A_PROJ_W
GATE_W = 128

V7X_VMEM_LIMIT = 56 * 1024 * 1024

NT_DIMS = (((1,), (1,)), ((), ()))


def _params(sem, vmem=V7X_VMEM_LIMIT):
    return pltpu.CompilerParams(dimension_semantics=sem, vmem_limit_bytes=vmem)


def _row_tile(m, cap):
    t = min(m, cap)
    while m % t:
        t //= 2
    return t


def _dot(a, b):
    return jnp.dot(a, b, preferred_element_type=F32)


def _dot_nt(a, b):
    return lax.dot_general(a, b, NT_DIMS, preferred_element_type=F32)


def _t5_bucket(rel):
    half = N_BUCKETS // 2
    exact = half // 2
    n = np.abs(rel)
    large = exact + (np.log(np.maximum(n, 1) / exact) / np.log(MAX_DISTANCE / exact) * (half - exact)).astype(np.int32)
    large = np.minimum(large, half - 1)
    return (rel > 0).astype(np.int32) * half + np.where(n < exact, n, large).astype(np.int32)


def _bias_kernel(rel_ref, map_ref, out_ref):
    bmap = map_ref[...]
    for h in range(A_HEADS):
        acc = jnp.zeros(bmap.shape, F32)
        for b in range(N_BUCKETS):
            acc = jnp.where(bmap == b, rel_ref[b, h], acc)
        out_ref[h] = acc


def _bias_table(rel_bias):
    lk = WINDOW + CHUNK
    rel = (np.arange(lk)[None, :] - WINDOW) - np.arange(CHUNK)[:, None]
    bmap = jnp.asarray(_t5_bucket(rel), jnp.int32)
    return pl.pallas_call(
        _bias_kernel,
        out_shape=jax.ShapeDtypeStruct((A_HEADS, CHUNK, lk), F32),
        in_specs=[pl.BlockSpec(memory_space=pltpu.SMEM),
                  pl.BlockSpec(memory_space=pltpu.VMEM)],
        out_specs=pl.BlockSpec(memory_space=pltpu.VMEM),
        name="t5_bias_table",
    )(rel_bias, bmap)


def _rms(x, g):
    return x * lax.rsqrt(jnp.mean(x * x, axis=-1, keepdims=True) + EPS) * g


def _in_proj_kernel(x_ref, g_ref, w_ref, o_ref, gate_ref, *, sub):
    for r in range(0, x_ref.shape[0], sub):
        h = _rms(x_ref[r:r + sub, :], g_ref[...]).astype(BF16)
        acc = _dot(h, w_ref[...])
        o_ref[r:r + sub, :] = acc[:, :PROJ_W].astype(o_ref.dtype)
        gate_ref[r:r + sub, :] = acc[:, PROJ_W:]


def _in_proj(x, g_mix, w_all, layer):
    m, d = x.shape
    tm = _row_tile(m, 512)
    n = PROJ_W + GATE_W
    return pl.pallas_call(
        functools.partial(_in_proj_kernel, sub=min(tm, 256)),
        out_shape=(jax.ShapeDtypeStruct((m, PROJ_W), BF16), jax.ShapeDtypeStruct((m, GATE_W), F32)),
        grid=(m // tm,),
        in_specs=[pl.BlockSpec((tm, d), lambda i: (i, 0)),
                  pl.BlockSpec((None, 1, d), lambda i: (layer, 0, 0)),
                  pl.BlockSpec((None, d, n), lambda i: (layer, 0, 0), pipeline_mode=pl.Buffered(1))],
        out_specs=(pl.BlockSpec((tm, PROJ_W), lambda i: (i, 0)),
                   pl.BlockSpec((tm, GATE_W), lambda i: (i, 0))),
        compiler_params=_params(("parallel",)),
        name="in_proj",
    )(x, g_mix, w_all)


def _mlstm_body(head_inputs, gates, gbias_ref, gmo_ref, eye_ref, c0_ref, n0_ref, m0_ref,
                hm_ref, cout_ref, nout_ref, mout_ref, c_scr, n_scr, m_scr, *, L):
    c = pl.program_id(1)
    last = pl.num_programs(1) - 1

    @pl.when(c == 0)
    def _():
        c_scr[...] = c0_ref[0]
        n_scr[...] = n0_ref[0]
        for h in range(M_HEADS):
            m_scr[h:h + 1, :] = jnp.broadcast_to(m0_ref[0, :, h:h + 1], (1, 128))

    a_all = gates + gbias_ref[...]
    logf = jnp.minimum(a_all, 0.0) - jnp.log(1.0 + jnp.exp(-jnp.abs(a_all)))
    row = lax.broadcasted_iota(jnp.int32, (L, GATE_W), 0)
    b_all = logf
    k = 1
    while k < L:
        b_all = b_all + jnp.where(row >= k, pltpu.roll(b_all, k, axis=0), 0.0)
        k *= 2
    b_sh = pltpu.roll(b_all, GATE_W - M_HEADS, axis=1)
    a_i = a_all - b_sh
    cmax = a_i
    k = 1
    while k < L:
        cmax = jnp.maximum(cmax, jnp.where(row >= k, pltpu.roll(cmax, k, axis=0), NEG_INF))
        k *= 2
    a_t = a_i.T
    tri = lax.broadcasted_iota(jnp.int32, (L, L), 0) >= lax.broadcasted_iota(jnp.int32, (L, L), 1)
    eye = eye_ref[...]

    for h in range(M_HEADS):
        b_col = b_sh[:, h:h + 1]
        a_col = a_i[:, h:h + 1]
        a_row = a_t[h:h + 1, :]
        m_prev = m_scr[h:h + 1, 0:1]
        m_run = jnp.maximum(cmax[:, h:h + 1], m_prev)
        decay_w = jnp.exp(jnp.where(tri, a_row - m_run, NEG_INF))
        qh, kh, vh, og = head_inputs(h)
        w = decay_w * (_dot_nt(qh, kh) * M_SCALE)
        g = jnp.exp(m_prev - m_run)
        cst = c_scr[h]
        n_row = n_scr[h:h + 1, :]
        num = _dot(w.astype(BF16), vh) + g * _dot_nt(qh, cst.astype(BF16))
        den = (jnp.sum(w, axis=1, keepdims=True)
               + g * jnp.sum(qh.astype(F32) * n_row, axis=1, keepdims=True))
        inv = 1.0 / jnp.maximum(jnp.abs(den), jnp.exp(-(b_col + m_run)))
        ms = jnp.mean(num * num, axis=1, keepdims=True)
        scale = inv * lax.rsqrt(inv * inv * ms + EPS)
        hm_ref[0, :, h * M_DV:(h + 1) * M_DV] = (
            num * scale * gmo_ref[:, h * M_DV:(h + 1) * M_DV] * jax.nn.sigmoid(og)).astype(hm_ref.dtype)

        b_last = b_col[L - 1:L, :]
        m_end = b_last + m_run[L - 1:L, :]
        wk = jnp.exp(b_last + a_col - m_end)
        decay = jnp.exp(b_last + m_prev - m_end)
        vw = (vh.astype(F32) * wk).astype(BF16)
        vw_t = _dot_nt(eye, vw).astype(BF16)
        c_new = decay * cst + M_SCALE * _dot(vw_t, kh)
        n_new = decay * n_row + M_SCALE * jnp.sum(kh.astype(F32) * wk, axis=0, keepdims=True)
        c_scr[h] = c_new
        n_scr[h:h + 1, :] = n_new
        m_scr[h:h + 1, :] = jnp.broadcast_to(m_end, (1, 128))

        @pl.when(c == last)
        def _():
            cout_ref[0, h] = c_new
            nout_ref[0, h:h + 1, :] = n_new
            mout_ref[0, :, h:h + 1] = m_end


def _mlstm_kernel(p_ref, gate_ref, *rest, L):
    def head_inputs(h):
        o = h * HEAD_W
        return (p_ref[0, :, o:o + M_DK], p_ref[0, :, o + M_DK:o + 2 * M_DK],
                p_ref[0, :, o + 2 * M_DK:o + 2 * M_DK + M_DV], p_ref[0, :, o + 2 * M_DK + M_DV:o + HEAD_W].astype(F32))

    _mlstm_body(head_inputs, gate_ref[0], *rest, L=L)


def _proj_mlstm_kernel(x_ref, g_ref, w_ref, gbias_ref, gmo_ref, eye_ref, c0_ref, n0_ref, m0_ref,
                       hm_ref, ap_ref, cout_ref, nout_ref, mout_ref, c_scr, n_scr, m_scr, *, L):
    hn = _rms(x_ref[0], g_ref[...]).astype(BF16)

    def head_inputs(h):
        o = h * HEAD_W
        ph = _dot(hn, w_ref[:, o:o + HEAD_W])
        return (ph[:, :M_DK].astype(BF16), ph[:, M_DK:2 * M_DK].astype(BF16),
                ph[:, 2 * M_DK:2 * M_DK + M_DV].astype(BF16), ph[:, 2 * M_DK + M_DV:])

    _mlstm_body(head_inputs, _dot(hn, w_ref[:, PROJ_W:]), gbias_ref, gmo_ref, eye_ref, c0_ref, n0_ref, m0_ref,
                hm_ref, cout_ref, nout_ref, mout_ref, c_scr, n_scr, m_scr, L=L)
    ap_ref[0] = _dot(hn, w_ref[:, M_PROJ_W:PROJ_W]).astype(ap_ref.dtype)


def _mlstm_specs(b, layer, L):
    state = [pl.BlockSpec((1, M_HEADS, M_DV, M_DK), lambda i, c: (i, 0, 0, 0)),
             pl.BlockSpec((1, M_HEADS, M_DK), lambda i, c: (i, 0, 0)),
             pl.BlockSpec((1, 1, M_HEADS), lambda i, c: (i, 0, 0))]
    params = [pl.BlockSpec((None, 1, GATE_W), lambda i, c: (layer, 0, 0)),
              pl.BlockSpec((None, 1, M_WIDTH), lambda i, c: (layer, 0, 0)),
              pl.BlockSpec((M_DV, M_DV), lambda i, c: (0, 0))]
    state_shapes = [jax.ShapeDtypeStruct((b, M_HEADS, M_DV, M_DK), F32),
                    jax.ShapeDtypeStruct((b, M_HEADS, M_DK), F32),
                    jax.ShapeDtypeStruct((b, 1, M_HEADS), F32)]
    scratch = [pltpu.VMEM((M_HEADS, M_DV, M_DK), F32), pltpu.VMEM((M_HEADS, M_DK), F32), pltpu.VMEM((M_HEADS, 128), F32)]
    return params + state, state, state_shapes, scratch


def _mlstm(proj, gates, gbias, g_mo, eye, c0, n0, m0, layer, L):
    b, s, _ = proj.shape
    ins, state_out, state_shapes, scratch = _mlstm_specs(b, layer, L)
    return pl.pallas_call(
        functools.partial(_mlstm_kernel, L=L),
        out_shape=[jax.ShapeDtypeStruct((b, s, M_WIDTH), BF16)] + state_shapes,
        grid=(b, s // L),
        in_specs=[pl.BlockSpec((1, L, M_PROJ_W), lambda i, c: (i, c, 0)),
                  pl.BlockSpec((1, L, GATE_W), lambda i, c: (i, c, 0))] + ins,
        out_specs=[pl.BlockSpec((1, L, M_WIDTH), lambda i, c: (i, c, 0))] + state_out,
        scratch_shapes=scratch,
        compiler_params=_params(("arbitrary", "arbitrary")),
        name="mlstm",
    )(proj, gates, gbias, g_mo, eye, c0, n0, m0)


def _proj_mlstm(x, g_mix, w_all, gbias, g_mo, eye, c0, n0, m0, layer, L):
    b, s, d = x.shape
    ins, state_out, state_shapes, scratch = _mlstm_specs(b, layer, L)
    return pl.pallas_call(
        functools.partial(_proj_mlstm_kernel, L=L),
        out_shape=[jax.ShapeDtypeStruct((b, s, M_WIDTH), BF16), jax.ShapeDtypeStruct((b, s, A_PROJ_W), BF16)] + state_shapes,
        grid=(b, s // L),
        in_specs=[pl.BlockSpec((1, L, d), lambda i, c: (i, c, 0)),
                  pl.BlockSpec((None, 1, d), lambda i, c: (layer, 0, 0)),
                  pl.BlockSpec((None, d, PROJ_W + GATE_W), lambda i, c: (layer, 0, 0), pipeline_mode=pl.Buffered(1))] + ins,
        out_specs=[pl.BlockSpec((1, L, M_WIDTH), lambda i, c: (i, c, 0)),
                   pl.BlockSpec((1, L, A_PROJ_W), lambda i, c: (i, c, 0))] + state_out,
        scratch_shapes=scratch,
        compiler_params=_params(("arbitrary", "arbitrary")),
        name="proj_mlstm",
    )(x, g_mix, w_all, gbias, g_mo, eye, c0, n0, m0)


def _swa_body(c, is_last, q_ref, k_ref, v_ref, ck_ref, cv_ref, gq_ref, gk_ref, gao_ref, bias_ref, sink_ref,
              e4_ref, eye_ref, cko_ref, cvo_ref, khist, vhist, write_ha, *, L, CS, cache_len, side_work=None):
    T = CS * L
    lk = WINDOW + L
    gl = A_GROUP * L
    kv_lane = lax.broadcasted_iota(jnp.int32, (1, A_KV_WIDTH), 1) // A_HEAD_DIM

    def lane_masked(kn):
        knb = kn.astype(BF16)
        return [jnp.where(kv_lane == kv, knb, 0.0) for kv in range(A_KV_HEADS)]

    @pl.when(c == 0)
    def _():
        khist[...] = ck_ref[0]
        vhist[...] = cv_ref[0]

    q = q_ref[0].astype(F32)
    k = k_ref[0].astype(F32)
    sq = jnp.concatenate([q[:, g * A_KV_WIDTH:(g + 1) * A_KV_WIDTH] for g in range(A_GROUP)] + [k], axis=0)
    sq = sq * sq
    sq_hi = sq.astype(BF16)
    sq_lo = (sq - sq_hi.astype(F32)).astype(BF16)
    ss = _dot(jnp.concatenate([sq_hi, sq_lo], axis=0), e4_ref[...])
    rinv = lax.rsqrt((ss[:5 * T] + ss[5 * T:]) * (1.0 / A_HEAD_DIM) + EPS)
    qn = [(q[:, g * A_KV_WIDTH:(g + 1) * A_KV_WIDTH] * rinv[g * T:(g + 1) * T, :]
           * gq_ref[:, g * A_KV_WIDTH:(g + 1) * A_KV_WIDTH]).astype(BF16) for g in range(A_GROUP)]
    kn = k * rinv[A_GROUP * T:, :] * gk_ref[...]

    k_all = jnp.concatenate([khist[...], kn], axis=0)
    v_all = jnp.concatenate([vhist[...], v_ref[0].astype(F32)], axis=0)
    vb_all = v_all.astype(BF16)

    for j in range(CS):
        rows = slice(j * L, j * L + lk)
        qst = jnp.concatenate([qn[g][j * L:(j + 1) * L, :] for g in range(A_GROUP)], axis=0)
        km = jnp.concatenate(lane_masked(k_all[rows]), axis=0)
        s_all = _dot_nt(km, qst)
        v_t = _dot_nt(eye_ref[...], vb_all[rows]).astype(BF16)
        var = jnp.minimum((c * CS + j) * L, WINDOW - cache_len) // L
        o_t = []
        if side_work is not None:
            side_work(j, CS)
        for kv in range(A_KV_HEADS):
            s = s_all[kv * lk:(kv + 1) * lk, :] + bias_ref[var, kv]
            sk = sink_ref[kv]
            mx = jnp.maximum(jnp.max(s, axis=0, keepdims=True), sk)
            p = jnp.exp(s - mx)
            den = jnp.sum(p, axis=0, keepdims=True) + jnp.exp(sk - mx)
            o = _dot(v_t[kv * A_HEAD_DIM:(kv + 1) * A_HEAD_DIM, :], p.astype(BF16))
            o_t.append(o * (1.0 / den))
        o_all = jnp.concatenate(o_t, axis=0).T
        ha = jnp.concatenate([o_all[g * L:(g + 1) * L, :] for g in range(A_GROUP)], axis=1)
        write_ha(j, _rms(ha, gao_ref[...]).astype(BF16))

    k_win = k_all[T:T + WINDOW]
    v_win = v_all[T:T + WINDOW]
    khist[...] = k_win
    vhist[...] = v_win

    @pl.when(is_last)
    def _():
        cko_ref[0] = k_win
        cvo_ref[0] = v_win


def _swa_kernel(*refs, L, CS, cache_len):
    ins, (ha_ref, cko_ref, cvo_ref, khist, vhist) = refs[:12], refs[12:]

    def write_ha(j, rows):
        ha_ref[0, j * L:(j + 1) * L, :] = rows

    _swa_body(pl.program_id(1), pl.program_id(1) == pl.num_programs(1) - 1, *ins, cko_ref, cvo_ref, khist, vhist,
              write_ha, L=L, CS=CS, cache_len=cache_len)


def _swa_out_kernel(*refs, L, CS, nc, n):
    ins, (x_ref, hm_ref, wm_ref, wa_ref), (o_ref, cko_ref, cvo_ref, khist, vhist, ha_prev) = (
        refs[:12], refs[12:16], refs[16:])
    s = pl.program_id(0)
    c = jnp.minimum(s, n - 1) % nc

    @pl.when(s == 0)
    def _():
        ha_prev[...] = jnp.zeros(ha_prev.shape, ha_prev.dtype)

    hm = hm_ref[...]
    ha_old = ha_prev[...]

    def project(j, parts):
        w = o_ref.shape[1] // parts
        cols = slice(j * w, (j + 1) * w)
        o_ref[:, cols] = x_ref[:, cols] + _dot(hm, wm_ref[:, cols]) + _dot(ha_old, wa_ref[:, cols])

    def write_ha(j, rows):
        ha_prev[j * L:(j + 1) * L, :] = rows

    _swa_body(c, jnp.logical_and(s < n, c == nc - 1), *ins, cko_ref, cvo_ref, khist, vhist,
              write_ha, L=L, CS=CS, cache_len=0, side_work=project)


def _swa_out(proj, col0, x, hm, w_out, g_q, g_k, g_ao, bias, sink, consts, layer, L, CS):
    b, s, _ = proj.shape
    m, d = x.shape
    T = CS * L
    nc = s // T
    n = b * nc
    lk = WINDOW + L
    gl = A_GROUP * L
    nvar = bias.shape[0]
    q_blk, kv_blk = col0 // A_WIDTH, (col0 + A_WIDTH) // A_KV_WIDTH
    e4, eye = consts
    zk = jnp.zeros((b, WINDOW, A_KV_WIDTH), F32)
    cur = lambda s_: jnp.minimum(s_, n - 1)
    prev = lambda s_: jnp.maximum(s_ - 1, 0)
    full = lambda *shape: pl.BlockSpec(shape, lambda s_: (0,) * len(shape))
    out_shape = (jax.ShapeDtypeStruct((m, d), F32),
                 jax.ShapeDtypeStruct((b, WINDOW, A_KV_WIDTH), F32),
                 jax.ShapeDtypeStruct((b, WINDOW, A_KV_WIDTH), F32))
    return pl.pallas_call(
        functools.partial(_swa_out_kernel, L=L, CS=CS, nc=nc, n=n),
        out_shape=out_shape,
        grid=(n + 1,),
        in_specs=[pl.BlockSpec((1, T, A_WIDTH), lambda s_: (cur(s_) // nc, cur(s_) % nc, q_blk)),
                  pl.BlockSpec((1, T, A_KV_WIDTH), lambda s_: (cur(s_) // nc, cur(s_) % nc, kv_blk)),
                  pl.BlockSpec((1, T, A_KV_WIDTH), lambda s_: (cur(s_) // nc, cur(s_) % nc, kv_blk + 1)),
                  pl.BlockSpec((1, WINDOW, A_KV_WIDTH), lambda s_: (cur(s_) // nc, 0, 0)),
                  pl.BlockSpec((1, WINDOW, A_KV_WIDTH), lambda s_: (cur(s_) // nc, 0, 0)),
                  pl.BlockSpec((None, 1, A_WIDTH), lambda s_: (layer, 0, 0)),
                  pl.BlockSpec((None, 1, A_KV_WIDTH), lambda s_: (layer, 0, 0)),
                  pl.BlockSpec((None, 1, A_WIDTH), lambda s_: (layer, 0, 0)),
                  full(nvar, A_KV_HEADS, lk, gl),
                  pl.BlockSpec((None, A_KV_HEADS, 1, gl), lambda s_: (layer, 0, 0, 0)),
                  full(A_KV_WIDTH, A_KV_WIDTH),
                  full(A_KV_WIDTH, A_KV_WIDTH),
                  pl.BlockSpec((T, d), lambda s_: (prev(s_), 0)),
                  pl.BlockSpec((T, M_WIDTH), lambda s_: (prev(s_), 0)),
                  pl.BlockSpec((None, M_WIDTH, d), lambda s_: (layer, 0, 0), pipeline_mode=pl.Buffered(1)),
                  pl.BlockSpec((None, A_WIDTH, d), lambda s_: (layer, 1, 0), pipeline_mode=pl.Buffered(1))],
        out_specs=(pl.BlockSpec((T, d), lambda s_: (prev(s_), 0)),
                   pl.BlockSpec((1, WINDOW, A_KV_WIDTH), lambda s_: (cur(s_) // nc, 0, 0)),
                   pl.BlockSpec((1, WINDOW, A_KV_WIDTH), lambda s_: (cur(s_) // nc, 0, 0))),
        scratch_shapes=[pltpu.VMEM((WINDOW, A_KV_WIDTH), F32),
                        pltpu.VMEM((WINDOW, A_KV_WIDTH), F32),
                        pltpu.VMEM((T, A_WIDTH), BF16)],
        compiler_params=_params(("arbitrary",)),
        name="swa_out",
    )(proj, proj, proj, zk, zk, g_q, g_k, g_ao, bias, sink, e4, eye, x, hm, w_out, w_out)


def _swa(proj, col0, ck, cv, g_q, g_k, g_ao, bias, sink, consts, layer, L, CS, cache_len):
    b, s, _ = proj.shape
    T = CS * L
    lk = WINDOW + L
    gl = A_GROUP * L
    nvar = bias.shape[0]
    q_blk, kv_blk = col0 // A_WIDTH, (col0 + A_WIDTH) // A_KV_WIDTH
    e4, eye = consts
    out_shape = (jax.ShapeDtypeStruct((b, s, A_WIDTH), BF16),
                 jax.ShapeDtypeStruct((b, WINDOW, A_KV_WIDTH), F32),
                 jax.ShapeDtypeStruct((b, WINDOW, A_KV_WIDTH), F32))
    full = lambda *shape: pl.BlockSpec(shape, lambda i, c: (0,) * len(shape))
    return pl.pallas_call(
        functools.partial(_swa_kernel, L=L, CS=CS, cache_len=cache_len),
        out_shape=out_shape,
        grid=(b, s // T),
        in_specs=[pl.BlockSpec((1, T, A_WIDTH), lambda i, c: (i, c, q_blk)),
                  pl.BlockSpec((1, T, A_KV_WIDTH), lambda i, c: (i, c, kv_blk)),
                  pl.BlockSpec((1, T, A_KV_WIDTH), lambda i, c: (i, c, kv_blk + 1)),
                  pl.BlockSpec((1, WINDOW, A_KV_WIDTH), lambda i, c: (i, 0, 0)),
                  pl.BlockSpec((1, WINDOW, A_KV_WIDTH), lambda i, c: (i, 0, 0)),
                  pl.BlockSpec((None, 1, A_WIDTH), lambda i, c: (layer, 0, 0)),
                  pl.BlockSpec((None, 1, A_KV_WIDTH), lambda i, c: (layer, 0, 0)),
                  pl.BlockSpec((None, 1, A_WIDTH), lambda i, c: (layer, 0, 0)),
                  full(nvar, A_KV_HEADS, lk, gl),
                  pl.BlockSpec((None, A_KV_HEADS, 1, gl), lambda i, c: (layer, 0, 0, 0)),
                  full(A_KV_WIDTH, A_KV_WIDTH),
                  full(A_KV_WIDTH, A_KV_WIDTH)],
        out_specs=(pl.BlockSpec((1, T, A_WIDTH), lambda i, c: (i, c, 0)),
                   pl.BlockSpec((1, WINDOW, A_KV_WIDTH), lambda i, c: (i, 0, 0)),
                   pl.BlockSpec((1, WINDOW, A_KV_WIDTH), lambda i, c: (i, 0, 0))),
        scratch_shapes=[pltpu.VMEM((WINDOW, A_KV_WIDTH), F32),
                        pltpu.VMEM((WINDOW, A_KV_WIDTH), F32)],
        compiler_params=_params(("arbitrary", "arbitrary")),
        name="swa",
    )(proj, proj, proj, ck, cv, g_q, g_k, g_ao, bias, sink, e4, eye)


def _out_proj_kernel(x_ref, hm_ref, ha_ref, wm_ref, wa_ref, o_ref):
    o_ref[...] = x_ref[...] + _dot(hm_ref[...], wm_ref[...]) + _dot(ha_ref[...], wa_ref[...])


def _out_proj(x, hm, ha, w_out, layer):
    m, d = x.shape
    tm = _row_tile(m, 512)
    return pl.pallas_call(
        _out_proj_kernel,
        out_shape=jax.ShapeDtypeStruct((m, d), F32),
        grid=(m // tm,),
        in_specs=[pl.BlockSpec((tm, d), lambda i: (i, 0)),
                  pl.BlockSpec((tm, M_WIDTH), lambda i: (i, 0)),
                  pl.BlockSpec((tm, A_WIDTH), lambda i: (i, 0)),
                  pl.BlockSpec((None, M_WIDTH, d), lambda i: (layer, 0, 0)),
                  pl.BlockSpec((None, A_WIDTH, d), lambda i: (layer, 1, 0))],
        out_specs=pl.BlockSpec((tm, d), lambda i: (i, 0)),
        compiler_params=_params(("parallel",)),
        name="out_proj",
    )(x, hm, ha, w_out, w_out)


def _ffn_kernel(x_ref, g_ref, wu_ref, wd_ref, o_ref, *rest):
    h_scr = rest[-1]
    f = pl.program_id(1)

    @pl.when(f == 0)
    def _():
        x = x_ref[...]
        h_scr[...] = _rms(x, g_ref[...]).astype(BF16)
        o_ref[...] = x

    wu, wd = wu_ref[...].astype(BF16), wd_ref[...].astype(BF16)
    if len(rest) == 3:
        rest[0][...] = wu
        rest[1][...] = wd
    u = _dot(h_scr[...], wu)
    a = jnp.square(jnp.maximum(u, 0.0)).astype(BF16)
    o_ref[...] += _dot(a, wd)


def _ffn(x, g_ffn, w_up, w_down, layer, emit_bf16=False):
    m, d = x.shape
    d_ff = w_up.shape[-1]
    tm = _row_tile(m, 512)
    tf = 512 if emit_bf16 else 1024
    assert not emit_bf16 or m == tm
    if emit_bf16:
        w_specs = [pl.BlockSpec((None, d, tf), lambda i, f: (layer, 0, f)),
                   pl.BlockSpec((None, tf, d), lambda i, f: (layer, f, 0))]
    else:
        w_specs = [pl.BlockSpec((d, tf), lambda i, f: (0, f)), pl.BlockSpec((tf, d), lambda i, f: (f, 0))]
    out_shape = [jax.ShapeDtypeStruct((m, d), F32)]
    out_specs = [pl.BlockSpec((tm, d), lambda i, f: (i, 0))]
    if emit_bf16:
        out_shape += [jax.ShapeDtypeStruct((d, d_ff), BF16), jax.ShapeDtypeStruct((d_ff, d), BF16)]
        out_specs += [pl.BlockSpec((d, tf), lambda i, f: (0, f)), pl.BlockSpec((tf, d), lambda i, f: (f, 0))]
    return pl.pallas_call(
        _ffn_kernel,
        out_shape=out_shape,
        grid=(m // tm, d_ff // tf),
        in_specs=[pl.BlockSpec((tm, d), lambda i, f: (i, 0)),
                  pl.BlockSpec((None, 1, d), lambda i, f: (layer, 0, 0))] + w_specs,
        out_specs=out_specs,
        scratch_shapes=[pltpu.VMEM((tm, d), BF16)],
        compiler_params=_params(("parallel", "arbitrary")),
        name="ffn_cast" if emit_bf16 else "ffn",
    )(x, g_ffn, w_up, w_down)


def _layer(x, wts, layer, lm, la, cs, bias, sink, ck, cv, c0, n0, m0, cache_len, ffn_w):
    b, s, d = x.shape
    x2 = x.reshape(b * s, d)
    eye = wts["consts"][1]
    if ck is None:
        hm, aproj, c_new, n_new, m_new = _proj_mlstm(x, wts["g_mix"], wts["w_all"], wts["gbias"], wts["g_mo"], eye,
                                                     c0, n0, m0, layer, lm)
        x2, k_new, v_new = _swa_out(aproj, 0, x2, hm.reshape(b * s, M_WIDTH), wts["w_out"], wts["g_q"], wts["g_k"],
                                    wts["g_ao"], bias, sink, wts["consts"], layer, la, cs)
    else:
        proj, gates = _in_proj(x2, wts["g_mix"], wts["w_all"], layer)
        proj = proj.reshape(b, s, PROJ_W)
        hm, c_new, n_new, m_new = _mlstm(proj, gates.reshape(b, s, GATE_W), wts["gbias"], wts["g_mo"], eye,
                                         c0, n0, m0, layer, lm)
        ha, k_new, v_new = _swa(proj, M_PROJ_W, ck, cv, wts["g_q"], wts["g_k"], wts["g_ao"], bias, sink, wts["consts"],
                                layer, la, cs, cache_len)
        x2 = _out_proj(x2, hm.reshape(b * s, M_WIDTH), ha.reshape(b * s, A_WIDTH), wts["w_out"], layer)
    if ffn_w is None:
        x2, *ffn_w = _ffn(x2, wts["g_ffn"], wts["w_up"], wts["w_down"], layer, emit_bf16=True)
    else:
        x2, = _ffn(x2, wts["g_ffn"], *ffn_w, layer)
    return x2.reshape(b, s, d), k_new, v_new, c_new, n_new, m_new, ffn_w


def _w_in_columns():
    o = np.cumsum((0, M_HEADS * M_DK, M_HEADS * M_DK, M_WIDTH, M_WIDTH, M_HEADS, M_HEADS, A_WIDTH, A_KV_WIDTH, A_KV_WIDTH))
    mq, mk, mv, mo, mi, mf, aq, ak, av = (int(v) for v in o[:9])
    runs = []
    for h in range(M_HEADS):
        runs += [(mq + h * M_DK, M_DK), (mk + h * M_DK, M_DK), (mv + h * M_DV, M_DV), (mo + h * M_DV, M_DV)]
    for g in range(A_GROUP):
        runs += [(aq + (kv * A_GROUP + g) * A_HEAD_DIM, A_HEAD_DIM) for kv in range(A_KV_HEADS)]
    return runs + [(ak, A_KV_WIDTH), (av, A_KV_WIDTH), (mi, M_HEADS), (mf, M_HEADS)]


def _w_in_layout_kernel(w_ref, o_ref):
    dst = 0
    for src, width in _w_in_columns():
        o_ref[:, dst:dst + width] = w_ref[:, src:src + width].astype(o_ref.dtype)
        dst += width
    o_ref[:, dst:] = jnp.zeros((o_ref.shape[0], o_ref.shape[1] - dst), o_ref.dtype)


def _w_in_layout(w_in):
    depth, d, d_in = w_in.shape
    tk = _row_tile(d, 256)
    return pl.pallas_call(
        _w_in_layout_kernel,
        out_shape=jax.ShapeDtypeStruct((depth, d, PROJ_W + GATE_W), BF16),
        grid=(depth, d // tk),
        in_specs=[pl.BlockSpec((None, tk, d_in), lambda l, i: (l, i, 0))],
        out_specs=pl.BlockSpec((None, tk, PROJ_W + GATE_W), lambda l, i: (l, i, 0)),
        compiler_params=_params(("parallel", "parallel")),
        name="w_in_layout",
    )(w_in)


def _head_consts():
    hd = np.arange(A_KV_WIDTH) // A_HEAD_DIM
    e4 = (hd[:, None] == hd[None, :]).astype(np.float32)
    eye = np.eye(A_KV_WIDTH, dtype=np.float32)
    return tuple(jnp.asarray(a, BF16) for a in (e4, eye))


def kernel(x_prompt, x_sample, cache_k, cache_v, state_C, state_n, state_m, rel_bias, g_mix, w_in, b_i, b_f, g_q, g_k, sinks, g_mo, g_ao, w_out, g_ffn, w_up, w_down):
    depth = w_in.shape[0]
    bp, sp, d = x_prompt.shape
    bs, ls, _ = x_sample.shape
    n_win = cache_k.shape[2]
    assert sp % CHUNK == 0 and n_win == WINDOW and ls % 16 == 0 and ls <= CHUNK
    lm = M_CHUNK if sp % M_CHUNK == 0 else CHUNK

    w_all = _w_in_layout(w_in.astype(BF16))
    gbias = jnp.concatenate([b_i, b_f, jnp.zeros((depth, GATE_W - 2 * M_HEADS), F32)], axis=-1)

    def regroup(a):
        rest = a.shape[2:]
        a = a.reshape(depth, A_KV_HEADS, A_GROUP, A_HEAD_DIM, *rest)
        return jnp.swapaxes(a, 1, 2).reshape(depth, A_WIDTH, *rest)

    w_out_b = jnp.concatenate([w_out[:, :M_WIDTH], regroup(w_out[:, M_WIDTH:])], axis=1).astype(BF16)
    wts = {
        "g_mix": g_mix.reshape(depth, 1, d), "w_all": w_all,
        "gbias": gbias.reshape(depth, 1, GATE_W), "g_mo": g_mo.reshape(depth, 1, M_WIDTH),
        "g_q": jnp.tile(g_q * A_SCALE, (1, A_HEADS)).reshape(depth, 1, A_WIDTH),
        "g_k": jnp.tile(g_k, (1, A_KV_HEADS)).reshape(depth, 1, A_KV_WIDTH),
        "g_ao": regroup(g_ao).reshape(depth, 1, A_WIDTH),
        "w_out": w_out_b, "g_ffn": g_ffn.reshape(depth, 1, d),
        "w_up": w_up, "w_down": w_down,
        "consts": _head_consts(),
    }

    bias = _bias_table(rel_bias)

    def bias_t(lq, cache_len):
        lk = WINDOW + lq
        t = bias[:, :lq, :lk].reshape(A_KV_HEADS, A_GROUP, lq, lk)
        t = t.transpose(0, 3, 1, 2).reshape(A_KV_HEADS, lk, A_GROUP * lq)
        n_prev = np.arange((WINDOW - cache_len) // lq + 1) * lq + cache_len
        dead = np.arange(lk)[None, :] < WINDOW - n_prev[:, None]
        return t[None] + jnp.asarray(np.where(dead, NEG_INF, 0.0)[:, None, :, None], F32)

    def sink_rows(lq):
        return jnp.repeat(sinks.reshape(depth, A_KV_HEADS, 1, A_GROUP), lq, axis=-1)

    zc =jnp.zeros((bp, M_HEADS, M_DV, M_DK), F32)
    zn = jnp.zeros((bp, M_HEADS, M_DK), F32)
    zm = jnp.zeros((bp, 1, M_HEADS), F32)

    xp, xs = x_prompt, x_sample
    bias_p, bias_s, sink_p, sink_s = bias_t(CHUNK, 0), bias_t(ls, WINDOW), sink_rows(CHUNK), sink_rows(ls)
    cs = A_CHUNKS_PER_STEP if sp % (A_CHUNKS_PER_STEP * CHUNK) == 0 else 1
    outs_p, outs_s = [], []
    for l in range(depth):
        xs, *st, ffn_w = _layer(xs, wts, l, ls, ls, 1, bias_s, sink_s,
                                cache_k[l].reshape(bs, n_win, A_KV_WIDTH), cache_v[l].reshape(bs, n_win, A_KV_WIDTH),
                                state_C[l], state_n[l], state_m[l].reshape(bs, 1, M_HEADS), WINDOW, None)
        outs_s.append(st)
        xp, *st, _ = _layer(xp, wts, l, lm, CHUNK, cs, bias_p, sink_p, None, None, zc, zn, zm, 0, ffn_w)
        outs_p.append(st)

    def stack(outs, b):
        k, v, c, n, m = (jnp.stack([o_[i] for o_ in outs]) for i in range(5))
        return (k.reshape(depth, b, WINDOW, A_KV_HEADS, A_HEAD_DIM), v.reshape(depth, b, WINDOW, A_KV_HEADS, A_HEAD_DIM),
                c, n, m.reshape(depth, b, M_HEADS))

    return (xp, xs) + stack(outs_p, bp) + stack(outs_s, bs)
```

```python
import functools

import jax
import jax.numpy as jnp
import numpy as np
from jax import lax
from jax.experimental import pallas as pl
from jax.experimental.pallas import tpu as pltpu

F32 = jnp.float32
BF16 = jnp.bfloat16

CHUNK = 64
M_CHUNK = 256
A_CHUNKS_PER_STEP = 4
M_HEADS = 4
M_DK = 128
M_DV = 256
M_WIDTH = M_HEADS * M_DV
A_HEADS = 16
A_KV_HEADS = 4
A_HEAD_DIM = 64
A_GROUP = A_HEADS // A_KV_HEADS
A_WIDTH = A_HEADS * A_HEAD_DIM
A_KV_WIDTH = A_KV_HEADS * A_HEAD_DIM
WINDOW = 128
N_BUCKETS = 32
MAX_DISTANCE = 128
EPS = 1e-6
NEG_INF = -1e30
M_SCALE = M_DK ** -0.5
A_SCALE = A_HEAD_DIM ** -0.5

HEAD_W = 2 * M_DK + 2 * M_DV
M_PROJ_W = M_HEADS * HEAD_W
A_PROJ_W = A_WIDTH + 2 * A_KV_WIDTH
PROJ_W = M_PROJ_W + A_PROJ_W
GATE_W = 128

V7X_VMEM_LIMIT = 56 * 1024 * 1024

NT_DIMS = (((1,), (1,)), ((), ()))


def _params(sem, vmem=V7X_VMEM_LIMIT):
    return pltpu.CompilerParams(dimension_semantics=sem, vmem_limit_bytes=vmem)


def _row_tile(m, cap):
    t = min(m, cap)
    while m % t:
        t //= 2
    return t


def _dot(a, b):
    return jnp.dot(a, b, preferred_element_type=F32)


def _dot_nt(a, b):
    return lax.dot_general(a, b, NT_DIMS, preferred_element_type=F32)


def _t5_bucket(rel):
    half = N_BUCKETS // 2
    exact = half // 2
    n = np.abs(rel)
    large = exact + (np.log(np.maximum(n, 1) / exact) / np.log(MAX_DISTANCE / exact) * (half - exact)).astype(np.int32)
    large = np.minimum(large, half - 1)
    return (rel > 0).astype(np.int32) * half + np.where(n < exact, n, large).astype(np.int32)


def _bias_kernel(rel_ref, map_ref, out_ref):
    bmap = map_ref[...]
    for h in range(A_HEADS):
        acc = jnp.zeros(bmap.shape, F32)
        for b in range(N_BUCKETS):
            acc = jnp.where(bmap == b, rel_ref[b, h], acc)
        out_ref[h] = acc


def _bias_table(rel_bias):
    lk = WINDOW + CHUNK
    rel = (np.arange(lk)[None, :] - WINDOW) - np.arange(CHUNK)[:, None]
    bmap = jnp.asarray(_t5_bucket(rel), jnp.int32)
    return pl.pallas_call(
        _bias_kernel,
        out_shape=jax.ShapeDtypeStruct((A_HEADS, CHUNK, lk), F32),
        in_specs=[pl.BlockSpec(memory_space=pltpu.SMEM),
                  pl.BlockSpec(memory_space=pltpu.VMEM)],
        out_specs=pl.BlockSpec(memory_space=pltpu.VMEM),
        name="t5_bias_table",
    )(rel_bias, bmap)


def _rms(x, g):
    return x * lax.rsqrt(jnp.mean(x * x, axis=-1, keepdims=True) + EPS) * g


def _in_proj_kernel(x_ref, g_ref, w_ref, o_ref, gate_ref, *, sub):
    for r in range(0, x_ref.shape[0], sub):
        h = _rms(x_ref[r:r + sub, :], g_ref[...]).astype(BF16)
        acc = _dot(h, w_ref[...])
        o_ref[r:r + sub, :] = acc[:, :PROJ_W].astype(o_ref.dtype)
        gate_ref[r:r + sub, :] = acc[:, PROJ_W:]


def _in_proj(x, g_mix, w_all, layer):
    m, d = x.shape
    tm = _row_tile(m, 512)
    n = PROJ_W + GATE_W
    return pl.pallas_call(
        functools.partial(_in_proj_kernel, sub=min(tm, 256)),
        out_shape=(jax.ShapeDtypeStruct((m, PROJ_W), BF16), jax.ShapeDtypeStruct((m, GATE_W), F32)),
        grid=(m // tm,),
        in_specs=[pl.BlockSpec((tm, d), lambda i: (i, 0)),
                  pl.BlockSpec((None, 1, d), lambda i: (layer, 0, 0)),
                  pl.BlockSpec((None, d, n), lambda i: (layer, 0, 0), pipeline_mode=pl.Buffered(1))],
        out_specs=(pl.BlockSpec((tm, PROJ_W), lambda i: (i, 0)),
                   pl.BlockSpec((tm, GATE_W), lambda i: (i, 0))),
        compiler_params=_params(("parallel",)),
        name="in_proj",
    )(x, g_mix, w_all)


def _mlstm_body(head_inputs, gates, gbias_ref, gmo_ref, eye_ref, c0_ref, n0_ref, m0_ref,
                hm_ref, cout_ref, nout_ref, mout_ref, c_scr, n_scr, m_scr, *, L):
    c = pl.program_id(1)
    last = pl.num_programs(1) - 1

    @pl.when(c == 0)
    def _():
        c_scr[...] = c0_ref[0]
        n_scr[...] = n0_ref[0]
        for h in range(M_HEADS):
            m_scr[h:h + 1, :] = jnp.broadcast_to(m0_ref[0, :, h:h + 1], (1, 128))

    a_all = gates + gbias_ref[...]
    logf = jnp.minimum(a_all, 0.0) - jnp.log(1.0 + jnp.exp(-jnp.abs(a_all)))
    row = lax.broadcasted_iota(jnp.int32, (L, GATE_W), 0)
    b_all = logf
    k = 1
    while k < L:
        b_all = b_all + jnp.where(row >= k, pltpu.roll(b_all, k, axis=0), 0.0)
        k *= 2
    b_sh = pltpu.roll(b_all, GATE_W - M_HEADS, axis=1)
    a_i = a_all - b_sh
    cmax = a_i
    k = 1
    while k < L:
        cmax = jnp.maximum(cmax, jnp.where(row >= k, pltpu.roll(cmax, k, axis=0), NEG_INF))
        k *= 2
    a_t = a_i.T
    tri = lax.broadcasted_iota(jnp.int32, (L, L), 0) >= lax.broadcasted_iota(jnp.int32, (L, L), 1)
    eye = eye_ref[...]

    for h in range(M_HEADS):
        b_col = b_sh[:, h:h + 1]
        a_col = a_i[:, h:h + 1]
        a_row = a_t[h:h + 1, :]
        m_prev = m_scr[h:h + 1, 0:1]
        m_run = jnp.maximum(cmax[:, h:h + 1], m_prev)
        decay_w = jnp.exp(jnp.where(tri, a_row - m_run, NEG_INF))
        qh, kh, vh, og = head_inputs(h)
        w = decay_w * (_dot_nt(qh, kh) * M_SCALE)
        g = jnp.exp(m_prev - m_run)
        cst = c_scr[h]
        n_row = n_scr[h:h + 1, :]
        num = _dot(w.astype(BF16), vh) + g * _dot_nt(qh, cst.astype(BF16))
        den = (jnp.sum(w, axis=1, keepdims=True)
               + g * jnp.sum(qh.astype(F32) * n_row, axis=1, keepdims=True))
        inv = 1.0 / jnp.maximum(jnp.abs(den), jnp.exp(-(b_col + m_run)))
        ms = jnp.mean(num * num, axis=1, keepdims=True)
        scale = inv * lax.rsqrt(inv * inv * ms + EPS)
        hm_ref[0, :, h * M_DV:(h + 1) * M_DV] = (
            num * scale * gmo_ref[:, h * M_DV:(h + 1) * M_DV] * jax.nn.sigmoid(og)).astype(hm_ref.dtype)

        b_last = b_col[L - 1:L, :]
        m_end = b_last + m_run[L - 1:L, :]
        wk = jnp.exp(b_last + a_col - m_end)
        decay = jnp.exp(b_last + m_prev - m_end)
        vw = (vh.astype(F32) * wk).astype(BF16)
        vw_t = _dot_nt(eye, vw).astype(BF16)
        c_new = decay * cst + M_SCALE * _dot(vw_t, kh)
        n_new = decay * n_row + M_SCALE * jnp.sum(kh.astype(F32) * wk, axis=0, keepdims=True)
        c_scr[h] = c_new
        n_scr[h:h + 1, :] = n_new
        m_scr[h:h + 1, :] = jnp.broadcast_to(m_end, (1, 128))

        @pl.when(c == last)
        def _():
            cout_ref[0, h] = c_new
            nout_ref[0, h:h + 1, :] = n_new
            mout_ref[0, :, h:h + 1] = m_end


def _mlstm_kernel(p_ref, gate_ref, *rest, L):
    def head_inputs(h):
        o = h * HEAD_W
        return (p_ref[0, :, o:o + M_DK], p_ref[0, :, o + M_DK:o + 2 * M_DK],
                p_ref[0, :, o + 2 * M_DK:o + 2 * M_DK + M_DV], p_ref[0, :, o + 2 * M_DK + M_DV:o + HEAD_W].astype(F32))

    _mlstm_body(head_inputs, gate_ref[0], *rest, L=L)


def _proj_mlstm_kernel(x_ref, g_ref, w_ref, gbias_ref, gmo_ref, eye_ref, c0_ref, n0_ref, m0_ref,
                       hm_ref, ap_ref, cout_ref, nout_ref, mout_ref, c_scr, n_scr, m_scr, *, L):
    hn = _rms(x_ref[0], g_ref[...]).astype(BF16)

    def head_inputs(h):
        o = h * HEAD_W
        ph = _dot(hn, w_ref[:, o:o + HEAD_W])
        return (ph[:, :M_DK].astype(BF16), ph[:, M_DK:2 * M_DK].astype(BF16),
                ph[:, 2 * M_DK:2 * M_DK + M_DV].astype(BF16), ph[:, 2 * M_DK + M_DV:])

    _mlstm_body(head_inputs, _dot(hn, w_ref[:, PROJ_W:]), gbias_ref, gmo_ref, eye_ref, c0_ref, n0_ref, m0_ref,
                hm_ref, cout_ref, nout_ref, mout_ref, c_scr, n_scr, m_scr, L=L)
    ap_ref[0] = _dot(hn, w_ref[:, M_PROJ_W:PROJ_W]).astype(ap_ref.dtype)


def _mlstm_specs(b, layer, L):
    state = [pl.BlockSpec((1, M_HEADS, M_DV, M_DK), lambda i, c: (i, 0, 0, 0)),
             pl.BlockSpec((1, M_HEADS, M_DK), lambda i, c: (i, 0, 0)),
             pl.BlockSpec((1, 1, M_HEADS), lambda i, c: (i, 0, 0))]
    params = [pl.BlockSpec((None, 1, GATE_W), lambda i, c: (layer, 0, 0)),
              pl.BlockSpec((None, 1, M_WIDTH), lambda i, c: (layer, 0, 0)),
              pl.BlockSpec((M_DV, M_DV), lambda i, c: (0, 0))]
    state_shapes = [jax.ShapeDtypeStruct((b, M_HEADS, M_DV, M_DK), F32),
                    jax.ShapeDtypeStruct((b, M_HEADS, M_DK), F32),
                    jax.ShapeDtypeStruct((b, 1, M_HEADS), F32)]
    scratch = [pltpu.VMEM((M_HEADS, M_DV, M_DK), F32), pltpu.VMEM((M_HEADS, M_DK), F32), pltpu.VMEM((M_HEADS, 128), F32)]
    return params + state, state, state_shapes, scratch


def _mlstm(proj, gates, gbias, g_mo, eye, c0, n0, m0, layer, L):
    b, s, _ = proj.shape
    ins, state_out, state_shapes, scratch = _mlstm_specs(b, layer, L)
    return pl.pallas_call(
        functools.partial(_mlstm_kernel, L=L),
        out_shape=[jax.ShapeDtypeStruct((b, s, M_WIDTH), BF16)] + state_shapes,
        grid=(b, s // L),
        in_specs=[pl.BlockSpec((1, L, M_PROJ_W), lambda i, c: (i, c, 0)),
                  pl.BlockSpec((1, L, GATE_W), lambda i, c: (i, c, 0))] + ins,
        out_specs=[pl.BlockSpec((1, L, M_WIDTH), lambda i, c: (i, c, 0))] + state_out,
        scratch_shapes=scratch,
        compiler_params=_params(("arbitrary", "arbitrary")),
        name="mlstm",
    )(proj, gates, gbias, g_mo, eye, c0, n0, m0)


def _proj_mlstm(x, g_mix, w_all, gbias, g_mo, eye, c0, n0, m0, layer, L):
    b, s, d = x.shape
    ins, state_out, state_shapes, scratch = _mlstm_specs(b, layer, L)
    return pl.pallas_call(
        functools.partial(_proj_mlstm_kernel, L=L),
        out_shape=[jax.ShapeDtypeStruct((b, s, M_WIDTH), BF16), jax.ShapeDtypeStruct((b, s, A_PROJ_W), BF16)] + state_shapes,
        grid=(b, s // L),
        in_specs=[pl.BlockSpec((1, L, d), lambda i, c: (i, c, 0)),
                  pl.BlockSpec((None, 1, d), lambda i, c: (layer, 0, 0)),
                  pl.BlockSpec((None, d, PROJ_W + GATE_W), lambda i, c: (layer, 0, 0), pipeline_mode=pl.Buffered(1))] + ins,
        out_specs=[pl.BlockSpec((1, L, M_WIDTH), lambda i, c: (i, c, 0)),
                   pl.BlockSpec((1, L, A_PROJ_W), lambda i, c: (i, c, 0))] + state_out,
        scratch_shapes=scratch,
        compiler_params=_params(("arbitrary", "arbitrary")),
        name="proj_mlstm",
    )(x, g_mix, w_all, gbias, g_mo, eye, c0, n0, m0)


def _swa_body(c, is_last, q_ref, k_ref, v_ref, ck_ref, cv_ref, gq_ref, gk_ref, gao_ref, bias_ref, sink_ref,
              e4_ref, eye_ref, cko_ref, cvo_ref, khist, vhist, write_ha, *, L, CS, cache_len, side_work=None):
    T = CS * L
    lk = WINDOW + L
    gl = A_GROUP * L
    kv_lane = lax.broadcasted_iota(jnp.int32, (1, A_KV_WIDTH), 1) // A_HEAD_DIM

    def lane_masked(kn):
        knb = kn.astype(BF16)
        return [jnp.where(kv_lane == kv, knb, 0.0) for kv in range(A_KV_HEADS)]

    @pl.when(c == 0)
    def _():
        khist[...] = ck_ref[0]
        vhist[...] = cv_ref[0]

    q = q_ref[0].astype(F32)
    k = k_ref[0].astype(F32)
    sq = jnp.concatenate([q[:, g * A_KV_WIDTH:(g + 1) * A_KV_WIDTH] for g in range(A_GROUP)] + [k], axis=0)
    sq = sq * sq
    sq_hi = sq.astype(BF16)
    sq_lo = (sq - sq_hi.astype(F32)).astype(BF16)
    ss = _dot(jnp.concatenate([sq_hi, sq_lo], axis=0), e4_ref[...])
    rinv = lax.rsqrt((ss[:5 * T] + ss[5 * T:]) * (1.0 / A_HEAD_DIM) + EPS)
    qn = [(q[:, g * A_KV_WIDTH:(g + 1) * A_KV_WIDTH] * rinv[g * T:(g + 1) * T, :]
           * gq_ref[:, g * A_KV_WIDTH:(g + 1) * A_KV_WIDTH]).astype(BF16) for g in range(A_GROUP)]
    kn = k * rinv[A_GROUP * T:, :] * gk_ref[...]

    k_all = jnp.concatenate([khist[...], kn], axis=0)
    v_all = jnp.concatenate([vhist[...], v_ref[0].astype(F32)], axis=0)
    vb_all = v_all.astype(BF16)

    for j in range(CS):
        rows = slice(j * L, j * L + lk)
        qst = jnp.concatenate([qn[g][j * L:(j + 1) * L, :] for g in range(A_GROUP)], axis=0)
        km = jnp.concatenate(lane_masked(k_all[rows]), axis=0)
        s_all = _dot_nt(km, qst)
        v_t = _dot_nt(eye_ref[...], vb_all[rows]).astype(BF16)
        var = jnp.minimum((c * CS + j) * L, WINDOW - cache_len) // L
        o_t = []
        if side_work is not None:
            side_work(j, CS)
        for kv in range(A_KV_HEADS):
            s = s_all[kv * lk:(kv + 1) * lk, :] + bias_ref[var, kv]
            sk = sink_ref[kv]
            mx = jnp.maximum(jnp.max(s, axis=0, keepdims=True), sk)
            p = jnp.exp(s - mx)
            den = jnp.sum(p, axis=0, keepdims=True) + jnp.exp(sk - mx)
            o = _dot(v_t[kv * A_HEAD_DIM:(kv + 1) * A_HEAD_DIM, :], p.astype(BF16))
            o_t.append(o * (1.0 / den))
        o_all = jnp.concatenate(o_t, axis=0).T
        ha = jnp.concatenate([o_all[g * L:(g + 1) * L, :] for g in range(A_GROUP)], axis=1)
        write_ha(j, _rms(ha, gao_ref[...]).astype(BF16))

    k_win = k_all[T:T + WINDOW]
    v_win = v_all[T:T + WINDOW]
    khist[...] = k_win
    vhist[...] = v_win

    @pl.when(is_last)
    def _():
        cko_ref[0] = k_win
        cvo_ref[0] = v_win


def _swa_kernel(*refs, L, CS, cache_len):
    ins, (ha_ref, cko_ref, cvo_ref, khist, vhist) = refs[:12], refs[12:]

    def write_ha(j, rows):
        ha_ref[0, j * L:(j + 1) * L, :] = rows

    _swa_body(pl.program_id(1), pl.program_id(1) == pl.num_programs(1) - 1, *ins, cko_ref, cvo_ref, khist, vhist,
              write_ha, L=L, CS=CS, cache_len=cache_len)


def _swa_out_kernel(*refs, L, CS, nc, n):
    ins, (x_ref, hm_ref, wm_ref, wa_ref), (o_ref, cko_ref, cvo_ref, khist, vhist, ha_prev) = (
        refs[:12], refs[12:16], refs[16:])
    s = pl.program_id(0)
    c = jnp.minimum(s, n - 1) % nc

    @pl.when(s == 0)
    def _():
        ha_prev[...] = jnp.zeros(ha_prev.shape, ha_prev.dtype)

    hm = hm_ref[...]
    ha_old = ha_prev[...]

    def project(j, parts):
        w = o_ref.shape[1] // parts
        cols = slice(j * w, (j + 1) * w)
        o_ref[:, cols] = x_ref[:, cols] + _dot(hm, wm_ref[:, cols]) + _dot(ha_old, wa_ref[:, cols])

    def write_ha(j, rows):
        ha_prev[j * L:(j + 1) * L, :] = rows

    _swa_body(c, jnp.logical_and(s < n, c == nc - 1), *ins, cko_ref, cvo_ref, khist, vhist,
              write_ha, L=L, CS=CS, cache_len=0, side_work=project)


def _swa_out(proj, col0, x, hm, w_out, g_q, g_k, g_ao, bias, sink, consts, layer, L, CS):
    b, s, _ = proj.shape
    m, d = x.shape
    T = CS * L
    nc = s // T
    n = b * nc
    lk = WINDOW + L
    gl = A_GROUP * L
    nvar = bias.shape[0]
    q_blk, kv_blk = col0 // A_WIDTH, (col0 + A_WIDTH) // A_KV_WIDTH
    e4, eye = consts
    zk = jnp.zeros((b, WINDOW, A_KV_WIDTH), F32)
    cur = lambda s_: jnp.minimum(s_, n - 1)
    prev = lambda s_: jnp.maximum(s_ - 1, 0)
    full = lambda *shape: pl.BlockSpec(shape, lambda s_: (0,) * len(shape))
    out_shape = (jax.ShapeDtypeStruct((m, d), F32),
                 jax.ShapeDtypeStruct((b, WINDOW, A_KV_WIDTH), F32),
                 jax.ShapeDtypeStruct((b, WINDOW, A_KV_WIDTH), F32))
    return pl.pallas_call(
        functools.partial(_swa_out_kernel, L=L, CS=CS, nc=nc, n=n),
        out_shape=out_shape,
        grid=(n + 1,),
        in_specs=[pl.BlockSpec((1, T, A_WIDTH), lambda s_: (cur(s_) // nc, cur(s_) % nc, q_blk)),
                  pl.BlockSpec((1, T, A_KV_WIDTH), lambda s_: (cur(s_) // nc, cur(s_) % nc, kv_blk)),
                  pl.BlockSpec((1, T, A_KV_WIDTH), lambda s_: (cur(s_) // nc, cur(s_) % nc, kv_blk + 1)),
                  pl.BlockSpec((1, WINDOW, A_KV_WIDTH), lambda s_: (cur(s_) // nc, 0, 0)),
                  pl.BlockSpec((1, WINDOW, A_KV_WIDTH), lambda s_: (cur(s_) // nc, 0, 0)),
                  pl.BlockSpec((None, 1, A_WIDTH), lambda s_: (layer, 0, 0)),
                  pl.BlockSpec((None, 1, A_KV_WIDTH), lambda s_: (layer, 0, 0)),
                  pl.BlockSpec((None, 1, A_WIDTH), lambda s_: (layer, 0, 0)),
                  full(nvar, A_KV_HEADS, lk, gl),
                  pl.BlockSpec((None, A_KV_HEADS, 1, gl), lambda s_: (layer, 0, 0, 0)),
                  full(A_KV_WIDTH, A_KV_WIDTH),
                  full(A_KV_WIDTH, A_KV_WIDTH),
                  pl.BlockSpec((T, d), lambda s_: (prev(s_), 0)),
                  pl.BlockSpec((T, M_WIDTH), lambda s_: (prev(s_), 0)),
                  pl.BlockSpec((None, M_WIDTH, d), lambda s_: (layer, 0, 0), pipeline_mode=pl.Buffered(1)),
                  pl.BlockSpec((None, A_WIDTH, d), lambda s_: (layer, 1, 0), pipeline_mode=pl.Buffered(1))],
        out_specs=(pl.BlockSpec((T, d), lambda s_: (prev(s_), 0)),
                   pl.BlockSpec((1, WINDOW, A_KV_WIDTH), lambda s_: (cur(s_) // nc, 0, 0)),
                   pl.BlockSpec((1, WINDOW, A_KV_WIDTH), lambda s_: (cur(s_) // nc, 0, 0))),
        scratch_shapes=[pltpu.VMEM((WINDOW, A_KV_WIDTH), F32),
                        pltpu.VMEM((WINDOW, A_KV_WIDTH), F32),
                        pltpu.VMEM((T, A_WIDTH), BF16)],
        compiler_params=_params(("arbitrary",)),
        name="swa_out",
    )(proj, proj, proj, zk, zk, g_q, g_k, g_ao, bias, sink, e4, eye, x, hm, w_out, w_out)


def _swa(proj, col0, ck, cv, g_q, g_k, g_ao, bias, sink, consts, layer, L, CS, cache_len):
    b, s, _ = proj.shape
    T = CS * L
    lk = WINDOW + L
    gl = A_GROUP * L
    nvar = bias.shape[0]
    q_blk, kv_blk = col0 // A_WIDTH, (col0 + A_WIDTH) // A_KV_WIDTH
    e4, eye = consts
    out_shape = (jax.ShapeDtypeStruct((b, s, A_WIDTH), BF16),
                 jax.ShapeDtypeStruct((b, WINDOW, A_KV_WIDTH), F32),
                 jax.ShapeDtypeStruct((b, WINDOW, A_KV_WIDTH), F32))
    full = lambda *shape: pl.BlockSpec(shape, lambda i, c: (0,) * len(shape))
    return pl.pallas_call(
        functools.partial(_swa_kernel, L=L, CS=CS, cache_len=cache_len),
        out_shape=out_shape,
        grid=(b, s // T),
        in_specs=[pl.BlockSpec((1, T, A_WIDTH), lambda i, c: (i, c, q_blk)),
                  pl.BlockSpec((1, T, A_KV_WIDTH), lambda i, c: (i, c, kv_blk)),
                  pl.BlockSpec((1, T, A_KV_WIDTH), lambda i, c: (i, c, kv_blk + 1)),
                  pl.BlockSpec((1, WINDOW, A_KV_WIDTH), lambda i, c: (i, 0, 0)),
                  pl.BlockSpec((1, WINDOW, A_KV_WIDTH), lambda i, c: (i, 0, 0)),
                  pl.BlockSpec((None, 1, A_WIDTH), lambda i, c: (layer, 0, 0)),
                  pl.BlockSpec((None, 1, A_KV_WIDTH), lambda i, c: (layer, 0, 0)),
                  pl.BlockSpec((None, 1, A_WIDTH), lambda i, c: (layer, 0, 0)),
                  full(nvar, A_KV_HEADS, lk, gl),
                  pl.BlockSpec((None, A_KV_HEADS, 1, gl), lambda i, c: (layer, 0, 0, 0)),
                  full(A_KV_WIDTH, A_KV_WIDTH),
                  full(A_KV_WIDTH, A_KV_WIDTH)],
        out_specs=(pl.BlockSpec((1, T, A_WIDTH), lambda i, c: (i, c, 0)),
                   pl.BlockSpec((1, WINDOW, A_KV_WIDTH), lambda i, c: (i, 0, 0)),
                   pl.BlockSpec((1, WINDOW, A_KV_WIDTH), lambda i, c: (i, 0, 0))),
        scratch_shapes=[pltpu.VMEM((WINDOW, A_KV_WIDTH), F32),
                        pltpu.VMEM((WINDOW, A_KV_WIDTH), F32)],
        compiler_params=_params(("arbitrary", "arbitrary")),
        name="swa",
    )(proj, proj, proj, ck, cv, g_q, g_k, g_ao, bias, sink, e4, eye)


def _out_proj_kernel(x_ref, hm_ref, ha_ref, wm_ref, wa_ref, o_ref):
    o_ref[...] = x_ref[...] + _dot(hm_ref[...], wm_ref[...]) + _dot(ha_ref[...], wa_ref[...])


def _out_proj(x, hm, ha, w_out, layer):
    m, d = x.shape
    tm = _row_tile(m, 512)
    return pl.pallas_call(
        _out_proj_kernel,
        out_shape=jax.ShapeDtypeStruct((m, d), F32),
        grid=(m // tm,),
        in_specs=[pl.BlockSpec((tm, d), lambda i: (i, 0)),
                  pl.BlockSpec((tm, M_WIDTH), lambda i: (i, 0)),
                  pl.BlockSpec((tm, A_WIDTH), lambda i: (i, 0)),
                  pl.BlockSpec((None, M_WIDTH, d), lambda i: (layer, 0, 0)),
                  pl.BlockSpec((None, A_WIDTH, d), lambda i: (layer, 1, 0))],
        out_specs=pl.BlockSpec((tm, d), lambda i: (i, 0)),
        compiler_params=_params(("parallel",)),
        name="out_proj",
    )(x, hm, ha, w_out, w_out)


def _ffn_kernel(x_ref, xn_ref, g_ref, wu_ref, wd_ref, o_ref, *rest):
    h_scr = rest[-1]
    i, f, nf = pl.program_id(0), pl.program_id(1), pl.num_programs(1)
    slot = i % 2

    @pl.when(jnp.logical_and(i == 0, f == 0))
    def _():
        h_scr[0] = _rms(x_ref[...], g_ref[...]).astype(BF16)

    @pl.when(f == 0)
    def _():
        o_ref[...] = x_ref[...]

    wu, wd = wu_ref[...].astype(BF16), wd_ref[...].astype(BF16)
    if len(rest) == 3:
        rest[0][...] = wu
        rest[1][...] = wd
    u = _dot(h_scr[slot], wu)
    a = jnp.square(jnp.maximum(u, 0.0)).astype(BF16)
    o_ref[...] += _dot(a, wd)

    part = x_ref.shape[0] // nf
    rows = pl.ds(pl.multiple_of(f * part, part), part)
    h_scr[1 - slot, rows, :] = _rms(xn_ref[rows, :], g_ref[...]).astype(BF16)


def _ffn(x, g_ffn, w_up, w_down, layer, emit_bf16=False):
    m, d = x.shape
    d_ff = w_up.shape[-1]
    tm = _row_tile(m, 512)
    tf = 512 if emit_bf16 else 1024
    assert not emit_bf16 or m == tm
    assert tm % (d_ff // tf) == 0 and (tm // (d_ff // tf)) % 16 == 0
    nxt = lambda i, f: (jnp.minimum(i + 1, m // tm - 1), 0)
    if emit_bf16:
        w_specs = [pl.BlockSpec((None, d, tf), lambda i, f: (layer, 0, f)),
                   pl.BlockSpec((None, tf, d), lambda i, f: (layer, f, 0))]
    else:
        w_specs = [pl.BlockSpec((d, tf), lambda i, f: (0, f)), pl.BlockSpec((tf, d), lambda i, f: (f, 0))]
    out_shape = [jax.ShapeDtypeStruct((m, d), F32)]
    out_specs = [pl.BlockSpec((tm, d), lambda i, f: (i, 0))]
    if emit_bf16:
        out_shape += [jax.ShapeDtypeStruct((d, d_ff), BF16), jax.ShapeDtypeStruct((d_ff, d), BF16)]
        out_specs += [pl.BlockSpec((d, tf), lambda i, f: (0, f)), pl.BlockSpec((tf, d), lambda i, f: (f, 0))]
    return pl.pallas_call(
        _ffn_kernel,
        out_shape=out_shape,
        grid=(m // tm, d_ff // tf),
        in_specs=[pl.BlockSpec((tm, d), lambda i, f: (i, 0)),
                  pl.BlockSpec((tm, d), nxt),
                  pl.BlockSpec((None, 1, d), lambda i, f: (layer, 0, 0))] + w_specs,
        out_specs=out_specs,
        scratch_shapes=[pltpu.VMEM((2, tm, d), BF16)],
        compiler_params=_params(("arbitrary", "arbitrary")),
        name="ffn_cast" if emit_bf16 else "ffn",
    )(x, x, g_ffn, w_up, w_down)


def _layer(x, wts, layer, lm, la, cs, bias, sink, ck, cv, c0, n0, m0, cache_len, ffn_w):
    b, s, d = x.shape
    x2 = x.reshape(b * s, d)
    eye = wts["consts"][1]
    if ck is None:
        hm, aproj, c_new, n_new, m_new = _proj_mlstm(x, wts["g_mix"], wts["w_all"], wts["gbias"], wts["g_mo"], eye,
                                                     c0, n0, m0, layer, lm)
        x2, k_new, v_new = _swa_out(aproj, 0, x2, hm.reshape(b * s, M_WIDTH), wts["w_out"], wts["g_q"], wts["g_k"],
                                    wts["g_ao"], bias, sink, wts["consts"], layer, la, cs)
    else:
        proj, gates = _in_proj(x2, wts["g_mix"], wts["w_all"], layer)
        proj = proj.reshape(b, s, PROJ_W)
        hm, c_new, n_new, m_new = _mlstm(proj, gates.reshape(b, s, GATE_W), wts["gbias"], wts["g_mo"], eye,
                                         c0, n0, m0, layer, lm)
        ha, k_new, v_new = _swa(proj, M_PROJ_W, ck, cv, wts["g_q"], wts["g_k"], wts["g_ao"], bias, sink, wts["consts"],
                                layer, la, cs, cache_len)
        x2 = _out_proj(x2, hm.reshape(b * s, M_WIDTH), ha.reshape(b * s, A_WIDTH), wts["w_out"], layer)
    if ffn_w is None:
        x2, *ffn_w = _ffn(x2, wts["g_ffn"], wts["w_up"], wts["w_down"], layer, emit_bf16=True)
    else:
        x2, = _ffn(x2, wts["g_ffn"], *ffn_w, layer)
    return x2.reshape(b, s, d), k_new, v_new, c_new, n_new, m_new, ffn_w


def _w_in_columns():
    o = np.cumsum((0, M_HEADS * M_DK, M_HEADS * M_DK, M_WIDTH, M_WIDTH, M_HEADS, M_HEADS, A_WIDTH, A_KV_WIDTH, A_KV_WIDTH))
    mq, mk, mv, mo, mi, mf, aq, ak, av = (int(v) for v in o[:9])
    runs = []
    for h in range(M_HEADS):
        runs += [(mq + h * M_DK, M_DK), (mk + h * M_DK, M_DK), (mv + h * M_DV, M_DV), (mo + h * M_DV, M_DV)]
    for g in range(A_GROUP):
        runs += [(aq + (kv * A_GROUP + g) * A_HEAD_DIM, A_HEAD_DIM) for kv in range(A_KV_HEADS)]
    return runs + [(ak, A_KV_WIDTH), (av, A_KV_WIDTH), (mi, M_HEADS), (mf, M_HEADS)]


def _w_in_layout_kernel(w_ref, o_ref):
    dst = 0
    for src, width in _w_in_columns():
        o_ref[:, dst:dst + width] = w_ref[:, src:src + width].astype(o_ref.dtype)
        dst += width
    o_ref[:, dst:] = jnp.zeros((o_ref.shape[0], o_ref.shape[1] - dst), o_ref.dtype)


def _w_in_layout(w_in):
    depth, d, d_in = w_in.shape
    tk = _row_tile(d, 256)
    return pl.pallas_call(
        _w_in_layout_kernel,
        out_shape=jax.ShapeDtypeStruct((depth, d, PROJ_W + GATE_W), BF16),
        grid=(depth, d // tk),
        in_specs=[pl.BlockSpec((None, tk, d_in), lambda l, i: (l, i, 0))],
        out_specs=pl.BlockSpec((None, tk, PROJ_W + GATE_W), lambda l, i: (l, i, 0)),
        compiler_params=_params(("parallel", "parallel")),
        name="w_in_layout",
    )(w_in)


def _head_consts():
    hd = np.arange(A_KV_WIDTH) // A_HEAD_DIM
    e4 = (hd[:, None] == hd[None, :]).astype(np.float32)
    eye = np.eye(A_KV_WIDTH, dtype=np.float32)
    return tuple(jnp.asarray(a, BF16) for a in (e4, eye))


def kernel(x_prompt, x_sample, cache_k, cache_v, state_C, state_n, state_m, rel_bias, g_mix, w_in, b_i, b_f, g_q, g_k, sinks, g_mo, g_ao, w_out, g_ffn, w_up, w_down):
    depth = w_in.shape[0]
    bp, sp, d = x_prompt.shape
    bs, ls, _ = x_sample.shape
    n_win = cache_k.shape[2]
    assert sp % CHUNK == 0 and n_win == WINDOW and ls % 16 == 0 and ls <= CHUNK
    lm = M_CHUNK if sp % M_CHUNK == 0 else CHUNK

    w_all = _w_in_layout(w_in)
    gbias = jnp.concatenate([b_i, b_f, jnp.zeros((depth, GATE_W - 2 * M_HEADS), F32)], axis=-1)

    def regroup(a):
        rest = a.shape[2:]
        a = a.reshape(depth, A_KV_HEADS, A_GROUP, A_HEAD_DIM, *rest)
        return jnp.swapaxes(a, 1, 2).reshape(depth, A_WIDTH, *rest)

    w_out_b = jnp.concatenate([w_out[:, :M_WIDTH], regroup(w_out[:, M_WIDTH:])], axis=1).astype(BF16)
    wts = {
        "g_mix": g_mix.reshape(depth, 1, d), "w_all": w_all,
        "gbias": gbias.reshape(depth, 1, GATE_W), "g_mo": g_mo.reshape(depth, 1, M_WIDTH),
        "g_q": jnp.tile(g_q * A_SCALE, (1, A_HEADS)).reshape(depth, 1, A_WIDTH),
        "g_k": jnp.tile(g_k, (1, A_KV_HEADS)).reshape(depth, 1, A_KV_WIDTH),
        "g_ao": regroup(g_ao).reshape(depth, 1, A_WIDTH),
        "w_out": w_out_b, "g_ffn": g_ffn.reshape(depth, 1, d),
        "w_up": w_up, "w_down": w_down,
        "consts": _head_consts(),
    }

    bias = _bias_table(rel_bias)

    def bias_t(lq, cache_len):
        lk = WINDOW + lq
        t = bias[:, :lq, :lk].reshape(A_KV_HEADS, A_GROUP, lq, lk)
        t = t.transpose(0, 3, 1, 2).reshape(A_KV_HEADS, lk, A_GROUP * lq)
        n_prev = np.arange((WINDOW - cache_len) // lq + 1) * lq + cache_len
        dead = np.arange(lk)[None, :] < WINDOW - n_prev[:, None]
        return t[None] + jnp.asarray(np.where(dead, NEG_INF, 0.0)[:, None, :, None], F32)

    def sink_rows(lq):
        return jnp.repeat(sinks.reshape(depth, A_KV_HEADS, 1, A_GROUP), lq, axis=-1)

    zc =jnp.zeros((bp, M_HEADS, M_DV, M_DK), F32)
    zn = jnp.zeros((bp, M_HEADS, M_DK), F32)
    zm = jnp.zeros((bp, 1, M_HEADS), F32)

    xp, xs = x_prompt, x_sample
    bias_p, bias_s, sink_p, sink_s = bias_t(CHUNK, 0), bias_t(ls, WINDOW), sink_rows(CHUNK), sink_rows(ls)
    cs = A_CHUNKS_PER_STEP if sp % (A_CHUNKS_PER_STEP * CHUNK) == 0 else 1
    outs_p, outs_s = [], []
    for l in range(depth):
        xs, *st, ffn_w = _layer(xs, wts, l, ls, ls, 1, bias_s, sink_s,
                                cache_k[l].reshape(bs, n_win, A_KV_WIDTH), cache_v[l].reshape(bs, n_win, A_KV_WIDTH),
                                state_C[l], state_n[l], state_m[l].reshape(bs, 1, M_HEADS), WINDOW, None)
        outs_s.append(st)
        xp, *st, _ = _layer(xp, wts, l, lm, CHUNK, cs, bias_p, sink_p, None, None, zc, zn, zm, 0, ffn_w)
        outs_p.append(st)

    def stack(outs, b):
        k, v, c, n, m = (jnp.stack([o_[i] for o_ in outs]) for i in range(5))
        return (k.reshape(depth, b, WINDOW, A_KV_HEADS, A_HEAD_DIM), v.reshape(depth, b, WINDOW, A_KV_HEADS, A_HEAD_DIM),
                c, n, m.reshape(depth, b, M_HEADS))

    return (xp, xs) + stack(outs_p, bp) + stack(outs_s, bs)
```

```python
import functools

import jax
import jax.numpy as jnp
import numpy as np
from jax import lax
from jax.experimental import pallas as pl
from jax.experimental.pallas import tpu as pltpu

F32 = jnp.float32
BF16 = jnp.bfloat16

CHUNK = 64
M_CHUNK = 256
A_CHUNKS_PER_STEP = 4
M_HEADS = 4
M_DK = 128
M_DV = 256
M_WIDTH = M_HEADS * M_DV
A_HEADS = 16
A_KV_HEADS = 4
A_HEAD_DIM = 64
A_GROUP = A_HEADS // A_KV_HEADS
A_WIDTH = A_HEADS * A_HEAD_DIM
A_KV_WIDTH = A_KV_HEADS * A_HEAD_DIM
WINDOW = 128
N_BUCKETS = 32
MAX_DISTANCE = 128
EPS = 1e-6
NEG_INF = -1e30
M_SCALE = M_DK ** -0.5
A_SCALE = A_HEAD_DIM ** -0.5

HEAD_W = 2 * M_DK + 2 * M_DV
M_PROJ_W = M_HEADS * HEAD_W
A_PROJ_W = A_WIDTH + 2 * A_KV_WIDTH
PROJ_W = M_PROJ_W + A_PROJ_W
GATE_W = 128

V7X_VMEM_LIMIT = 56 * 1024 * 1024

NT_DIMS = (((1,), (1,)), ((), ()))


def _params(sem, vmem=V7X_VMEM_LIMIT):
    return pltpu.CompilerParams(dimension_semantics=sem, vmem_limit_bytes=vmem)


def _row_tile(m, cap):
    t = min(m, cap)
    while m % t:
        t //= 2
    return t


def _dot(a, b):
    return jnp.dot(a, b, preferred_element_type=F32)


def _dot_nt(a, b):
    return lax.dot_general(a, b, NT_DIMS, preferred_element_type=F32)


def _t5_bucket(rel):
    half = N_BUCKETS // 2
    exact = half // 2
    n = np.abs(rel)
    large = exact + (np.log(np.maximum(n, 1) / exact) / np.log(MAX_DISTANCE / exact) * (half - exact)).astype(np.int32)
    large = np.minimum(large, half - 1)
    return (rel > 0).astype(np.int32) * half + np.where(n < exact, n, large).astype(np.int32)


def _bias_kernel(rel_ref, map_ref, out_ref):
    bmap = map_ref[...]
    for h in range(A_HEADS):
        acc = jnp.zeros(bmap.shape, F32)
        for b in range(N_BUCKETS):
            acc = jnp.where(bmap == b, rel_ref[b, h], acc)
        out_ref[h] = acc


def _bias_table(rel_bias):
    lk = WINDOW + CHUNK
    rel = (np.arange(lk)[None, :] - WINDOW) - np.arange(CHUNK)[:, None]
    bmap = jnp.asarray(_t5_bucket(rel), jnp.int32)
    return pl.pallas_call(
        _bias_kernel,
        out_shape=jax.ShapeDtypeStruct((A_HEADS, CHUNK, lk), F32),
        in_specs=[pl.BlockSpec(memory_space=pltpu.SMEM),
                  pl.BlockSpec(memory_space=pltpu.VMEM)],
        out_specs=pl.BlockSpec(memory_space=pltpu.VMEM),
        name="t5_bias_table",
    )(rel_bias, bmap)


def _rms(x, g):
    return x * lax.rsqrt(jnp.mean(x * x, axis=-1, keepdims=True) + EPS) * g


def _in_proj_kernel(x_ref, g_ref, w_ref, o_ref, gate_ref, *, sub):
    for r in range(0, x_ref.shape[0], sub):
        h = _rms(x_ref[r:r + sub, :], g_ref[...]).astype(BF16)
        acc = _dot(h, w_ref[...])
        o_ref[r:r + sub, :] = acc[:, :PROJ_W].astype(o_ref.dtype)
        gate_ref[r:r + sub, :] = acc[:, PROJ_W:]


def _in_proj(x, g_mix, w_all, layer):
    m, d = x.shape
    tm = _row_tile(m, 512)
    n = PROJ_W + GATE_W
    return pl.pallas_call(
        functools.partial(_in_proj_kernel, sub=min(tm, 256)),
        out_shape=(jax.ShapeDtypeStruct((m, PROJ_W), BF16), jax.ShapeDtypeStruct((m, GATE_W), F32)),
        grid=(m // tm,),
        in_specs=[pl.BlockSpec((tm, d), lambda i: (i, 0)),
                  pl.BlockSpec((None, 1, d), lambda i: (layer, 0, 0)),
                  pl.BlockSpec((None, d, n), lambda i: (layer, 0, 0), pipeline_mode=pl.Buffered(1))],
        out_specs=(pl.BlockSpec((tm, PROJ_W), lambda i: (i, 0)),
                   pl.BlockSpec((tm, GATE_W), lambda i: (i, 0))),
        compiler_params=_params(("parallel",)),
        name="in_proj",
    )(x, g_mix, w_all)


def _mlstm_body(head_inputs, gates, gbias_ref, gmo_ref, eye_ref, c0_ref, n0_ref, m0_ref,
                hm_ref, cout_ref, nout_ref, mout_ref, c_scr, n_scr, m_scr, *, L):
    c = pl.program_id(1)
    last = pl.num_programs(1) - 1

    @pl.when(c == 0)
    def _():
        c_scr[...] = c0_ref[0]
        n_scr[...] = n0_ref[0]
        for h in range(M_HEADS):
            m_scr[h:h + 1, :] = jnp.broadcast_to(m0_ref[0, :, h:h + 1], (1, 128))

    a_all = gates + gbias_ref[...]
    logf = jnp.minimum(a_all, 0.0) - jnp.log(1.0 + jnp.exp(-jnp.abs(a_all)))
    row = lax.broadcasted_iota(jnp.int32, (L, GATE_W), 0)
    b_all = logf
    k = 1
    while k < L:
        b_all = b_all + jnp.where(row >= k, pltpu.roll(b_all, k, axis=0), 0.0)
        k *= 2
    b_sh = pltpu.roll(b_all, GATE_W - M_HEADS, axis=1)
    a_i = a_all - b_sh
    cmax = a_i
    k = 1
    while k < L:
        cmax = jnp.maximum(cmax, jnp.where(row >= k, pltpu.roll(cmax, k, axis=0), NEG_INF))
        k *= 2
    a_t = a_i.T
    tri = lax.broadcasted_iota(jnp.int32, (L, L), 0) >= lax.broadcasted_iota(jnp.int32, (L, L), 1)
    eye = eye_ref[...]

    for h in range(M_HEADS):
        b_col = b_sh[:, h:h + 1]
        a_col = a_i[:, h:h + 1]
        a_row = a_t[h:h + 1, :]
        m_prev = m_scr[h:h + 1, 0:1]
        m_run = jnp.maximum(cmax[:, h:h + 1], m_prev)
        decay_w = jnp.exp(jnp.where(tri, a_row - m_run, NEG_INF))
        qh, kh, vh, og = head_inputs(h)
        w = decay_w * (_dot_nt(qh, kh) * M_SCALE)
        g = jnp.exp(m_prev - m_run)
        cst = c_scr[h]
        n_row = n_scr[h:h + 1, :]
        num = _dot(w.astype(BF16), vh) + g * _dot_nt(qh, cst.astype(BF16))
        den = (jnp.sum(w, axis=1, keepdims=True)
               + g * jnp.sum(qh.astype(F32) * n_row, axis=1, keepdims=True))
        inv = 1.0 / jnp.maximum(jnp.abs(den), jnp.exp(-(b_col + m_run)))
        ms = jnp.mean(num * num, axis=1, keepdims=True)
        scale = inv * lax.rsqrt(inv * inv * ms + EPS)
        hm_ref[0, :, h * M_DV:(h + 1) * M_DV] = (
            num * scale * gmo_ref[:, h * M_DV:(h + 1) * M_DV] * jax.nn.sigmoid(og)).astype(hm_ref.dtype)

        b_last = b_col[L - 1:L, :]
        m_end = b_last + m_run[L - 1:L, :]
        wk = jnp.exp(b_last + a_col - m_end)
        decay = jnp.exp(b_last + m_prev - m_end)
        vw = (vh.astype(F32) * wk).astype(BF16)
        vw_t = _dot_nt(eye, vw).astype(BF16)
        c_new = decay * cst + M_SCALE * _dot(vw_t, kh)
        n_new = decay * n_row + M_SCALE * jnp.sum(kh.astype(F32) * wk, axis=0, keepdims=True)
        c_scr[h] = c_new
        n_scr[h:h + 1, :] = n_new
        m_scr[h:h + 1, :] = jnp.broadcast_to(m_end, (1, 128))

        @pl.when(c == last)
        def _():
            cout_ref[0, h] = c_new
            nout_ref[0, h:h + 1, :] = n_new
            mout_ref[0, :, h:h + 1] = m_end


def _mlstm_kernel(p_ref, gate_ref, *rest, L):
    def head_inputs(h):
        o = h * HEAD_W
        return (p_ref[0, :, o:o + M_DK], p_ref[0, :, o + M_DK:o + 2 * M_DK],
                p_ref[0, :, o + 2 * M_DK:o + 2 * M_DK + M_DV], p_ref[0, :, o + 2 * M_DK + M_DV:o + HEAD_W].astype(F32))

    _mlstm_body(head_inputs, gate_ref[0], *rest, L=L)


def _proj_mlstm_kernel(x_ref, g_ref, w_ref, gbias_ref, gmo_ref, eye_ref, c0_ref, n0_ref, m0_ref,
                       hm_ref, ap_ref, cout_ref, nout_ref, mout_ref, c_scr, n_scr, m_scr, *, L):
    hn = _rms(x_ref[0], g_ref[...]).astype(BF16)

    def head_inputs(h):
        o = h * HEAD_W
        ph = _dot(hn, w_ref[:, o:o + HEAD_W])
        return (ph[:, :M_DK].astype(BF16), ph[:, M_DK:2 * M_DK].astype(BF16),
                ph[:, 2 * M_DK:2 * M_DK + M_DV].astype(BF16), ph[:, 2 * M_DK + M_DV:])

    _mlstm_body(head_inputs, _dot(hn, w_ref[:, PROJ_W:]), gbias_ref, gmo_ref, eye_ref, c0_ref, n0_ref, m0_ref,
                hm_ref, cout_ref, nout_ref, mout_ref, c_scr, n_scr, m_scr, L=L)
    ap_ref[0] = _dot(hn, w_ref[:, M_PROJ_W:PROJ_W]).astype(ap_ref.dtype)


def _mlstm_specs(b, layer, L):
    state = [pl.BlockSpec((1, M_HEADS, M_DV, M_DK), lambda i, c: (i, 0, 0, 0)),
             pl.BlockSpec((1, M_HEADS, M_DK), lambda i, c: (i, 0, 0)),
             pl.BlockSpec((1, 1, M_HEADS), lambda i, c: (i, 0, 0))]
    params = [pl.BlockSpec((None, 1, GATE_W), lambda i, c: (layer, 0, 0)),
              pl.BlockSpec((None, 1, M_WIDTH), lambda i, c: (layer, 0, 0)),
              pl.BlockSpec((M_DV, M_DV), lambda i, c: (0, 0))]
    state_shapes = [jax.ShapeDtypeStruct((b, M_HEADS, M_DV, M_DK), F32),
                    jax.ShapeDtypeStruct((b, M_HEADS, M_DK), F32),
                    jax.ShapeDtypeStruct((b, 1, M_HEADS), F32)]
    scratch = [pltpu.VMEM((M_HEADS, M_DV, M_DK), F32), pltpu.VMEM((M_HEADS, M_DK), F32), pltpu.VMEM((M_HEADS, 128), F32)]
    return params + state, state, state_shapes, scratch


def _mlstm(proj, gates, gbias, g_mo, eye, c0, n0, m0, layer, L):
    b, s, _ = proj.shape
    ins, state_out, state_shapes, scratch = _mlstm_specs(b, layer, L)
    return pl.pallas_call(
        functools.partial(_mlstm_kernel, L=L),
        out_shape=[jax.ShapeDtypeStruct((b, s, M_WIDTH), BF16)] + state_shapes,
        grid=(b, s // L),
        in_specs=[pl.BlockSpec((1, L, M_PROJ_W), lambda i, c: (i, c, 0)),
                  pl.BlockSpec((1, L, GATE_W), lambda i, c: (i, c, 0))] + ins,
        out_specs=[pl.BlockSpec((1, L, M_WIDTH), lambda i, c: (i, c, 0))] + state_out,
        scratch_shapes=scratch,
        compiler_params=_params(("arbitrary", "arbitrary")),
        name="mlstm",
    )(proj, gates, gbias, g_mo, eye, c0, n0, m0)


def _proj_mlstm(x, g_mix, w_all, gbias, g_mo, eye, c0, n0, m0, layer, L):
    b, s, d = x.shape
    ins, state_out, state_shapes, scratch = _mlstm_specs(b, layer, L)
    return pl.pallas_call(
        functools.partial(_proj_mlstm_kernel, L=L),
        out_shape=[jax.ShapeDtypeStruct((b, s, M_WIDTH), BF16), jax.ShapeDtypeStruct((b, s, A_PROJ_W), BF16)] + state_shapes,
        grid=(b, s // L),
        in_specs=[pl.BlockSpec((1, L, d), lambda i, c: (i, c, 0)),
                  pl.BlockSpec((None, 1, d), lambda i, c: (layer, 0, 0)),
                  pl.BlockSpec((None, d, PROJ_W + GATE_W), lambda i, c: (layer, 0, 0), pipeline_mode=pl.Buffered(1))] + ins,
        out_specs=[pl.BlockSpec((1, L, M_WIDTH), lambda i, c: (i, c, 0)),
                   pl.BlockSpec((1, L, A_PROJ_W), lambda i, c: (i, c, 0))] + state_out,
        scratch_shapes=scratch,
        compiler_params=_params(("arbitrary", "arbitrary")),
        name="proj_mlstm",
    )(x, g_mix, w_all, gbias, g_mo, eye, c0, n0, m0)


def _swa_body(c, is_last, q_ref, k_ref, v_ref, ck_ref, cv_ref, gq_ref, gk_ref, gao_ref, bias_ref, sink_ref,
              e4_ref, eye_ref, cko_ref, cvo_ref, khist, vhist, write_ha, *, L, CS, cache_len, side_work=None):
    T = CS * L
    lk = WINDOW + L
    gl = A_GROUP * L
    kv_lane = lax.broadcasted_iota(jnp.int32, (1, A_KV_WIDTH), 1) // A_HEAD_DIM

    def lane_masked(kn):
        knb = kn.astype(BF16)
        return [jnp.where(kv_lane == kv, knb, 0.0) for kv in range(A_KV_HEADS)]

    @pl.when(c == 0)
    def _():
        khist[...] = ck_ref[0]
        vhist[...] = cv_ref[0]

    q = q_ref[0].astype(F32)
    k = k_ref[0].astype(F32)
    sq = jnp.concatenate([q[:, g * A_KV_WIDTH:(g + 1) * A_KV_WIDTH] for g in range(A_GROUP)] + [k], axis=0)
    sq = sq * sq
    sq_hi = sq.astype(BF16)
    sq_lo = (sq - sq_hi.astype(F32)).astype(BF16)
    ss = _dot(jnp.concatenate([sq_hi, sq_lo], axis=0), e4_ref[...])
    rinv = lax.rsqrt((ss[:5 * T] + ss[5 * T:]) * (1.0 / A_HEAD_DIM) + EPS)
    qn = [(q[:, g * A_KV_WIDTH:(g + 1) * A_KV_WIDTH] * rinv[g * T:(g + 1) * T, :]
           * gq_ref[:, g * A_KV_WIDTH:(g + 1) * A_KV_WIDTH]).astype(BF16) for g in range(A_GROUP)]
    kn = k * rinv[A_GROUP * T:, :] * gk_ref[...]

    k_all = jnp.concatenate([khist[...], kn], axis=0)
    v_all = jnp.concatenate([vhist[...], v_ref[0].astype(F32)], axis=0)
    vb_all = v_all.astype(BF16)

    for j in range(CS):
        rows = slice(j * L, j * L + lk)
        qst = jnp.concatenate([qn[g][j * L:(j + 1) * L, :] for g in range(A_GROUP)], axis=0)
        km = jnp.concatenate(lane_masked(k_all[rows]), axis=0)
        s_all = _dot_nt(km, qst)
        v_t = _dot_nt(eye_ref[...], vb_all[rows]).astype(BF16)
        var = jnp.minimum((c * CS + j) * L, WINDOW - cache_len) // L
        o_t = []
        if side_work is not None:
            side_work(j, CS)
        for kv in range(A_KV_HEADS):
            s = s_all[kv * lk:(kv + 1) * lk, :] + bias_ref[var, kv]
            sk = sink_ref[kv]
            mx = jnp.maximum(jnp.max(s, axis=0, keepdims=True), sk)
            p = jnp.exp(s - mx)
            den = jnp.sum(p, axis=0, keepdims=True) + jnp.exp(sk - mx)
            o = _dot(v_t[kv * A_HEAD_DIM:(kv + 1) * A_HEAD_DIM, :], p.astype(BF16))
            o_t.append(o * (1.0 / den))
        o_all = jnp.concatenate(o_t, axis=0).T
        ha = jnp.concatenate([o_all[g * L:(g + 1) * L, :] for g in range(A_GROUP)], axis=1)
        write_ha(j, _rms(ha, gao_ref[...]).astype(BF16))

    k_win = k_all[T:T + WINDOW]
    v_win = v_all[T:T + WINDOW]
    khist[...] = k_win
    vhist[...] = v_win

    @pl.when(is_last)
    def _():
        cko_ref[0] = k_win
        cvo_ref[0] = v_win


def _swa_kernel(*refs, L, CS, cache_len):
    ins, (ha_ref, cko_ref, cvo_ref, khist, vhist) = refs[:12], refs[12:]

    def write_ha(j, rows):
        ha_ref[0, j * L:(j + 1) * L, :] = rows

    _swa_body(pl.program_id(1), pl.program_id(1) == pl.num_programs(1) - 1, *ins, cko_ref, cvo_ref, khist, vhist,
              write_ha, L=L, CS=CS, cache_len=cache_len)


def _swa_out_kernel(*refs, L, CS, nc, n):
    ins, (x_ref, hm_ref, wm_ref, wa_ref), (o_ref, cko_ref, cvo_ref, khist, vhist, ha_prev) = (
        refs[:12], refs[12:16], refs[16:])
    s = pl.program_id(0)
    c = jnp.minimum(s, n - 1) % nc

    @pl.when(s == 0)
    def _():
        ha_prev[...] = jnp.zeros(ha_prev.shape, ha_prev.dtype)

    hm = hm_ref[...]
    ha_old = ha_prev[...]

    def project(j, parts):
        w = o_ref.shape[1] // parts
        cols = slice(j * w, (j + 1) * w)
        o_ref[:, cols] = x_ref[:, cols] + _dot(hm, wm_ref[:, cols]) + _dot(ha_old, wa_ref[:, cols])

    def write_ha(j, rows):
        ha_prev[j * L:(j + 1) * L, :] = rows

    _swa_body(c, jnp.logical_and(s < n, c == nc - 1), *ins, cko_ref, cvo_ref, khist, vhist,
              write_ha, L=L, CS=CS, cache_len=0, side_work=project)


def _swa_out(proj, col0, x, hm, w_out, g_q, g_k, g_ao, bias, sink, consts, layer, L, CS):
    b, s, _ = proj.shape
    m, d = x.shape
    T = CS * L
    nc = s // T
    n = b * nc
    lk = WINDOW + L
    gl = A_GROUP * L
    nvar = bias.shape[0]
    q_blk, kv_blk = col0 // A_WIDTH, (col0 + A_WIDTH) // A_KV_WIDTH
    e4, eye = consts
    zk = jnp.zeros((b, WINDOW, A_KV_WIDTH), F32)
    cur = lambda s_: jnp.minimum(s_, n - 1)
    prev = lambda s_: jnp.maximum(s_ - 1, 0)
    full = lambda *shape: pl.BlockSpec(shape, lambda s_: (0,) * len(shape))
    out_shape = (jax.ShapeDtypeStruct((m, d), F32),
                 jax.ShapeDtypeStruct((b, WINDOW, A_KV_WIDTH), F32),
                 jax.ShapeDtypeStruct((b, WINDOW, A_KV_WIDTH), F32))
    return pl.pallas_call(
        functools.partial(_swa_out_kernel, L=L, CS=CS, nc=nc, n=n),
        out_shape=out_shape,
        grid=(n + 1,),
        in_specs=[pl.BlockSpec((1, T, A_WIDTH), lambda s_: (cur(s_) // nc, cur(s_) % nc, q_blk)),
                  pl.BlockSpec((1, T, A_KV_WIDTH), lambda s_: (cur(s_) // nc, cur(s_) % nc, kv_blk)),
                  pl.BlockSpec((1, T, A_KV_WIDTH), lambda s_: (cur(s_) // nc, cur(s_) % nc, kv_blk + 1)),
                  pl.BlockSpec((1, WINDOW, A_KV_WIDTH), lambda s_: (cur(s_) // nc, 0, 0)),
                  pl.BlockSpec((1, WINDOW, A_KV_WIDTH), lambda s_: (cur(s_) // nc, 0, 0)),
                  pl.BlockSpec((None, 1, A_WIDTH), lambda s_: (layer, 0, 0)),
                  pl.BlockSpec((None, 1, A_KV_WIDTH), lambda s_: (layer, 0, 0)),
                  pl.BlockSpec((None, 1, A_WIDTH), lambda s_: (layer, 0, 0)),
                  full(nvar, A_KV_HEADS, lk, gl),
                  pl.BlockSpec((None, A_KV_HEADS, 1, gl), lambda s_: (layer, 0, 0, 0)),
                  full(A_KV_WIDTH, A_KV_WIDTH),
                  full(A_KV_WIDTH, A_KV_WIDTH),
                  pl.BlockSpec((T, d), lambda s_: (prev(s_), 0)),
                  pl.BlockSpec((T, M_WIDTH), lambda s_: (prev(s_), 0)),
                  pl.BlockSpec((None, M_WIDTH, d), lambda s_: (layer, 0, 0), pipeline_mode=pl.Buffered(1)),
                  pl.BlockSpec((None, A_WIDTH, d), lambda s_: (layer, 1, 0), pipeline_mode=pl.Buffered(1))],
        out_specs=(pl.BlockSpec((T, d), lambda s_: (prev(s_), 0)),
                   pl.BlockSpec((1, WINDOW, A_KV_WIDTH), lambda s_: (cur(s_) // nc, 0, 0)),
                   pl.BlockSpec((1, WINDOW, A_KV_WIDTH), lambda s_: (cur(s_) // nc, 0, 0))),
        scratch_shapes=[pltpu.VMEM((WINDOW, A_KV_WIDTH), F32),
                        pltpu.VMEM((WINDOW, A_KV_WIDTH), F32),
                        pltpu.VMEM((T, A_WIDTH), BF16)],
        compiler_params=_params(("arbitrary",)),
        name="swa_out",
    )(proj, proj, proj, zk, zk, g_q, g_k, g_ao, bias, sink, e4, eye, x, hm, w_out, w_out)


def _swa(proj, col0, ck, cv, g_q, g_k, g_ao, bias, sink, consts, layer, L, CS, cache_len):
    b, s, _ = proj.shape
    T = CS * L
    lk = WINDOW + L
    gl = A_GROUP * L
    nvar = bias.shape[0]
    q_blk, kv_blk = col0 // A_WIDTH, (col0 + A_WIDTH) // A_KV_WIDTH
    e4, eye = consts
    out_shape = (jax.ShapeDtypeStruct((b, s, A_WIDTH), BF16),
                 jax.ShapeDtypeStruct((b, WINDOW, A_KV_WIDTH), F32),
                 jax.ShapeDtypeStruct((b, WINDOW, A_KV_WIDTH), F32))
    full = lambda *shape: pl.BlockSpec(shape, lambda i, c: (0,) * len(shape))
    return pl.pallas_call(
        functools.partial(_swa_kernel, L=L, CS=CS, cache_len=cache_len),
        out_shape=out_shape,
        grid=(b, s // T),
        in_specs=[pl.BlockSpec((1, T, A_WIDTH), lambda i, c: (i, c, q_blk)),
                  pl.BlockSpec((1, T, A_KV_WIDTH), lambda i, c: (i, c, kv_blk)),
                  pl.BlockSpec((1, T, A_KV_WIDTH), lambda i, c: (i, c, kv_blk + 1)),
                  pl.BlockSpec((1, WINDOW, A_KV_WIDTH), lambda i, c: (i, 0, 0)),
                  pl.BlockSpec((1, WINDOW, A_KV_WIDTH), lambda i, c: (i, 0, 0)),
                  pl.BlockSpec((None, 1, A_WIDTH), lambda i, c: (layer, 0, 0)),
                  pl.BlockSpec((None, 1, A_KV_WIDTH), lambda i, c: (layer, 0, 0)),
                  pl.BlockSpec((None, 1, A_WIDTH), lambda i, c: (layer, 0, 0)),
                  full(nvar, A_KV_HEADS, lk, gl),
                  pl.BlockSpec((None, A_KV_HEADS, 1, gl), lambda i, c: (layer, 0, 0, 0)),
                  full(A_KV_WIDTH, A_KV_WIDTH),
                  full(A_KV_WIDTH, A_KV_WIDTH)],
        out_specs=(pl.BlockSpec((1, T, A_WIDTH), lambda i, c: (i, c, 0)),
                   pl.BlockSpec((1, WINDOW, A_KV_WIDTH), lambda i, c: (i, 0, 0)),
                   pl.BlockSpec((1, WINDOW, A_KV_WIDTH), lambda i, c: (i, 0, 0))),
        scratch_shapes=[pltpu.VMEM((WINDOW, A_KV_WIDTH), F32),
                        pltpu.VMEM((WINDOW, A_KV_WIDTH), F32)],
        compiler_params=_params(("arbitrary", "arbitrary")),
        name="swa",
    )(proj, proj, proj, ck, cv, g_q, g_k, g_ao, bias, sink, e4, eye)


def _out_proj_kernel(x_ref, hm_ref, ha_ref, wm_ref, wa_ref, o_ref):
    o_ref[...] = x_ref[...] + _dot(hm_ref[...], wm_ref[...]) + _dot(ha_ref[...], wa_ref[...])


def _out_proj(x, hm, ha, w_out, layer):
    m, d = x.shape
    tm = _row_tile(m, 512)
    return pl.pallas_call(
        _out_proj_kernel,
        out_shape=jax.ShapeDtypeStruct((m, d), F32),
        grid=(m // tm,),
        in_specs=[pl.BlockSpec((tm, d), lambda i: (i, 0)),
                  pl.BlockSpec((tm, M_WIDTH), lambda i: (i, 0)),
                  pl.BlockSpec((tm, A_WIDTH), lambda i: (i, 0)),
                  pl.BlockSpec((None, M_WIDTH, d), lambda i: (layer, 0, 0)),
                  pl.BlockSpec((None, A_WIDTH, d), lambda i: (layer, 1, 0))],
        out_specs=pl.BlockSpec((tm, d), lambda i: (i, 0)),
        compiler_params=_params(("parallel",)),
        name="out_proj",
    )(x, hm, ha, w_out, w_out)


def _ffn_kernel(x_ref, g_ref, wu_ref, wd_ref, o_ref, *rest):
    h_scr = rest[-1]
    f = pl.program_id(1)

    @pl.when(f == 0)
    def _():
        x = x_ref[...]
        h_scr[...] = _rms(x, g_ref[...]).astype(BF16)
        o_ref[...] = x

    wu, wd = wu_ref[...].astype(BF16), wd_ref[...].astype(BF16)
    if len(rest) == 3:
        rest[0][...] = wu
        rest[1][...] = wd
    u = _dot(h_scr[...], wu)
    a = jnp.square(jnp.maximum(u, 0.0)).astype(BF16)
    o_ref[...] += _dot(a, wd)


def _ffn(x, g_ffn, w_up, w_down, layer, emit_bf16=False):
    m, d = x.shape
    d_ff = w_up.shape[-1]
    tm = _row_tile(m, 512)
    tf = 512 if emit_bf16 else 1024
    assert not emit_bf16 or m == tm
    if emit_bf16:
        w_specs = [pl.BlockSpec((None, d, tf), lambda i, f: (layer, 0, f)),
                   pl.BlockSpec((None, tf, d), lambda i, f: (layer, f, 0))]
    else:
        w_specs = [pl.BlockSpec((d, tf), lambda i, f: (0, f)), pl.BlockSpec((tf, d), lambda i, f: (f, 0))]
    out_shape = [jax.ShapeDtypeStruct((m, d), F32)]
    out_specs = [pl.BlockSpec((tm, d), lambda i, f: (i, 0))]
    if emit_bf16:
        out_shape += [jax.ShapeDtypeStruct((d, d_ff), BF16), jax.ShapeDtypeStruct((d_ff, d), BF16)]
        out_specs += [pl.BlockSpec((d, tf), lambda i, f: (0, f)), pl.BlockSpec((tf, d), lambda i, f: (f, 0))]
    return pl.pallas_call(
        _ffn_kernel,
        out_shape=out_shape,
        grid=(m // tm, d_ff // tf),
        in_specs=[pl.BlockSpec((tm, d), lambda i, f: (i, 0)),
                  pl.BlockSpec((None, 1, d), lambda i, f: (layer, 0, 0))] + w_specs,
        out_specs=out_specs,
        scratch_shapes=[pltpu.VMEM((tm, d), BF16)],
        compiler_params=_params(("parallel", "arbitrary")),
        name="ffn_cast" if emit_bf16 else "ffn",
    )(x, g_ffn, w_up, w_down)


def _layer(x, wts, layer, lm, la, cs, bias, sink, ck, cv, c0, n0, m0, cache_len, ffn_w):
    b, s, d = x.shape
    x2 = x.reshape(b * s, d)
    eye = wts["consts"][1]
    if ck is None:
        hm, aproj, c_new, n_new, m_new = _proj_mlstm(x, wts["g_mix"], wts["w_all"], wts["gbias"], wts["g_mo"], eye,
                                                     c0, n0, m0, layer, lm)
        x2, k_new, v_new = _swa_out(aproj, 0, x2, hm.reshape(b * s, M_WIDTH), wts["w_out"], wts["g_q"], wts["g_k"],
                                    wts["g_ao"], bias, sink, wts["consts"], layer, la, cs)
    else:
        proj, gates = _in_proj(x2, wts["g_mix"], wts["w_all"], layer)
        proj = proj.reshape(b, s, PROJ_W)
        hm, c_new, n_new, m_new = _mlstm(proj, gates.reshape(b, s, GATE_W), wts["gbias"], wts["g_mo"], eye,
                                         c0, n0, m0, layer, lm)
        ha, k_new, v_new = _swa(proj, M_PROJ_W, ck, cv, wts["g_q"], wts["g_k"], wts["g_ao"], bias, sink, wts["consts"],
                                layer, la, cs, cache_len)
        x2 = _out_proj(x2, hm.reshape(b * s, M_WIDTH), ha.reshape(b * s, A_WIDTH), wts["w_out"], layer)
    if ffn_w is None:
        x2, *ffn_w = _ffn(x2, wts["g_ffn"], wts["w_up"], wts["w_down"], layer, emit_bf16=True)
    else:
        x2, = _ffn(x2, wts["g_ffn"], *ffn_w, layer)
    return x2.reshape(b, s, d), k_new, v_new, c_new, n_new, m_new, ffn_w


def _w_in_columns():
    o = np.cumsum((0, M_HEADS * M_DK, M_HEADS * M_DK, M_WIDTH, M_WIDTH, M_HEADS, M_HEADS, A_WIDTH, A_KV_WIDTH, A_KV_WIDTH))
    mq, mk, mv, mo, mi, mf, aq, ak, av = (int(v) for v in o[:9])
    runs = []
    for h in range(M_HEADS):
        runs += [(mq + h * M_DK, M_DK), (mk + h * M_DK, M_DK), (mv + h * M_DV, M_DV), (mo + h * M_DV, M_DV)]
    for g in range(A_GROUP):
        runs += [(aq + (kv * A_GROUP + g) * A_HEAD_DIM, A_HEAD_DIM) for kv in range(A_KV_HEADS)]
    return runs + [(ak, A_KV_WIDTH), (av, A_KV_WIDTH), (mi, M_HEADS), (mf, M_HEADS)]


def _w_in_layout_kernel(w_ref, o_ref):
    dst = 0
    for src, width in _w_in_columns():
        o_ref[:, dst:dst + width] = w_ref[:, src:src + width].astype(o_ref.dtype)
        dst += width
    o_ref[:, dst:] = jnp.zeros((o_ref.shape[0], o_ref.shape[1] - dst), o_ref.dtype)


def _w_in_layout(w_in):
    depth, d, d_in = w_in.shape
    tk = _row_tile(d, 256)
    return pl.pallas_call(
        _w_in_layout_kernel,
        out_shape=jax.ShapeDtypeStruct((depth, d, PROJ_W + GATE_W), BF16),
        grid=(depth, d // tk),
        in_specs=[pl.BlockSpec((None, tk, d_in), lambda l, i: (l, i, 0))],
        out_specs=pl.BlockSpec((None, tk, PROJ_W + GATE_W), lambda l, i: (l, i, 0)),
        compiler_params=_params(("parallel", "parallel")),
        name="w_in_layout",
    )(w_in)


def _head_consts():
    hd = np.arange(A_KV_WIDTH) // A_HEAD_DIM
    e4 = (hd[:, None] == hd[None, :]).astype(np.float32)
    eye = np.eye(A_KV_WIDTH, dtype=np.float32)
    return tuple(jnp.asarray(a, BF16) for a in (e4, eye))


def kernel(x_prompt, x_sample, cache_k, cache_v, state_C, state_n, state_m, rel_bias, g_mix, w_in, b_i, b_f, g_q, g_k, sinks, g_mo, g_ao, w_out, g_ffn, w_up, w_down):
    depth = w_in.shape[0]
    bp, sp, d = x_prompt.shape
    bs, ls, _ = x_sample.shape
    n_win = cache_k.shape[2]
    assert sp % CHUNK == 0 and n_win == WINDOW and ls % 16 == 0 and ls <= CHUNK
    lm = M_CHUNK if sp % M_CHUNK == 0 else CHUNK

    w_all = _w_in_layout(w_in)
    gbias = jnp.concatenate([b_i, b_f, jnp.zeros((depth, GATE_W - 2 * M_HEADS), F32)], axis=-1)

    def regroup(a):
        rest = a.shape[2:]
        a = a.reshape(depth, A_KV_HEADS, A_GROUP, A_HEAD_DIM, *rest)
        return jnp.swapaxes(a, 1, 2).reshape(depth, A_WIDTH, *rest)

    w_out_b = jnp.concatenate([w_out[:, :M_WIDTH], regroup(w_out[:, M_WIDTH:])], axis=1).astype(BF16)
    wts = {
        "g_mix": g_mix.reshape(depth, 1, d), "w_all": w_all,
        "gbias": gbias.reshape(depth, 1, GATE_W), "g_mo": g_mo.reshape(depth, 1, M_WIDTH),
        "g_q": jnp.tile(g_q * A_SCALE, (1, A_HEADS)).reshape(depth, 1, A_WIDTH),
        "g_k": jnp.tile(g_k, (1, A_KV_HEADS)).reshape(depth, 1, A_KV_WIDTH),
        "g_ao": regroup(g_ao).reshape(depth, 1, A_WIDTH),
        "w_out": w_out_b, "g_ffn": g_ffn.reshape(depth, 1, d),
        "w_up": w_up, "w_down": w_down,
        "consts": _head_consts(),
    }

    bias = _bias_table(rel_bias)

    def bias_t(lq, cache_len):
        lk = WINDOW + lq
        t = bias[:, :lq, :lk].reshape(A_KV_HEADS, A_GROUP, lq, lk)
        t = t.transpose(0, 3, 1, 2).reshape(A_KV_HEADS, lk, A_GROUP * lq)
        n_prev = np.arange((WINDOW - cache_len) // lq + 1) * lq + cache_len
        dead = np.arange(lk)[None, :] < WINDOW - n_prev[:, None]
        return t[None] + jnp.asarray(np.where(dead, NEG_INF, 0.0)[:, None, :, None], F32)

    def sink_rows(lq):
        return jnp.repeat(sinks.reshape(depth, A_KV_HEADS, 1, A_GROUP), lq, axis=-1)

    zc =jnp.zeros((bp, M_HEADS, M_DV, M_DK), F32)
    zn = jnp.zeros((bp, M_HEADS, M_DK), F32)
    zm = jnp.zeros((bp, 1, M_HEADS), F32)

    xp, xs = x_prompt, x_sample
    bias_p, bias_s, sink_p, sink_s = bias_t(CHUNK, 0), bias_t(ls, WINDOW), sink_rows(CHUNK), sink_rows(ls)
    cs = A_CHUNKS_PER_STEP if sp % (A_CHUNKS_PER_STEP * CHUNK) == 0 else 1
    outs_p, outs_s = [], []
    for l in range(depth):
        xs, *st, ffn_w = _layer(xs, wts, l, ls, ls, 1, bias_s, sink_s,
                                cache_k[l].reshape(bs, n_win, A_KV_WIDTH), cache_v[l].reshape(bs, n_win, A_KV_WIDTH),
                                state_C[l], state_n[l], state_m[l].reshape(bs, 1, M_HEADS), WINDOW, None)
        outs_s.append(st)
        xp, *st, _ = _layer(xp, wts, l, lm, CHUNK, cs, bias_p, sink_p, None, None, zc, zn, zm, 0, ffn_w)
        outs_p.append(st)

    def stack(outs, b):
        k, v, c, n, m = (jnp.stack([o_[i] for o_ in outs]) for i in range(5))
        return (k.reshape(depth, b, WINDOW, A_KV_HEADS, A_HEAD_DIM), v.reshape(depth, b, WINDOW, A_KV_HEADS, A_HEAD_DIM),
                c, n, m.reshape(depth, b, M_HEADS))

    return (xp, xs) + stack(outs_p, bp) + stack(outs_s, bs)
```

```python
import functools

import jax
import jax.numpy as jnp
import numpy as np
from jax import lax
from jax.experimental import pallas as pl
from jax.experimental.pallas import tpu as pltpu

F32 = jnp.float32
BF16 = jnp.bfloat16

CHUNK = 64
M_CHUNK = 512
A_CHUNKS_PER_STEP = 4
M_HEADS = 4
M_DK = 128
M_DV = 256
M_WIDTH = M_HEADS * M_DV
A_HEADS = 16
A_KV_HEADS = 4
A_HEAD_DIM = 64
A_GROUP = A_HEADS // A_KV_HEADS
A_WIDTH = A_HEADS * A_HEAD_DIM
A_KV_WIDTH = A_KV_HEADS * A_HEAD_DIM
WINDOW = 128
N_BUCKETS = 32
MAX_DISTANCE = 128
EPS = 1e-6
NEG_INF = -1e30
M_SCALE = M_DK ** -0.5
A_SCALE = A_HEAD_DIM ** -0.5

HEAD_W = 2 * M_DK + 2 * M_DV
M_PROJ_W = M_HEADS * HEAD_W
A_PROJ_W = A_WIDTH + 2 * A_KV_WIDTH
PROJ_W = M_PROJ_W + A_PROJ_W
GATE_W = 128

V7X_VMEM_LIMIT = 56 * 1024 * 1024

NT_DIMS = (((1,), (1,)), ((), ()))


def _params(sem, vmem=V7X_VMEM_LIMIT):
    return pltpu.CompilerParams(dimension_semantics=sem, vmem_limit_bytes=vmem)


def _row_tile(m, cap):
    t = min(m, cap)
    while m % t:
        t //= 2
    return t


def _dot(a, b):
    return jnp.dot(a, b, preferred_element_type=F32)


def _dot_nt(a, b):
    return lax.dot_general(a, b, NT_DIMS, preferred_element_type=F32)


def _t5_bucket(rel):
    half = N_BUCKETS // 2
    exact = half // 2
    n = np.abs(rel)
    large = exact + (np.log(np.maximum(n, 1) / exact) / np.log(MAX_DISTANCE / exact) * (half - exact)).astype(np.int32)
    large = np.minimum(large, half - 1)
    return (rel > 0).astype(np.int32) * half + np.where(n < exact, n, large).astype(np.int32)


def _bias_kernel(rel_ref, map_ref, out_ref):
    bmap = map_ref[...]
    for h in range(A_HEADS):
        acc = jnp.zeros(bmap.shape, F32)
        for b in range(N_BUCKETS):
            acc = jnp.where(bmap == b, rel_ref[b, h], acc)
        out_ref[h] = acc


def _bias_table(rel_bias):
    lk = WINDOW + CHUNK
    rel = (np.arange(lk)[None, :] - WINDOW) - np.arange(CHUNK)[:, None]
    bmap = jnp.asarray(_t5_bucket(rel), jnp.int32)
    return pl.pallas_call(
        _bias_kernel,
        out_shape=jax.ShapeDtypeStruct((A_HEADS, CHUNK, lk), F32),
        in_specs=[pl.BlockSpec(memory_space=pltpu.SMEM),
                  pl.BlockSpec(memory_space=pltpu.VMEM)],
        out_specs=pl.BlockSpec(memory_space=pltpu.VMEM),
        name="t5_bias_table",
    )(rel_bias, bmap)


def _rms(x, g):
    return x * lax.rsqrt(jnp.mean(x * x, axis=-1, keepdims=True) + EPS) * g


def _in_proj_kernel(x_ref, g_ref, w_ref, o_ref, gate_ref, *, sub):
    for r in range(0, x_ref.shape[0], sub):
        h = _rms(x_ref[r:r + sub, :], g_ref[...]).astype(BF16)
        acc = _dot(h, w_ref[...])
        o_ref[r:r + sub, :] = acc[:, :PROJ_W].astype(o_ref.dtype)
        gate_ref[r:r + sub, :] = acc[:, PROJ_W:]


def _in_proj(x, g_mix, w_all, layer):
    m, d = x.shape
    tm = _row_tile(m, 512)
    n = PROJ_W + GATE_W
    return pl.pallas_call(
        functools.partial(_in_proj_kernel, sub=min(tm, 256)),
        out_shape=(jax.ShapeDtypeStruct((m, PROJ_W), BF16), jax.ShapeDtypeStruct((m, GATE_W), F32)),
        grid=(m // tm,),
        in_specs=[pl.BlockSpec((tm, d), lambda i: (i, 0)),
                  pl.BlockSpec((None, 1, d), lambda i: (layer, 0, 0)),
                  pl.BlockSpec((None, d, n), lambda i: (layer, 0, 0), pipeline_mode=pl.Buffered(1))],
        out_specs=(pl.BlockSpec((tm, PROJ_W), lambda i: (i, 0)),
                   pl.BlockSpec((tm, GATE_W), lambda i: (i, 0))),
        compiler_params=_params(("parallel",)),
        name="in_proj",
    )(x, g_mix, w_all)


def _mlstm_body(head_inputs, gates, gbias_ref, gmo_ref, eye_ref, c0_ref, n0_ref, m0_ref,
                hm_ref, cout_ref, nout_ref, mout_ref, c_scr, n_scr, m_scr, *, L):
    c = pl.program_id(1)
    last = pl.num_programs(1) - 1

    @pl.when(c == 0)
    def _():
        c_scr[...] = c0_ref[0]
        n_scr[...] = n0_ref[0]
        for h in range(M_HEADS):
            m_scr[h:h + 1, :] = jnp.broadcast_to(m0_ref[0, :, h:h + 1], (1, 128))

    a_all = gates + gbias_ref[...]
    logf = jnp.minimum(a_all, 0.0) - jnp.log(1.0 + jnp.exp(-jnp.abs(a_all)))
    row = lax.broadcasted_iota(jnp.int32, (L, GATE_W), 0)
    b_all = logf
    k = 1
    while k < L:
        b_all = b_all + jnp.where(row >= k, pltpu.roll(b_all, k, axis=0), 0.0)
        k *= 2
    b_sh = pltpu.roll(b_all, GATE_W - M_HEADS, axis=1)
    a_i = a_all - b_sh
    cmax = a_i
    k = 1
    while k < L:
        cmax = jnp.maximum(cmax, jnp.where(row >= k, pltpu.roll(cmax, k, axis=0), NEG_INF))
        k *= 2
    a_t = a_i.T
    tri = lax.broadcasted_iota(jnp.int32, (L, L), 0) >= lax.broadcasted_iota(jnp.int32, (L, L), 1)
    eye = eye_ref[...]

    for h in range(M_HEADS):
        b_col = b_sh[:, h:h + 1]
        a_col = a_i[:, h:h + 1]
        a_row = a_t[h:h + 1, :]
        m_prev = m_scr[h:h + 1, 0:1]
        m_run = jnp.maximum(cmax[:, h:h + 1], m_prev)
        decay_w = jnp.exp(jnp.where(tri, a_row - m_run, NEG_INF))
        qh, kh, vh, og = head_inputs(h)
        w = decay_w * (_dot_nt(qh, kh) * M_SCALE)
        g = jnp.exp(m_prev - m_run)
        cst = c_scr[h]
        n_row = n_scr[h:h + 1, :]
        num = _dot(w.astype(BF16), vh) + g * _dot_nt(qh, cst.astype(BF16))
        den = (jnp.sum(w, axis=1, keepdims=True)
               + g * jnp.sum(qh.astype(F32) * n_row, axis=1, keepdims=True))
        inv = 1.0 / jnp.maximum(jnp.abs(den), jnp.exp(-(b_col + m_run)))
        ms = jnp.mean(num * num, axis=1, keepdims=True)
        scale = inv * lax.rsqrt(inv * inv * ms + EPS)
        hm_ref[0, :, h * M_DV:(h + 1) * M_DV] = (
            num * scale * gmo_ref[:, h * M_DV:(h + 1) * M_DV] * jax.nn.sigmoid(og)).astype(hm_ref.dtype)

        b_last = b_col[L - 1:L, :]
        m_end = b_last + m_run[L - 1:L, :]
        wk = jnp.exp(b_last + a_col - m_end)
        decay = jnp.exp(b_last + m_prev - m_end)
        vw = (vh.astype(F32) * wk).astype(BF16)
        vw_t = _dot_nt(eye, vw).astype(BF16)
        c_new = decay * cst + M_SCALE * _dot(vw_t, kh)
        n_new = decay * n_row + M_SCALE * jnp.sum(kh.astype(F32) * wk, axis=0, keepdims=True)
        c_scr[h] = c_new
        n_scr[h:h + 1, :] = n_new
        m_scr[h:h + 1, :] = jnp.broadcast_to(m_end, (1, 128))

        @pl.when(c == last)
        def _():
            cout_ref[0, h] = c_new
            nout_ref[0, h:h + 1, :] = n_new
            mout_ref[0, :, h:h + 1] = m_end


def _mlstm_kernel(p_ref, gate_ref, *rest, L):
    def head_inputs(h):
        o = h * HEAD_W
        return (p_ref[0, :, o:o + M_DK], p_ref[0, :, o + M_DK:o + 2 * M_DK],
                p_ref[0, :, o + 2 * M_DK:o + 2 * M_DK + M_DV], p_ref[0, :, o + 2 * M_DK + M_DV:o + HEAD_W].astype(F32))

    _mlstm_body(head_inputs, gate_ref[0], *rest, L=L)


def _proj_mlstm_kernel(x_ref, g_ref, w_ref, gbias_ref, gmo_ref, eye_ref, c0_ref, n0_ref, m0_ref,
                       hm_ref, ap_ref, cout_ref, nout_ref, mout_ref, c_scr, n_scr, m_scr, *, L):
    hn = _rms(x_ref[0], g_ref[...]).astype(BF16)

    def head_inputs(h):
        o = h * HEAD_W
        ph = _dot(hn, w_ref[:, o:o + HEAD_W])
        return (ph[:, :M_DK].astype(BF16), ph[:, M_DK:2 * M_DK].astype(BF16),
                ph[:, 2 * M_DK:2 * M_DK + M_DV].astype(BF16), ph[:, 2 * M_DK + M_DV:])

    _mlstm_body(head_inputs, _dot(hn, w_ref[:, PROJ_W:]), gbias_ref, gmo_ref, eye_ref, c0_ref, n0_ref, m0_ref,
                hm_ref, cout_ref, nout_ref, mout_ref, c_scr, n_scr, m_scr, L=L)
    ap_ref[0] = _dot(hn, w_ref[:, M_PROJ_W:PROJ_W]).astype(ap_ref.dtype)


def _mlstm_specs(b, layer, L):
    state = [pl.BlockSpec((1, M_HEADS, M_DV, M_DK), lambda i, c: (i, 0, 0, 0)),
             pl.BlockSpec((1, M_HEADS, M_DK), lambda i, c: (i, 0, 0)),
             pl.BlockSpec((1, 1, M_HEADS), lambda i, c: (i, 0, 0))]
    params = [pl.BlockSpec((None, 1, GATE_W), lambda i, c: (layer, 0, 0)),
              pl.BlockSpec((None, 1, M_WIDTH), lambda i, c: (layer, 0, 0)),
              pl.BlockSpec((M_DV, M_DV), lambda i, c: (0, 0))]
    state_shapes = [jax.ShapeDtypeStruct((b, M_HEADS, M_DV, M_DK), F32),
                    jax.ShapeDtypeStruct((b, M_HEADS, M_DK), F32),
                    jax.ShapeDtypeStruct((b, 1, M_HEADS), F32)]
    scratch = [pltpu.VMEM((M_HEADS, M_DV, M_DK), F32), pltpu.VMEM((M_HEADS, M_DK), F32), pltpu.VMEM((M_HEADS, 128), F32)]
    return params + state, state, state_shapes, scratch


def _mlstm(proj, gates, gbias, g_mo, eye, c0, n0, m0, layer, L):
    b, s, _ = proj.shape
    ins, state_out, state_shapes, scratch = _mlstm_specs(b, layer, L)
    return pl.pallas_call(
        functools.partial(_mlstm_kernel, L=L),
        out_shape=[jax.ShapeDtypeStruct((b, s, M_WIDTH), BF16)] + state_shapes,
        grid=(b, s // L),
        in_specs=[pl.BlockSpec((1, L, M_PROJ_W), lambda i, c: (i, c, 0)),
                  pl.BlockSpec((1, L, GATE_W), lambda i, c: (i, c, 0))] + ins,
        out_specs=[pl.BlockSpec((1, L, M_WIDTH), lambda i, c: (i, c, 0))] + state_out,
        scratch_shapes=scratch,
        compiler_params=_params(("arbitrary", "arbitrary")),
        name="mlstm",
    )(proj, gates, gbias, g_mo, eye, c0, n0, m0)


def _proj_mlstm(x, g_mix, w_all, gbias, g_mo, eye, c0, n0, m0, layer, L):
    b, s, d = x.shape
    ins, state_out, state_shapes, scratch = _mlstm_specs(b, layer, L)
    return pl.pallas_call(
        functools.partial(_proj_mlstm_kernel, L=L),
        out_shape=[jax.ShapeDtypeStruct((b, s, M_WIDTH), BF16), jax.ShapeDtypeStruct((b, s, A_PROJ_W), BF16)] + state_shapes,
        grid=(b, s // L),
        in_specs=[pl.BlockSpec((1, L, d), lambda i, c: (i, c, 0)),
                  pl.BlockSpec((None, 1, d), lambda i, c: (layer, 0, 0)),
                  pl.BlockSpec((None, d, PROJ_W + GATE_W), lambda i, c: (layer, 0, 0), pipeline_mode=pl.Buffered(1))] + ins,
        out_specs=[pl.BlockSpec((1, L, M_WIDTH), lambda i, c: (i, c, 0)),
                   pl.BlockSpec((1, L, A_PROJ_W), lambda i, c: (i, c, 0))] + state_out,
        scratch_shapes=scratch,
        compiler_params=_params(("arbitrary", "arbitrary")),
        name="proj_mlstm",
    )(x, g_mix, w_all, gbias, g_mo, eye, c0, n0, m0)


def _swa_body(c, is_last, q_ref, k_ref, v_ref, ck_ref, cv_ref, gq_ref, gk_ref, gao_ref, bias_ref, sink_ref,
              e4_ref, eye_ref, cko_ref, cvo_ref, khist, vhist, write_ha, *, L, CS, cache_len, side_work=None):
    T = CS * L
    lk = WINDOW + L
    gl = A_GROUP * L
    kv_lane = lax.broadcasted_iota(jnp.int32, (1, A_KV_WIDTH), 1) // A_HEAD_DIM

    def lane_masked(kn):
        knb = kn.astype(BF16)
        return [jnp.where(kv_lane == kv, knb, 0.0) for kv in range(A_KV_HEADS)]

    @pl.when(c == 0)
    def _():
        khist[...] = ck_ref[0]
        vhist[...] = cv_ref[0]

    q = q_ref[0].astype(F32)
    k = k_ref[0].astype(F32)
    sq = jnp.concatenate([q[:, g * A_KV_WIDTH:(g + 1) * A_KV_WIDTH] for g in range(A_GROUP)] + [k], axis=0)
    sq = sq * sq
    sq_hi = sq.astype(BF16)
    sq_lo = (sq - sq_hi.astype(F32)).astype(BF16)
    ss = _dot(jnp.concatenate([sq_hi, sq_lo], axis=0), e4_ref[...])
    rinv = lax.rsqrt((ss[:5 * T] + ss[5 * T:]) * (1.0 / A_HEAD_DIM) + EPS)
    qn = [(q[:, g * A_KV_WIDTH:(g + 1) * A_KV_WIDTH] * rinv[g * T:(g + 1) * T, :]
           * gq_ref[:, g * A_KV_WIDTH:(g + 1) * A_KV_WIDTH]).astype(BF16) for g in range(A_GROUP)]
    kn = k * rinv[A_GROUP * T:, :] * gk_ref[...]

    k_all = jnp.concatenate([khist[...], kn], axis=0)
    v_all = jnp.concatenate([vhist[...], v_ref[0].astype(F32)], axis=0)
    vb_all = v_all.astype(BF16)

    for j in range(CS):
        rows = slice(j * L, j * L + lk)
        qst = jnp.concatenate([qn[g][j * L:(j + 1) * L, :] for g in range(A_GROUP)], axis=0)
        km = jnp.concatenate(lane_masked(k_all[rows]), axis=0)
        s_all = _dot_nt(km, qst)
        v_t = _dot_nt(eye_ref[...], vb_all[rows]).astype(BF16)
        var = jnp.minimum((c * CS + j) * L, WINDOW - cache_len) // L
        o_t = []
        if side_work is not None:
            side_work(j, CS)
        for kv in range(A_KV_HEADS):
            s = s_all[kv * lk:(kv + 1) * lk, :] + bias_ref[var, kv]
            sk = sink_ref[kv]
            mx = jnp.maximum(jnp.max(s, axis=0, keepdims=True), sk)
            p = jnp.exp(s - mx)
            den = jnp.sum(p, axis=0, keepdims=True) + jnp.exp(sk - mx)
            o = _dot(v_t[kv * A_HEAD_DIM:(kv + 1) * A_HEAD_DIM, :], p.astype(BF16))
            o_t.append(o * (1.0 / den))
        o_all = jnp.concatenate(o_t, axis=0).T
        ha = jnp.concatenate([o_all[g * L:(g + 1) * L, :] for g in range(A_GROUP)], axis=1)
        write_ha(j, _rms(ha, gao_ref[...]).astype(BF16))

    k_win = k_all[T:T + WINDOW]
    v_win = v_all[T:T + WINDOW]
    khist[...] = k_win
    vhist[...] = v_win

    @pl.when(is_last)
    def _():
        cko_ref[0] = k_win
        cvo_ref[0] = v_win


def _swa_kernel(*refs, L, CS, cache_len):
    ins, (ha_ref, cko_ref, cvo_ref, khist, vhist) = refs[:12], refs[12:]

    def write_ha(j, rows):
        ha_ref[0, j * L:(j + 1) * L, :] = rows

    _swa_body(pl.program_id(1), pl.program_id(1) == pl.num_programs(1) - 1, *ins, cko_ref, cvo_ref, khist, vhist,
              write_ha, L=L, CS=CS, cache_len=cache_len)


def _swa_out_kernel(*refs, L, CS, nc, n):
    ins, (x_ref, hm_ref, wm_ref, wa_ref), (o_ref, cko_ref, cvo_ref, khist, vhist, ha_prev) = (
        refs[:12], refs[12:16], refs[16:])
    s = pl.program_id(0)
    c = jnp.minimum(s, n - 1) % nc

    @pl.when(s == 0)
    def _():
        ha_prev[...] = jnp.zeros(ha_prev.shape, ha_prev.dtype)

    hm = hm_ref[...]
    ha_old = ha_prev[...]

    def project(j, parts):
        w = o_ref.shape[1] // parts
        cols = slice(j * w, (j + 1) * w)
        o_ref[:, cols] = x_ref[:, cols] + _dot(hm, wm_ref[:, cols]) + _dot(ha_old, wa_ref[:, cols])

    def write_ha(j, rows):
        ha_prev[j * L:(j + 1) * L, :] = rows

    _swa_body(c, jnp.logical_and(s < n, c == nc - 1), *ins, cko_ref, cvo_ref, khist, vhist,
              write_ha, L=L, CS=CS, cache_len=0, side_work=project)


def _swa_out(proj, col0, x, hm, w_out, g_q, g_k, g_ao, bias, sink, consts, layer, L, CS):
    b, s, _ = proj.shape
    m, d = x.shape
    T = CS * L
    nc = s // T
    n = b * nc
    lk = WINDOW + L
    gl = A_GROUP * L
    nvar = bias.shape[0]
    q_blk, kv_blk = col0 // A_WIDTH, (col0 + A_WIDTH) // A_KV_WIDTH
    e4, eye = consts
    zk = jnp.zeros((b, WINDOW, A_KV_WIDTH), F32)
    cur = lambda s_: jnp.minimum(s_, n - 1)
    prev = lambda s_: jnp.maximum(s_ - 1, 0)
    full = lambda *shape: pl.BlockSpec(shape, lambda s_: (0,) * len(shape))
    out_shape = (jax.ShapeDtypeStruct((m, d), F32),
                 jax.ShapeDtypeStruct((b, WINDOW, A_KV_WIDTH), F32),
                 jax.ShapeDtypeStruct((b, WINDOW, A_KV_WIDTH), F32))
    return pl.pallas_call(
        functools.partial(_swa_out_kernel, L=L, CS=CS, nc=nc, n=n),
        out_shape=out_shape,
        grid=(n + 1,),
        in_specs=[pl.BlockSpec((1, T, A_WIDTH), lambda s_: (cur(s_) // nc, cur(s_) % nc, q_blk)),
                  pl.BlockSpec((1, T, A_KV_WIDTH), lambda s_: (cur(s_) // nc, cur(s_) % nc, kv_blk)),
                  pl.BlockSpec((1, T, A_KV_WIDTH), lambda s_: (cur(s_) // nc, cur(s_) % nc, kv_blk + 1)),
                  pl.BlockSpec((1, WINDOW, A_KV_WIDTH), lambda s_: (cur(s_) // nc, 0, 0)),
                  pl.BlockSpec((1, WINDOW, A_KV_WIDTH), lambda s_: (cur(s_) // nc, 0, 0)),
                  pl.BlockSpec((None, 1, A_WIDTH), lambda s_: (layer, 0, 0)),
                  pl.BlockSpec((None, 1, A_KV_WIDTH), lambda s_: (layer, 0, 0)),
                  pl.BlockSpec((None, 1, A_WIDTH), lambda s_: (layer, 0, 0)),
                  full(nvar, A_KV_HEADS, lk, gl),
                  pl.BlockSpec((None, A_KV_HEADS, 1, gl), lambda s_: (layer, 0, 0, 0)),
                  full(A_KV_WIDTH, A_KV_WIDTH),
                  full(A_KV_WIDTH, A_KV_WIDTH),
                  pl.BlockSpec((T, d), lambda s_: (prev(s_), 0)),
                  pl.BlockSpec((T, M_WIDTH), lambda s_: (prev(s_), 0)),
                  pl.BlockSpec((None, M_WIDTH, d), lambda s_: (layer, 0, 0), pipeline_mode=pl.Buffered(1)),
                  pl.BlockSpec((None, A_WIDTH, d), lambda s_: (layer, 1, 0), pipeline_mode=pl.Buffered(1))],
        out_specs=(pl.BlockSpec((T, d), lambda s_: (prev(s_), 0)),
                   pl.BlockSpec((1, WINDOW, A_KV_WIDTH), lambda s_: (cur(s_) // nc, 0, 0)),
                   pl.BlockSpec((1, WINDOW, A_KV_WIDTH), lambda s_: (cur(s_) // nc, 0, 0))),
        scratch_shapes=[pltpu.VMEM((WINDOW, A_KV_WIDTH), F32),
                        pltpu.VMEM((WINDOW, A_KV_WIDTH), F32),
                        pltpu.VMEM((T, A_WIDTH), BF16)],
        compiler_params=_params(("arbitrary",)),
        name="swa_out",
    )(proj, proj, proj, zk, zk, g_q, g_k, g_ao, bias, sink, e4, eye, x, hm, w_out, w_out)


def _swa(proj, col0, ck, cv, g_q, g_k, g_ao, bias, sink, consts, layer, L, CS, cache_len):
    b, s, _ = proj.shape
    T = CS * L
    lk = WINDOW + L
    gl = A_GROUP * L
    nvar = bias.shape[0]
    q_blk, kv_blk = col0 // A_WIDTH, (col0 + A_WIDTH) // A_KV_WIDTH
    e4, eye = consts
    out_shape = (jax.ShapeDtypeStruct((b, s, A_WIDTH), BF16),
                 jax.ShapeDtypeStruct((b, WINDOW, A_KV_WIDTH), F32),
                 jax.ShapeDtypeStruct((b, WINDOW, A_KV_WIDTH), F32))
    full = lambda *shape: pl.BlockSpec(shape, lambda i, c: (0,) * len(shape))
    return pl.pallas_call(
        functools.partial(_swa_kernel, L=L, CS=CS, cache_len=cache_len),
        out_shape=out_shape,
        grid=(b, s // T),
        in_specs=[pl.BlockSpec((1, T, A_WIDTH), lambda i, c: (i, c, q_blk)),
                  pl.BlockSpec((1, T, A_KV_WIDTH), lambda i, c: (i, c, kv_blk)),
                  pl.BlockSpec((1, T, A_KV_WIDTH), lambda i, c: (i, c, kv_blk + 1)),
                  pl.BlockSpec((1, WINDOW, A_KV_WIDTH), lambda i, c: (i, 0, 0)),
                  pl.BlockSpec((1, WINDOW, A_KV_WIDTH), lambda i, c: (i, 0, 0)),
                  pl.BlockSpec((None, 1, A_WIDTH), lambda i, c: (layer, 0, 0)),
                  pl.BlockSpec((None, 1, A_KV_WIDTH), lambda i, c: (layer, 0, 0)),
                  pl.BlockSpec((None, 1, A_WIDTH), lambda i, c: (layer, 0, 0)),
                  full(nvar, A_KV_HEADS, lk, gl),
                  pl.BlockSpec((None, A_KV_HEADS, 1, gl), lambda i, c: (layer, 0, 0, 0)),
                  full(A_KV_WIDTH, A_KV_WIDTH),
                  full(A_KV_WIDTH, A_KV_WIDTH)],
        out_specs=(pl.BlockSpec((1, T, A_WIDTH), lambda i, c: (i, c, 0)),
                   pl.BlockSpec((1, WINDOW, A_KV_WIDTH), lambda i, c: (i, 0, 0)),
                   pl.BlockSpec((1, WINDOW, A_KV_WIDTH), lambda i, c: (i, 0, 0))),
        scratch_shapes=[pltpu.VMEM((WINDOW, A_KV_WIDTH), F32),
                        pltpu.VMEM((WINDOW, A_KV_WIDTH), F32)],
        compiler_params=_params(("arbitrary", "arbitrary")),
        name="swa",
    )(proj, proj, proj, ck, cv, g_q, g_k, g_ao, bias, sink, e4, eye)


def _out_proj_kernel(x_ref, hm_ref, ha_ref, wm_ref, wa_ref, o_ref):
    o_ref[...] = x_ref[...] + _dot(hm_ref[...], wm_ref[...]) + _dot(ha_ref[...], wa_ref[...])


def _out_proj(x, hm, ha, w_out, layer):
    m, d = x.shape
    tm = _row_tile(m, 512)
    return pl.pallas_call(
        _out_proj_kernel,
        out_shape=jax.ShapeDtypeStruct((m, d), F32),
        grid=(m // tm,),
        in_specs=[pl.BlockSpec((tm, d), lambda i: (i, 0)),
                  pl.BlockSpec((tm, M_WIDTH), lambda i: (i, 0)),
                  pl.BlockSpec((tm, A_WIDTH), lambda i: (i, 0)),
                  pl.BlockSpec((None, M_WIDTH, d), lambda i: (layer, 0, 0)),
                  pl.BlockSpec((None, A_WIDTH, d), lambda i: (layer, 1, 0))],
        out_specs=pl.BlockSpec((tm, d), lambda i: (i, 0)),
        compiler_params=_params(("parallel",)),
        name="out_proj",
    )(x, hm, ha, w_out, w_out)


def _ffn_kernel(x_ref, g_ref, wu_ref, wd_ref, o_ref, *rest):
    h_scr = rest[-1]
    f = pl.program_id(1)

    @pl.when(f == 0)
    def _():
        x = x_ref[...]
        h_scr[...] = _rms(x, g_ref[...]).astype(BF16)
        o_ref[...] = x

    wu, wd = wu_ref[...].astype(BF16), wd_ref[...].astype(BF16)
    if len(rest) == 3:
        rest[0][...] = wu
        rest[1][...] = wd
    u = _dot(h_scr[...], wu)
    a = jnp.square(jnp.maximum(u, 0.0)).astype(BF16)
    o_ref[...] += _dot(a, wd)


def _ffn(x, g_ffn, w_up, w_down, layer, emit_bf16=False):
    m, d = x.shape
    d_ff = w_up.shape[-1]
    tm = _row_tile(m, 512)
    tf = 512 if emit_bf16 else 1024
    assert not emit_bf16 or m == tm
    if emit_bf16:
        w_specs = [pl.BlockSpec((None, d, tf), lambda i, f: (layer, 0, f)),
                   pl.BlockSpec((None, tf, d), lambda i, f: (layer, f, 0))]
    else:
        w_specs = [pl.BlockSpec((d, tf), lambda i, f: (0, f)), pl.BlockSpec((tf, d), lambda i, f: (f, 0))]
    out_shape = [jax.ShapeDtypeStruct((m, d), F32)]
    out_specs = [pl.BlockSpec((tm, d), lambda i, f: (i, 0))]
    if emit_bf16:
        out_shape += [jax.ShapeDtypeStruct((d, d_ff), BF16), jax.ShapeDtypeStruct((d_ff, d), BF16)]
        out_specs += [pl.BlockSpec((d, tf), lambda i, f: (0, f)), pl.BlockSpec((tf, d), lambda i, f: (f, 0))]
    return pl.pallas_call(
        _ffn_kernel,
        out_shape=out_shape,
        grid=(m // tm, d_ff // tf),
        in_specs=[pl.BlockSpec((tm, d), lambda i, f: (i, 0)),
                  pl.BlockSpec((None, 1, d), lambda i, f: (layer, 0, 0))] + w_specs,
        out_specs=out_specs,
        scratch_shapes=[pltpu.VMEM((tm, d), BF16)],
        compiler_params=_params(("parallel", "arbitrary")),
        name="ffn_cast" if emit_bf16 else "ffn",
    )(x, g_ffn, w_up, w_down)


def _layer(x, wts, layer, lm, la, cs, bias, sink, ck, cv, c0, n0, m0, cache_len, ffn_w):
    b, s, d = x.shape
    x2 = x.reshape(b * s, d)
    eye = wts["consts"][1]
    if ck is None:
        hm, aproj, c_new, n_new, m_new = _proj_mlstm(x, wts["g_mix"], wts["w_all"], wts["gbias"], wts["g_mo"], eye,
                                                     c0, n0, m0, layer, lm)
        x2, k_new, v_new = _swa_out(aproj, 0, x2, hm.reshape(b * s, M_WIDTH), wts["w_out"], wts["g_q"], wts["g_k"],
                                    wts["g_ao"], bias, sink, wts["consts"], layer, la, cs)
    else:
        proj, gates = _in_proj(x2, wts["g_mix"], wts["w_all"], layer)
        proj = proj.reshape(b, s, PROJ_W)
        hm, c_new, n_new, m_new = _mlstm(proj, gates.reshape(b, s, GATE_W), wts["gbias"], wts["g_mo"], eye,
                                         c0, n0, m0, layer, lm)
        ha, k_new, v_new = _swa(proj, M_PROJ_W, ck, cv, wts["g_q"], wts["g_k"], wts["g_ao"], bias, sink, wts["consts"],
                                layer, la, cs, cache_len)
        x2 = _out_proj(x2, hm.reshape(b * s, M_WIDTH), ha.reshape(b * s, A_WIDTH), wts["w_out"], layer)
    if ffn_w is None:
        x2, *ffn_w = _ffn(x2, wts["g_ffn"], wts["w_up"], wts["w_down"], layer, emit_bf16=True)
    else:
        x2, = _ffn(x2, wts["g_ffn"], *ffn_w, layer)
    return x2.reshape(b, s, d), k_new, v_new, c_new, n_new, m_new, ffn_w


def _w_in_columns():
    o = np.cumsum((0, M_HEADS * M_DK, M_HEADS * M_DK, M_WIDTH, M_WIDTH, M_HEADS, M_HEADS, A_WIDTH, A_KV_WIDTH, A_KV_WIDTH))
    mq, mk, mv, mo, mi, mf, aq, ak, av = (int(v) for v in o[:9])
    runs = []
    for h in range(M_HEADS):
        runs += [(mq + h * M_DK, M_DK), (mk + h * M_DK, M_DK), (mv + h * M_DV, M_DV), (mo + h * M_DV, M_DV)]
    for g in range(A_GROUP):
        runs += [(aq + (kv * A_GROUP + g) * A_HEAD_DIM, A_HEAD_DIM) for kv in range(A_KV_HEADS)]
    return runs + [(ak, A_KV_WIDTH), (av, A_KV_WIDTH), (mi, M_HEADS), (mf, M_HEADS)]


def _w_in_layout_kernel(w_ref, o_ref):
    dst = 0
    for src, width in _w_in_columns():
        o_ref[:, dst:dst + width] = w_ref[:, src:src + width].astype(o_ref.dtype)
        dst += width
    o_ref[:, dst:] = jnp.zeros((o_ref.shape[0], o_ref.shape[1] - dst), o_ref.dtype)


def _w_in_layout(w_in):
    depth, d, d_in = w_in.shape
    tk = _row_tile(d, 256)
    return pl.pallas_call(
        _w_in_layout_kernel,
        out_shape=jax.ShapeDtypeStruct((depth, d, PROJ_W + GATE_W), BF16),
        grid=(depth, d // tk),
        in_specs=[pl.BlockSpec((None, tk, d_in), lambda l, i: (l, i, 0))],
        out_specs=pl.BlockSpec((None, tk, PROJ_W + GATE_W), lambda l, i: (l, i, 0)),
        compiler_params=_params(("parallel", "parallel")),
        name="w_in_layout",
    )(w_in)


def _head_consts():
    hd = np.arange(A_KV_WIDTH) // A_HEAD_DIM
    e4 = (hd[:, None] == hd[None, :]).astype(np.float32)
    eye = np.eye(A_KV_WIDTH, dtype=np.float32)
    return tuple(jnp.asarray(a, BF16) for a in (e4, eye))


def kernel(x_prompt, x_sample, cache_k, cache_v, state_C, state_n, state_m, rel_bias, g_mix, w_in, b_i, b_f, g_q, g_k, sinks, g_mo, g_ao, w_out, g_ffn, w_up, w_down):
    depth = w_in.shape[0]
    bp, sp, d = x_prompt.shape
    bs, ls, _ = x_sample.shape
    n_win = cache_k.shape[2]
    assert sp % CHUNK == 0 and n_win == WINDOW and ls % 16 == 0 and ls <= CHUNK
    lm = M_CHUNK if sp % M_CHUNK == 0 else CHUNK

    w_all = _w_in_layout(w_in)
    gbias = jnp.concatenate([b_i, b_f, jnp.zeros((depth, GATE_W - 2 * M_HEADS), F32)], axis=-1)

    def regroup(a):
        rest = a.shape[2:]
        a = a.reshape(depth, A_KV_HEADS, A_GROUP, A_HEAD_DIM, *rest)
        return jnp.swapaxes(a, 1, 2).reshape(depth, A_WIDTH, *rest)

    w_out_b = jnp.concatenate([w_out[:, :M_WIDTH], regroup(w_out[:, M_WIDTH:])], axis=1).astype(BF16)
    wts = {
        "g_mix": g_mix.reshape(depth, 1, d), "w_all": w_all,
        "gbias": gbias.reshape(depth, 1, GATE_W), "g_mo": g_mo.reshape(depth, 1, M_WIDTH),
        "g_q": jnp.tile(g_q * A_SCALE, (1, A_HEADS)).reshape(depth, 1, A_WIDTH),
        "g_k": jnp.tile(g_k, (1, A_KV_HEADS)).reshape(depth, 1, A_KV_WIDTH),
        "g_ao": regroup(g_ao).reshape(depth, 1, A_WIDTH),
        "w_out": w_out_b, "g_ffn": g_ffn.reshape(depth, 1, d),
        "w_up": w_up, "w_down": w_down,
        "consts": _head_consts(),
    }

    bias = _bias_table(rel_bias)

    def bias_t(lq, cache_len):
        lk = WINDOW + lq
        t = bias[:, :lq, :lk].reshape(A_KV_HEADS, A_GROUP, lq, lk)
        t = t.transpose(0, 3, 1, 2).reshape(A_KV_HEADS, lk, A_GROUP * lq)
        n_prev = np.arange((WINDOW - cache_len) // lq + 1) * lq + cache_len
        dead = np.arange(lk)[None, :] < WINDOW - n_prev[:, None]
        return t[None] + jnp.asarray(np.where(dead, NEG_INF, 0.0)[:, None, :, None], F32)

    def sink_rows(lq):
        return jnp.repeat(sinks.reshape(depth, A_KV_HEADS, 1, A_GROUP), lq, axis=-1)

    zc =jnp.zeros((bp, M_HEADS, M_DV, M_DK), F32)
    zn = jnp.zeros((bp, M_HEADS, M_DK), F32)
    zm = jnp.zeros((bp, 1, M_HEADS), F32)

    xp, xs = x_prompt, x_sample
    bias_p, bias_s, sink_p, sink_s = bias_t(CHUNK, 0), bias_t(ls, WINDOW), sink_rows(CHUNK), sink_rows(ls)
    cs = A_CHUNKS_PER_STEP if sp % (A_CHUNKS_PER_STEP * CHUNK) == 0 else 1
    outs_p, outs_s = [], []
    for l in range(depth):
        xs, *st, ffn_w = _layer(xs, wts, l, ls, ls, 1, bias_s, sink_s,
                                cache_k[l].reshape(bs, n_win, A_KV_WIDTH), cache_v[l].reshape(bs, n_win, A_KV_WIDTH),
                                state_C[l], state_n[l], state_m[l].reshape(bs, 1, M_HEADS), WINDOW, None)
        outs_s.append(st)
        xp, *st, _ = _layer(xp, wts, l, lm, CHUNK, cs, bias_p, sink_p, None, None, zc, zn, zm, 0, ffn_w)
        outs_p.append(st)

    def stack(outs, b):
        k, v, c, n, m = (jnp.stack([o_[i] for o_ in outs]) for i in range(5))
        return (k.reshape(depth, b, WINDOW, A_KV_HEADS, A_HEAD_DIM), v.reshape(depth, b, WINDOW, A_KV_HEADS, A_HEAD_DIM),
                c, n, m.reshape(depth, b, M_HEADS))

    return (xp, xs) + stack(outs_p, bp) + stack(outs_s, bs)
```

```python
import functools

import jax
import jax.numpy as jnp
import numpy as np
from jax import lax
from jax.experimental import pallas as pl
from jax.experimental.pallas import tpu as pltpu

F32 = jnp.float32
BF16 = jnp.bfloat16

CHUNK = 64
M_CHUNK = 512
A_CHUNKS_PER_STEP = 4
M_HEADS = 4
M_DK = 128
M_DV = 256
M_WIDTH = M_HEADS * M_DV
A_HEADS = 16
A_KV_HEADS = 4
A_HEAD_DIM = 64
A_GROUP = A_HEADS // A_KV_HEADS
A_WIDTH = A_HEADS * A_HEAD_DIM
A_KV_WIDTH = A_KV_HEADS * A_HEAD_DIM
WINDOW = 128
N_BUCKETS = 32
MAX_DISTANCE = 128
EPS = 1e-6
NEG_INF = -1e30
M_SCALE = M_DK ** -0.5
A_SCALE = A_HEAD_DIM ** -0.5

HEAD_W = 2 * M_DK + 2 * M_DV
M_PROJ_W = M_HEADS * HEAD_W
A_PROJ_W = A_WIDTH + 2 * A_KV_WIDTH
PROJ_W = M_PROJ_W + A_PROJ_W
GATE_W = 128

V7X_VMEM_LIMIT = 56 * 1024 * 1024

NT_DIMS = (((1,), (1,)), ((), ()))


def _params(sem, vmem=V7X_VMEM_LIMIT):
    return pltpu.CompilerParams(dimension_semantics=sem, vmem_limit_bytes=vmem)


def _row_tile(m, cap):
    t = min(m, cap)
    while m % t:
        t //= 2
    return t


def _dot(a, b):
    return jnp.dot(a, b, preferred_element_type=F32)


def _dot_nt(a, b):
    return lax.dot_general(a, b, NT_DIMS, preferred_element_type=F32)


def _t5_bucket(rel):
    half = N_BUCKETS // 2
    exact = half // 2
    n = np.abs(rel)
    large = exact + (np.log(np.maximum(n, 1) / exact) / np.log(MAX_DISTANCE / exact) * (half - exact)).astype(np.int32)
    large = np.minimum(large, half - 1)
    return (rel > 0).astype(np.int32) * half + np.where(n < exact, n, large).astype(np.int32)


def _bias_kernel(rel_ref, map_ref, out_ref):
    bmap = map_ref[...]
    for h in range(A_HEADS):
        acc = jnp.zeros(bmap.shape, F32)
        for b in range(N_BUCKETS):
            acc = jnp.where(bmap == b, rel_ref[b, h], acc)
        out_ref[h] = acc


def _bias_table(rel_bias):
    lk = WINDOW + CHUNK
    rel = (np.arange(lk)[None, :] - WINDOW) - np.arange(CHUNK)[:, None]
    bmap = jnp.asarray(_t5_bucket(rel), jnp.int32)
    return pl.pallas_call(
        _bias_kernel,
        out_shape=jax.ShapeDtypeStruct((A_HEADS, CHUNK, lk), F32),
        in_specs=[pl.BlockSpec(memory_space=pltpu.SMEM),
                  pl.BlockSpec(memory_space=pltpu.VMEM)],
        out_specs=pl.BlockSpec(memory_space=pltpu.VMEM),
        name="t5_bias_table",
    )(rel_bias, bmap)


def _rms(x, g):
    return x * lax.rsqrt(jnp.mean(x * x, axis=-1, keepdims=True) + EPS) * g


def _in_proj_kernel(x_ref, g_ref, w_ref, o_ref, gate_ref, *, sub):
    for r in range(0, x_ref.shape[0], sub):
        h = _rms(x_ref[r:r + sub, :], g_ref[...]).astype(BF16)
        acc = _dot(h, w_ref[...])
        o_ref[r:r + sub, :] = acc[:, :PROJ_W].astype(o_ref.dtype)
        gate_ref[r:r + sub, :] = acc[:, PROJ_W:]


def _in_proj(x, g_mix, w_all, layer):
    m, d = x.shape
    tm = _row_tile(m, 512)
    n = PROJ_W + GATE_W
    return pl.pallas_call(
        functools.partial(_in_proj_kernel, sub=min(tm, 256)),
        out_shape=(jax.ShapeDtypeStruct((m, PROJ_W), BF16), jax.ShapeDtypeStruct((m, GATE_W), F32)),
        grid=(m // tm,),
        in_specs=[pl.BlockSpec((tm, d), lambda i: (i, 0)),
                  pl.BlockSpec((None, 1, d), lambda i: (layer, 0, 0)),
                  pl.BlockSpec((None, d, n), lambda i: (layer, 0, 0), pipeline_mode=pl.Buffered(1))],
        out_specs=(pl.BlockSpec((tm, PROJ_W), lambda i: (i, 0)),
                   pl.BlockSpec((tm, GATE_W), lambda i: (i, 0))),
        compiler_params=_params(("parallel",)),
        name="in_proj",
    )(x, g_mix, w_all)


def _mlstm_body(head_inputs, gates, gbias_ref, gmo_ref, eye_ref, c0_ref, n0_ref, m0_ref,
                hm_ref, cout_ref, nout_ref, mout_ref, c_scr, n_scr, m_scr, *, L):
    c = pl.program_id(1)
    last = pl.num_programs(1) - 1

    @pl.when(c == 0)
    def _():
        c_scr[...] = c0_ref[0]
        n_scr[...] = n0_ref[0]
        for h in range(M_HEADS):
            m_scr[h:h + 1, :] = jnp.broadcast_to(m0_ref[0, :, h:h + 1], (1, 128))

    a_all = gates + gbias_ref[...]
    logf = jnp.minimum(a_all, 0.0) - jnp.log(1.0 + jnp.exp(-jnp.abs(a_all)))
    row = lax.broadcasted_iota(jnp.int32, (L, GATE_W), 0)
    b_all = logf
    k = 1
    while k < L:
        b_all = b_all + jnp.where(row >= k, pltpu.roll(b_all, k, axis=0), 0.0)
        k *= 2
    b_sh = pltpu.roll(b_all, GATE_W - M_HEADS, axis=1)
    a_i = a_all - b_sh
    cmax = a_i
    k = 1
    while k < L:
        cmax = jnp.maximum(cmax, jnp.where(row >= k, pltpu.roll(cmax, k, axis=0), NEG_INF))
        k *= 2
    a_t = a_i.T
    tri = lax.broadcasted_iota(jnp.int32, (L, L), 0) >= lax.broadcasted_iota(jnp.int32, (L, L), 1)
    eye = eye_ref[...]

    for h in range(M_HEADS):
        b_col = b_sh[:, h:h + 1]
        a_col = a_i[:, h:h + 1]
        a_row = a_t[h:h + 1, :]
        m_prev = m_scr[h:h + 1, 0:1]
        m_run = jnp.maximum(cmax[:, h:h + 1], m_prev)
        decay_w = jnp.exp(jnp.where(tri, a_row - m_run, NEG_INF))
        qh, kh, vh, og = head_inputs(h)
        w = decay_w * (_dot_nt(qh, kh) * M_SCALE)
        g = jnp.exp(m_prev - m_run)
        cst = c_scr[h]
        n_row = n_scr[h:h + 1, :]
        num = _dot(w.astype(BF16), vh) + g * _dot_nt(qh, cst.astype(BF16))
        den = (jnp.sum(w, axis=1, keepdims=True)
               + g * jnp.sum(qh.astype(F32) * n_row, axis=1, keepdims=True))
        inv = 1.0 / jnp.maximum(jnp.abs(den), jnp.exp(-(b_col + m_run)))
        ms = jnp.mean(num * num, axis=1, keepdims=True)
        scale = inv * lax.rsqrt(inv * inv * ms + EPS)
        hm_ref[0, :, h * M_DV:(h + 1) * M_DV] = (
            num * scale * gmo_ref[:, h * M_DV:(h + 1) * M_DV] * jax.nn.sigmoid(og)).astype(hm_ref.dtype)

        b_last = b_col[L - 1:L, :]
        m_end = b_last + m_run[L - 1:L, :]
        wk = jnp.exp(b_last + a_col - m_end)
        decay = jnp.exp(b_last + m_prev - m_end)
        vw = (vh.astype(F32) * wk).astype(BF16)
        vw_t = _dot_nt(eye, vw).astype(BF16)
        c_new = decay * cst + M_SCALE * _dot(vw_t, kh)
        n_new = decay * n_row + M_SCALE * jnp.sum(kh.astype(F32) * wk, axis=0, keepdims=True)
        c_scr[h] = c_new
        n_scr[h:h + 1, :] = n_new
        m_scr[h:h + 1, :] = jnp.broadcast_to(m_end, (1, 128))

        @pl.when(c == last)
        def _():
            cout_ref[0, h] = c_new
            nout_ref[0, h:h + 1, :] = n_new
            mout_ref[0, :, h:h + 1] = m_end


def _mlstm_kernel(p_ref, gate_ref, *rest, L):
    def head_inputs(h):
        o = h * HEAD_W
        return (p_ref[0, :, o:o + M_DK], p_ref[0, :, o + M_DK:o + 2 * M_DK],
                p_ref[0, :, o + 2 * M_DK:o + 2 * M_DK + M_DV], p_ref[0, :, o + 2 * M_DK + M_DV:o + HEAD_W].astype(F32))

    _mlstm_body(head_inputs, gate_ref[0], *rest, L=L)


def _proj_mlstm_kernel(x_ref, g_ref, w_ref, gbias_ref, gmo_ref, eye_ref, c0_ref, n0_ref, m0_ref,
                       hm_ref, ap_ref, cout_ref, nout_ref, mout_ref, c_scr, n_scr, m_scr, *, L):
    hn = _rms(x_ref[0], g_ref[...]).astype(BF16)

    def head_inputs(h):
        o = h * HEAD_W
        ph = _dot(hn, w_ref[:, o:o + HEAD_W])
        return (ph[:, :M_DK].astype(BF16), ph[:, M_DK:2 * M_DK].astype(BF16),
                ph[:, 2 * M_DK:2 * M_DK + M_DV].astype(BF16), ph[:, 2 * M_DK + M_DV:])

    gates = _dot(hn, w_ref[:, PROJ_W:])
    ap_ref[0] = _dot(hn, w_ref[:, M_PROJ_W:PROJ_W]).astype(ap_ref.dtype)
    _mlstm_body(head_inputs, gates, gbias_ref, gmo_ref, eye_ref, c0_ref, n0_ref, m0_ref,
                hm_ref, cout_ref, nout_ref, mout_ref, c_scr, n_scr, m_scr, L=L)


def _mlstm_specs(b, layer, L):
    state = [pl.BlockSpec((1, M_HEADS, M_DV, M_DK), lambda i, c: (i, 0, 0, 0)),
             pl.BlockSpec((1, M_HEADS, M_DK), lambda i, c: (i, 0, 0)),
             pl.BlockSpec((1, 1, M_HEADS), lambda i, c: (i, 0, 0))]
    params = [pl.BlockSpec((None, 1, GATE_W), lambda i, c: (layer, 0, 0)),
              pl.BlockSpec((None, 1, M_WIDTH), lambda i, c: (layer, 0, 0)),
              pl.BlockSpec((M_DV, M_DV), lambda i, c: (0, 0))]
    state_shapes = [jax.ShapeDtypeStruct((b, M_HEADS, M_DV, M_DK), F32),
                    jax.ShapeDtypeStruct((b, M_HEADS, M_DK), F32),
                    jax.ShapeDtypeStruct((b, 1, M_HEADS), F32)]
    scratch = [pltpu.VMEM((M_HEADS, M_DV, M_DK), F32), pltpu.VMEM((M_HEADS, M_DK), F32), pltpu.VMEM((M_HEADS, 128), F32)]
    return params + state, state, state_shapes, scratch


def _mlstm(proj, gates, gbias, g_mo, eye, c0, n0, m0, layer, L):
    b, s, _ = proj.shape
    ins, state_out, state_shapes, scratch = _mlstm_specs(b, layer, L)
    return pl.pallas_call(
        functools.partial(_mlstm_kernel, L=L),
        out_shape=[jax.ShapeDtypeStruct((b, s, M_WIDTH), BF16)] + state_shapes,
        grid=(b, s // L),
        in_specs=[pl.BlockSpec((1, L, M_PROJ_W), lambda i, c: (i, c, 0)),
                  pl.BlockSpec((1, L, GATE_W), lambda i, c: (i, c, 0))] + ins,
        out_specs=[pl.BlockSpec((1, L, M_WIDTH), lambda i, c: (i, c, 0))] + state_out,
        scratch_shapes=scratch,
        compiler_params=_params(("arbitrary", "arbitrary")),
        name="mlstm",
    )(proj, gates, gbias, g_mo, eye, c0, n0, m0)


def _proj_mlstm(x, g_mix, w_all, gbias, g_mo, eye, c0, n0, m0, layer, L):
    b, s, d = x.shape
    ins, state_out, state_shapes, scratch = _mlstm_specs(b, layer, L)
    return pl.pallas_call(
        functools.partial(_proj_mlstm_kernel, L=L),
        out_shape=[jax.ShapeDtypeStruct((b, s, M_WIDTH), BF16), jax.ShapeDtypeStruct((b, s, A_PROJ_W), BF16)] + state_shapes,
        grid=(b, s // L),
        in_specs=[pl.BlockSpec((1, L, d), lambda i, c: (i, c, 0)),
                  pl.BlockSpec((None, 1, d), lambda i, c: (layer, 0, 0)),
                  pl.BlockSpec((None, d, PROJ_W + GATE_W), lambda i, c: (layer, 0, 0), pipeline_mode=pl.Buffered(1))] + ins,
        out_specs=[pl.BlockSpec((1, L, M_WIDTH), lambda i, c: (i, c, 0)),
                   pl.BlockSpec((1, L, A_PROJ_W), lambda i, c: (i, c, 0))] + state_out,
        scratch_shapes=scratch,
        compiler_params=_params(("arbitrary", "arbitrary")),
        name="proj_mlstm",
    )(x, g_mix, w_all, gbias, g_mo, eye, c0, n0, m0)


def _swa_body(c, is_last, q_ref, k_ref, v_ref, ck_ref, cv_ref, gq_ref, gk_ref, gao_ref, bias_ref, sink_ref,
              e4_ref, eye_ref, cko_ref, cvo_ref, khist, vhist, write_ha, *, L, CS, cache_len, side_work=None):
    T = CS * L
    lk = WINDOW + L
    gl = A_GROUP * L
    kv_lane = lax.broadcasted_iota(jnp.int32, (1, A_KV_WIDTH), 1) // A_HEAD_DIM

    def lane_masked(kn):
        knb = kn.astype(BF16)
        return [jnp.where(kv_lane == kv, knb, 0.0) for kv in range(A_KV_HEADS)]

    @pl.when(c == 0)
    def _():
        khist[...] = ck_ref[0]
        vhist[...] = cv_ref[0]

    q = q_ref[0].astype(F32)
    k = k_ref[0].astype(F32)
    sq = jnp.concatenate([q[:, g * A_KV_WIDTH:(g + 1) * A_KV_WIDTH] for g in range(A_GROUP)] + [k], axis=0)
    sq = sq * sq
    sq_hi = sq.astype(BF16)
    sq_lo = (sq - sq_hi.astype(F32)).astype(BF16)
    ss = _dot(jnp.concatenate([sq_hi, sq_lo], axis=0), e4_ref[...])
    rinv = lax.rsqrt((ss[:5 * T] + ss[5 * T:]) * (1.0 / A_HEAD_DIM) + EPS)
    qn = [(q[:, g * A_KV_WIDTH:(g + 1) * A_KV_WIDTH] * rinv[g * T:(g + 1) * T, :]
           * gq_ref[:, g * A_KV_WIDTH:(g + 1) * A_KV_WIDTH]).astype(BF16) for g in range(A_GROUP)]
    kn = k * rinv[A_GROUP * T:, :] * gk_ref[...]

    k_all = jnp.concatenate([khist[...], kn], axis=0)
    v_all = jnp.concatenate([vhist[...], v_ref[0].astype(F32)], axis=0)
    vb_all = v_all.astype(BF16)

    for j in range(CS):
        rows = slice(j * L, j * L + lk)
        qst = jnp.concatenate([qn[g][j * L:(j + 1) * L, :] for g in range(A_GROUP)], axis=0)
        km = jnp.concatenate(lane_masked(k_all[rows]), axis=0)
        s_all = _dot_nt(km, qst)
        v_t = _dot_nt(eye_ref[...], vb_all[rows]).astype(BF16)
        var = jnp.minimum((c * CS + j) * L, WINDOW - cache_len) // L
        o_t = []
        if side_work is not None:
            side_work(j, CS)
        for kv in range(A_KV_HEADS):
            s = s_all[kv * lk:(kv + 1) * lk, :] + bias_ref[var, kv]
            sk = sink_ref[kv]
            mx = jnp.maximum(jnp.max(s, axis=0, keepdims=True), sk)
            p = jnp.exp(s - mx)
            den = jnp.sum(p, axis=0, keepdims=True) + jnp.exp(sk - mx)
            o = _dot(v_t[kv * A_HEAD_DIM:(kv + 1) * A_HEAD_DIM, :], p.astype(BF16))
            o_t.append(o * (1.0 / den))
        o_all = jnp.concatenate(o_t, axis=0).T
        ha = jnp.concatenate([o_all[g * L:(g + 1) * L, :] for g in range(A_GROUP)], axis=1)
        write_ha(j, _rms(ha, gao_ref[...]).astype(BF16))

    k_win = k_all[T:T + WINDOW]
    v_win = v_all[T:T + WINDOW]
    khist[...] = k_win
    vhist[...] = v_win

    @pl.when(is_last)
    def _():
        cko_ref[0] = k_win
        cvo_ref[0] = v_win


def _swa_kernel(*refs, L, CS, cache_len):
    ins, (ha_ref, cko_ref, cvo_ref, khist, vhist) = refs[:12], refs[12:]

    def write_ha(j, rows):
        ha_ref[0, j * L:(j + 1) * L, :] = rows

    _swa_body(pl.program_id(1), pl.program_id(1) == pl.num_programs(1) - 1, *ins, cko_ref, cvo_ref, khist, vhist,
              write_ha, L=L, CS=CS, cache_len=cache_len)


def _swa_out_kernel(*refs, L, CS, nc, n):
    ins, (x_ref, hm_ref, wm_ref, wa_ref), (o_ref, cko_ref, cvo_ref, khist, vhist, ha_prev) = (
        refs[:12], refs[12:16], refs[16:])
    s = pl.program_id(0)
    c = jnp.minimum(s, n - 1) % nc

    @pl.when(s == 0)
    def _():
        ha_prev[...] = jnp.zeros(ha_prev.shape, ha_prev.dtype)

    hm = hm_ref[...]
    ha_old = ha_prev[...]

    def project(j, parts):
        w = o_ref.shape[1] // parts
        cols = slice(j * w, (j + 1) * w)
        o_ref[:, cols] = x_ref[:, cols] + _dot(hm, wm_ref[:, cols]) + _dot(ha_old, wa_ref[:, cols])

    def write_ha(j, rows):
        ha_prev[j * L:(j + 1) * L, :] = rows

    _swa_body(c, jnp.logical_and(s < n, c == nc - 1), *ins, cko_ref, cvo_ref, khist, vhist,
              write_ha, L=L, CS=CS, cache_len=0, side_work=project)


def _swa_out(proj, col0, x, hm, w_out, g_q, g_k, g_ao, bias, sink, consts, layer, L, CS):
    b, s, _ = proj.shape
    m, d = x.shape
    T = CS * L
    nc = s // T
    n = b * nc
    lk = WINDOW + L
    gl = A_GROUP * L
    nvar = bias.shape[0]
    q_blk, kv_blk = col0 // A_WIDTH, (col0 + A_WIDTH) // A_KV_WIDTH
    e4, eye = consts
    zk = jnp.zeros((b, WINDOW, A_KV_WIDTH), F32)
    cur = lambda s_: jnp.minimum(s_, n - 1)
    prev = lambda s_: jnp.maximum(s_ - 1, 0)
    full = lambda *shape: pl.BlockSpec(shape, lambda s_: (0,) * len(shape))
    out_shape = (jax.ShapeDtypeStruct((m, d), F32),
                 jax.ShapeDtypeStruct((b, WINDOW, A_KV_WIDTH), F32),
                 jax.ShapeDtypeStruct((b, WINDOW, A_KV_WIDTH), F32))
    return pl.pallas_call(
        functools.partial(_swa_out_kernel, L=L, CS=CS, nc=nc, n=n),
        out_shape=out_shape,
        grid=(n + 1,),
        in_specs=[pl.BlockSpec((1, T, A_WIDTH), lambda s_: (cur(s_) // nc, cur(s_) % nc, q_blk)),
                  pl.BlockSpec((1, T, A_KV_WIDTH), lambda s_: (cur(s_) // nc, cur(s_) % nc, kv_blk)),
                  pl.BlockSpec((1, T, A_KV_WIDTH), lambda s_: (cur(s_) // nc, cur(s_) % nc, kv_blk + 1)),
                  pl.BlockSpec((1, WINDOW, A_KV_WIDTH), lambda s_: (cur(s_) // nc, 0, 0)),
                  pl.BlockSpec((1, WINDOW, A_KV_WIDTH), lambda s_: (cur(s_) // nc, 0, 0)),
                  pl.BlockSpec((None, 1, A_WIDTH), lambda s_: (layer, 0, 0)),
                  pl.BlockSpec((None, 1, A_KV_WIDTH), lambda s_: (layer, 0, 0)),
                  pl.BlockSpec((None, 1, A_WIDTH), lambda s_: (layer, 0, 0)),
                  full(nvar, A_KV_HEADS, lk, gl),
                  pl.BlockSpec((None, A_KV_HEADS, 1, gl), lambda s_: (layer, 0, 0, 0)),
                  full(A_KV_WIDTH, A_KV_WIDTH),
                  full(A_KV_WIDTH, A_KV_WIDTH),
                  pl.BlockSpec((T, d), lambda s_: (prev(s_), 0)),
                  pl.BlockSpec((T, M_WIDTH), lambda s_: (prev(s_), 0)),
                  pl.BlockSpec((None, M_WIDTH, d), lambda s_: (layer, 0, 0), pipeline_mode=pl.Buffered(1)),
                  pl.BlockSpec((None, A_WIDTH, d), lambda s_: (layer, 1, 0), pipeline_mode=pl.Buffered(1))],
        out_specs=(pl.BlockSpec((T, d), lambda s_: (prev(s_), 0)),
                   pl.BlockSpec((1, WINDOW, A_KV_WIDTH), lambda s_: (cur(s_) // nc, 0, 0)),
                   pl.BlockSpec((1, WINDOW, A_KV_WIDTH), lambda s_: (cur(s_) // nc, 0, 0))),
        scratch_shapes=[pltpu.VMEM((WINDOW, A_KV_WIDTH), F32),
                        pltpu.VMEM((WINDOW, A_KV_WIDTH), F32),
                        pltpu.VMEM((T, A_WIDTH), BF16)],
        compiler_params=_params(("arbitrary",)),
        name="swa_out",
    )(proj, proj, proj, zk, zk, g_q, g_k, g_ao, bias, sink, e4, eye, x, hm, w_out, w_out)


def _swa(proj, col0, ck, cv, g_q, g_k, g_ao, bias, sink, consts, layer, L, CS, cache_len):
    b, s, _ = proj.shape
    T = CS * L
    lk = WINDOW + L
    gl = A_GROUP * L
    nvar = bias.shape[0]
    q_blk, kv_blk = col0 // A_WIDTH, (col0 + A_WIDTH) // A_KV_WIDTH
    e4, eye = consts
    out_shape = (jax.ShapeDtypeStruct((b, s, A_WIDTH), BF16),
                 jax.ShapeDtypeStruct((b, WINDOW, A_KV_WIDTH), F32),
                 jax.ShapeDtypeStruct((b, WINDOW, A_KV_WIDTH), F32))
    full = lambda *shape: pl.BlockSpec(shape, lambda i, c: (0,) * len(shape))
    return pl.pallas_call(
        functools.partial(_swa_kernel, L=L, CS=CS, cache_len=cache_len),
        out_shape=out_shape,
        grid=(b, s // T),
        in_specs=[pl.BlockSpec((1, T, A_WIDTH), lambda i, c: (i, c, q_blk)),
                  pl.BlockSpec((1, T, A_KV_WIDTH), lambda i, c: (i, c, kv_blk)),
                  pl.BlockSpec((1, T, A_KV_WIDTH), lambda i, c: (i, c, kv_blk + 1)),
                  pl.BlockSpec((1, WINDOW, A_KV_WIDTH), lambda i, c: (i, 0, 0)),
                  pl.BlockSpec((1, WINDOW, A_KV_WIDTH), lambda i, c: (i, 0, 0)),
                  pl.BlockSpec((None, 1, A_WIDTH), lambda i, c: (layer, 0, 0)),
                  pl.BlockSpec((None, 1, A_KV_WIDTH), lambda i, c: (layer, 0, 0)),
                  pl.BlockSpec((None, 1, A_WIDTH), lambda i, c: (layer, 0, 0)),
                  full(nvar, A_KV_HEADS, lk, gl),
                  pl.BlockSpec((None, A_KV_HEADS, 1, gl), lambda i, c: (layer, 0, 0, 0)),
                  full(A_KV_WIDTH, A_KV_WIDTH),
                  full(A_KV_WIDTH, A_KV_WIDTH)],
        out_specs=(pl.BlockSpec((1, T, A_WIDTH), lambda i, c: (i, c, 0)),
                   pl.BlockSpec((1, WINDOW, A_KV_WIDTH), lambda i, c: (i, 0, 0)),
                   pl.BlockSpec((1, WINDOW, A_KV_WIDTH), lambda i, c: (i, 0, 0))),
        scratch_shapes=[pltpu.VMEM((WINDOW, A_KV_WIDTH), F32),
                        pltpu.VMEM((WINDOW, A_KV_WIDTH), F32)],
        compiler_params=_params(("arbitrary", "arbitrary")),
        name="swa",
    )(proj, proj, proj, ck, cv, g_q, g_k, g_ao, bias, sink, e4, eye)


def _out_proj_kernel(x_ref, hm_ref, ha_ref, wm_ref, wa_ref, o_ref):
    o_ref[...] = x_ref[...] + _dot(hm_ref[...], wm_ref[...]) + _dot(ha_ref[...], wa_ref[...])


def _out_proj(x, hm, ha, w_out, layer):
    m, d = x.shape
    tm = _row_tile(m, 512)
    return pl.pallas_call(
        _out_proj_kernel,
        out_shape=jax.ShapeDtypeStruct((m, d), F32),
        grid=(m // tm,),
        in_specs=[pl.BlockSpec((tm, d), lambda i: (i, 0)),
                  pl.BlockSpec((tm, M_WIDTH), lambda i: (i, 0)),
                  pl.BlockSpec((tm, A_WIDTH), lambda i: (i, 0)),
                  pl.BlockSpec((None, M_WIDTH, d), lambda i: (layer, 0, 0)),
                  pl.BlockSpec((None, A_WIDTH, d), lambda i: (layer, 1, 0))],
        out_specs=pl.BlockSpec((tm, d), lambda i: (i, 0)),
        compiler_params=_params(("parallel",)),
        name="out_proj",
    )(x, hm, ha, w_out, w_out)


def _ffn_kernel(x_ref, g_ref, wu_ref, wd_ref, o_ref, *rest):
    h_scr = rest[-1]
    f = pl.program_id(1)

    @pl.when(f == 0)
    def _():
        x = x_ref[...]
        h_scr[...] = _rms(x, g_ref[...]).astype(BF16)
        o_ref[...] = x

    wu, wd = wu_ref[...].astype(BF16), wd_ref[...].astype(BF16)
    if len(rest) == 3:
        rest[0][...] = wu
        rest[1][...] = wd
    u = _dot(h_scr[...], wu)
    a = jnp.square(jnp.maximum(u, 0.0)).astype(BF16)
    o_ref[...] += _dot(a, wd)


def _ffn(x, g_ffn, w_up, w_down, layer, emit_bf16=False):
    m, d = x.shape
    d_ff = w_up.shape[-1]
    tm = _row_tile(m, 512)
    tf = 512 if emit_bf16 else 1024
    assert not emit_bf16 or m == tm
    if emit_bf16:
        w_specs = [pl.BlockSpec((None, d, tf), lambda i, f: (layer, 0, f)),
                   pl.BlockSpec((None, tf, d), lambda i, f: (layer, f, 0))]
    else:
        w_specs = [pl.BlockSpec((d, tf), lambda i, f: (0, f)), pl.BlockSpec((tf, d), lambda i, f: (f, 0))]
    out_shape = [jax.ShapeDtypeStruct((m, d), F32)]
    out_specs = [pl.BlockSpec((tm, d), lambda i, f: (i, 0))]
    if emit_bf16:
        out_shape += [jax.ShapeDtypeStruct((d, d_ff), BF16), jax.ShapeDtypeStruct((d_ff, d), BF16)]
        out_specs += [pl.BlockSpec((d, tf), lambda i, f: (0, f)), pl.BlockSpec((tf, d), lambda i, f: (f, 0))]
    return pl.pallas_call(
        _ffn_kernel,
        out_shape=out_shape,
        grid=(m // tm, d_ff // tf),
        in_specs=[pl.BlockSpec((tm, d), lambda i, f: (i, 0)),
                  pl.BlockSpec((None, 1, d), lambda i, f: (layer, 0, 0))] + w_specs,
        out_specs=out_specs,
        scratch_shapes=[pltpu.VMEM((tm, d), BF16)],
        compiler_params=_params(("parallel", "arbitrary")),
        name="ffn_cast" if emit_bf16 else "ffn",
    )(x, g_ffn, w_up, w_down)


def _layer(x, wts, layer, lm, la, cs, bias, sink, ck, cv, c0, n0, m0, cache_len, ffn_w):
    b, s, d = x.shape
    x2 = x.reshape(b * s, d)
    eye = wts["consts"][1]
    if ck is None:
        hm, aproj, c_new, n_new, m_new = _proj_mlstm(x, wts["g_mix"], wts["w_all"], wts["gbias"], wts["g_mo"], eye,
                                                     c0, n0, m0, layer, lm)
        x2, k_new, v_new = _swa_out(aproj, 0, x2, hm.reshape(b * s, M_WIDTH), wts["w_out"], wts["g_q"], wts["g_k"],
                                    wts["g_ao"], bias, sink, wts["consts"], layer, la, cs)
    else:
        proj, gates = _in_proj(x2, wts["g_mix"], wts["w_all"], layer)
        proj = proj.reshape(b, s, PROJ_W)
        hm, c_new, n_new, m_new = _mlstm(proj, gates.reshape(b, s, GATE_W), wts["gbias"], wts["g_mo"], eye,
                                         c0, n0, m0, layer, lm)
        ha, k_new, v_new = _swa(proj, M_PROJ_W, ck, cv, wts["g_q"], wts["g_k"], wts["g_ao"], bias, sink, wts["consts"],
                                layer, la, cs, cache_len)
        x2 = _out_proj(x2, hm.reshape(b * s, M_WIDTH), ha.reshape(b * s, A_WIDTH), wts["w_out"], layer)
    if ffn_w is None:
        x2, *ffn_w = _ffn(x2, wts["g_ffn"], wts["w_up"], wts["w_down"], layer, emit_bf16=True)
    else:
        x2, = _ffn(x2, wts["g_ffn"], *ffn_w, layer)
    return x2.reshape(b, s, d), k_new, v_new, c_new, n_new, m_new, ffn_w


def _w_in_columns():
    o = np.cumsum((0, M_HEADS * M_DK, M_HEADS * M_DK, M_WIDTH, M_WIDTH, M_HEADS, M_HEADS, A_WIDTH, A_KV_WIDTH, A_KV_WIDTH))
    mq, mk, mv, mo, mi, mf, aq, ak, av = (int(v) for v in o[:9])
    runs = []
    for h in range(M_HEADS):
        runs += [(mq + h * M_DK, M_DK), (mk + h * M_DK, M_DK), (mv + h * M_DV, M_DV), (mo + h * M_DV, M_DV)]
    for g in range(A_GROUP):
        runs += [(aq + (kv * A_GROUP + g) * A_HEAD_DIM, A_HEAD_DIM) for kv in range(A_KV_HEADS)]
    return runs + [(ak, A_KV_WIDTH), (av, A_KV_WIDTH), (mi, M_HEADS), (mf, M_HEADS)]


def _w_in_layout_kernel(w_ref, o_ref):
    dst = 0
    for src, width in _w_in_columns():
        o_ref[:, dst:dst + width] = w_ref[:, src:src + width].astype(o_ref.dtype)
        dst += width
    o_ref[:, dst:] = jnp.zeros((o_ref.shape[0], o_ref.shape[1] - dst), o_ref.dtype)


def _w_in_layout(w_in):
    depth, d, d_in = w_in.shape
    tk = _row_tile(d, 256)
    return pl.pallas_call(
        _w_in_layout_kernel,
        out_shape=jax.ShapeDtypeStruct((depth, d, PROJ_W + GATE_W), BF16),
        grid=(depth, d // tk),
        in_specs=[pl.BlockSpec((None, tk, d_in), lambda l, i: (l, i, 0))],
        out_specs=pl.BlockSpec((None, tk, PROJ_W + GATE_W), lambda l, i: (l, i, 0)),
        compiler_params=_params(("parallel", "parallel")),
        name="w_in_layout",
    )(w_in)


def _head_consts():
    hd = np.arange(A_KV_WIDTH) // A_HEAD_DIM
    e4 = (hd[:, None] == hd[None, :]).astype(np.float32)
    eye = np.eye(A_KV_WIDTH, dtype=np.float32)
    return tuple(jnp.asarray(a, BF16) for a in (e4, eye))


def kernel(x_prompt, x_sample, cache_k, cache_v, state_C, state_n, state_m, rel_bias, g_mix, w_in, b_i, b_f, g_q, g_k, sinks, g_mo, g_ao, w_out, g_ffn, w_up, w_down):
    depth = w_in.shape[0]
    bp, sp, d = x_prompt.shape
    bs, ls, _ = x_sample.shape
    n_win = cache_k.shape[2]
    assert sp % CHUNK == 0 and n_win == WINDOW and ls % 16 == 0 and ls <= CHUNK
    lm = M_CHUNK if sp % M_CHUNK == 0 else CHUNK

    w_all = _w_in_layout(w_in)
    gbias = jnp.concatenate([b_i, b_f, jnp.zeros((depth, GATE_W - 2 * M_HEADS), F32)], axis=-1)

    def regroup(a):
        rest = a.shape[2:]
        a = a.reshape(depth, A_KV_HEADS, A_GROUP, A_HEAD_DIM, *rest)
        return jnp.swapaxes(a, 1, 2).reshape(depth, A_WIDTH, *rest)

    w_out_b = jnp.concatenate([w_out[:, :M_WIDTH], regroup(w_out[:, M_WIDTH:])], axis=1).astype(BF16)
    wts = {
        "g_mix": g_mix.reshape(depth, 1, d), "w_all": w_all,
        "gbias": gbias.reshape(depth, 1, GATE_W), "g_mo": g_mo.reshape(depth, 1, M_WIDTH),
        "g_q": jnp.tile(g_q * A_SCALE, (1, A_HEADS)).reshape(depth, 1, A_WIDTH),
        "g_k": jnp.tile(g_k, (1, A_KV_HEADS)).reshape(depth, 1, A_KV_WIDTH),
        "g_ao": regroup(g_ao).reshape(depth, 1, A_WIDTH),
        "w_out": w_out_b, "g_ffn": g_ffn.reshape(depth, 1, d),
        "w_up": w_up, "w_down": w_down,
        "consts": _head_consts(),
    }

    bias = _bias_table(rel_bias)

    def bias_t(lq, cache_len):
        lk = WINDOW + lq
        t = bias[:, :lq, :lk].reshape(A_KV_HEADS, A_GROUP, lq, lk)
        t = t.transpose(0, 3, 1, 2).reshape(A_KV_HEADS, lk, A_GROUP * lq)
        n_prev = np.arange((WINDOW - cache_len) // lq + 1) * lq + cache_len
        dead = np.arange(lk)[None, :] < WINDOW - n_prev[:, None]
        return t[None] + jnp.asarray(np.where(dead, NEG_INF, 0.0)[:, None, :, None], F32)

    def sink_rows(lq):
        return jnp.repeat(sinks.reshape(depth, A_KV_HEADS, 1, A_GROUP), lq, axis=-1)

    zc =jnp.zeros((bp, M_HEADS, M_DV, M_DK), F32)
    zn = jnp.zeros((bp, M_HEADS, M_DK), F32)
    zm = jnp.zeros((bp, 1, M_HEADS), F32)

    xp, xs = x_prompt, x_sample
    bias_p, bias_s, sink_p, sink_s = bias_t(CHUNK, 0), bias_t(ls, WINDOW), sink_rows(CHUNK), sink_rows(ls)
    cs = A_CHUNKS_PER_STEP if sp % (A_CHUNKS_PER_STEP * CHUNK) == 0 else 1
    outs_p, outs_s = [], []
    for l in range(depth):
        xs, *st, ffn_w = _layer(xs, wts, l, ls, ls, 1, bias_s, sink_s,
                                cache_k[l].reshape(bs, n_win, A_KV_WIDTH), cache_v[l].reshape(bs, n_win, A_KV_WIDTH),
                                state_C[l], state_n[l], state_m[l].reshape(bs, 1, M_HEADS), WINDOW, None)
        outs_s.append(st)
        xp, *st, _ = _layer(xp, wts, l, lm, CHUNK, cs, bias_p, sink_p, None, None, zc, zn, zm, 0, ffn_w)
        outs_p.append(st)

    def stack(outs, b):
        k, v, c, n, m = (jnp.stack([o_[i] for o_ in outs]) for i in range(5))
        return (k.reshape(depth, b, WINDOW, A_KV_HEADS, A_HEAD_DIM), v.reshape(depth, b, WINDOW, A_KV_HEADS, A_HEAD_DIM),
                c, n, m.reshape(depth, b, M_HEADS))

    return (xp, xs) + stack(outs_p, bp) + stack(outs_s, bs)
```
